```python
import jax, jax.numpy as jnp
from jax import lax
import numpy as np

D_MODEL = 2048
BATCH = 16
SEQ = 256
DEPTH = 1
DEC_BATCH = 4
DEC_SEQ = 1024
PAST_LEN = 512

GRID_W = 64
N_HEADS = 16
N_KV_HEADS = 4
HEAD_DIM = D_MODEL // N_HEADS
AXIS_DIM = HEAD_DIM // 2
ROPE_THETA = 10000.0
Q_BLOCK = 128
D_Q = N_HEADS * HEAD_DIM
D_KV = N_KV_HEADS * HEAD_DIM
D_CONV = D_MODEL // 2
CONV_WIDTH = 3
N_EXPERTS = 64
N_GROUPS = 8
TOPK_GROUPS = 4
TOP_K = 8
D_EXPERT = D_MODEL // 4
D_SHARED = D_EXPERT
ROUTED_SCALE = 2.5
MOE_BLOCK = 128
EPS = 1e-6
D_IN = 3 * D_CONV + D_Q + 2 * D_KV + 2 * D_MODEL
SPLIT_POINTS = (D_CONV, 2 * D_CONV, 3 * D_CONV, 3 * D_CONV + D_Q, 3 * D_CONV + D_Q + D_KV,
                3 * D_CONV + D_Q + 2 * D_KV, 3 * D_CONV + D_Q + 2 * D_KV + D_MODEL)

kernel_name = "hybrid_diffusion_prefix_conv_gqa_moe_step"


def rms_norm(x, g):
    xf = x.astype(jnp.float32)
    y = xf * lax.rsqrt(jnp.mean(xf * xf, axis=-1, keepdims=True) + EPS)
    return (y * g.astype(jnp.float32)).astype(x.dtype)


def axial_rope(n_rows):
    row = jnp.repeat(jnp.arange(n_rows), GRID_W).astype(jnp.float32)
    col = jnp.tile(jnp.arange(GRID_W), n_rows).astype(jnp.float32)
    inv = ROPE_THETA ** (-jnp.arange(0, AXIS_DIM, 2, dtype=jnp.float32) / AXIS_DIM)
    ang = jnp.concatenate([row[:, None] * inv, col[:, None] * inv], axis=-1)
    return jnp.cos(ang), jnp.sin(ang)


def apply_rope(x, cos, sin):
    xf = x.astype(jnp.float32)
    x1, x2 = xf[..., 0::2], xf[..., 1::2]
    c, s = cos[None, :, None, :], sin[None, :, None, :]
    out = jnp.stack([x1 * c - x2 * s, x1 * s + x2 * c], axis=-1).reshape(x.shape)
    return out.astype(x.dtype)


def block_attention(q, k, v):
    B, Sq, H, Dh = q.shape
    KVH = k.shape[2]
    G = H // KVH
    nb = Sq // Q_BLOCK
    qb = q.reshape(B, nb, Q_BLOCK, KVH, G, Dh).transpose(1, 0, 2, 3, 4, 5)
    scale = Dh ** -0.5

    def one_block(qi):
        s = jnp.einsum('bqkgd,bskd->bkgqs', qi, k, preferred_element_type=jnp.float32) * scale
        p = jax.nn.softmax(s, axis=-1)
        return jnp.einsum('bkgqs,bskd->bqkgd', p.astype(v.dtype), v)

    ob = lax.map(one_block, qb)
    return ob.transpose(1, 0, 2, 3, 4, 5).reshape(B, Sq, H * Dh)


def short_conv(u, w, b):
    out = lax.conv_general_dilated(u, w[:, None, :].astype(u.dtype), window_strides=(1,),
                                   padding=((1, 1),), dimension_numbers=('NWC', 'WIO', 'NWC'),
                                   feature_group_count=u.shape[-1])
    return out + b


def swiglu(x, wg, wu, wd):
    return (jax.nn.silu(x @ wg) * (x @ wu)) @ wd


def routed_experts(h, idx, w_sel, w_gate, w_up, w_down):
    T, D = h.shape
    TK = T * TOP_K
    e_flat = idx.reshape(-1)
    tok_flat = jnp.repeat(jnp.arange(T, dtype=jnp.int32), TOP_K)
    w_flat = w_sel.reshape(-1)
    order = jnp.argsort(e_flat)
    e_sorted, tok_sorted, w_sorted = e_flat[order], tok_flat[order], w_flat[order]
    counts = jnp.bincount(e_flat, length=N_EXPERTS)
    start = jnp.cumsum(counts) - counts
    padded = (counts + MOE_BLOCK - 1) // MOE_BLOCK * MOE_BLOCK
    pstart = jnp.cumsum(padded) - padded
    pend = pstart + padded
    dest = pstart[e_sorted] + (jnp.arange(TK) - start[e_sorted])
    n_blocks = (TK + N_EXPERTS * (MOE_BLOCK - 1)) // MOE_BLOCK + 1
    n_slots = n_blocks * MOE_BLOCK
    slot_tok = jnp.full((n_slots,), T, jnp.int32).at[dest].set(tok_sorted)
    slot_w = jnp.zeros((n_slots,), w_sorted.dtype).at[dest].set(w_sorted)
    block_expert = jnp.minimum(
        jnp.searchsorted(pend, jnp.arange(n_blocks) * MOE_BLOCK, side='right'), N_EXPERTS - 1)
    h_pad = jnp.concatenate([h, jnp.zeros((1, D), h.dtype)], axis=0)
    xb = h_pad[slot_tok].reshape(n_blocks, MOE_BLOCK, D)

    def expert_block(args):
        xi, e = args
        return swiglu(xi, w_gate[e], w_up[e], w_down[e])

    yb = lax.map(expert_block, (xb, block_expert))
    y = yb.reshape(n_slots, D) * slot_w[:, None]
    return jax.ops.segment_sum(y, slot_tok, num_segments=T + 1)[:T].astype(h.dtype)


def moe_ffn(h, p):
    T = h.shape[0]
    logits = jnp.einsum('td,de->te', h, p['w_router'], preferred_element_type=jnp.float32)
    scores = jax.nn.sigmoid(logits)
    biased = scores + p['b_router'].astype(jnp.float32)
    grp_score = lax.top_k(biased.reshape(T, N_GROUPS, N_EXPERTS // N_GROUPS), 2)[0].sum(-1)
    _, top_g = lax.top_k(grp_score, TOPK_GROUPS)
    gmask = jax.nn.one_hot(top_g, N_GROUPS, dtype=jnp.float32).sum(1) > 0
    emask = jnp.repeat(gmask, N_EXPERTS // N_GROUPS, axis=1)
    _, idx = lax.top_k(jnp.where(emask, biased, -jnp.inf), TOP_K)
    w_sel = jnp.take_along_axis(scores, idx, axis=1)
    w_sel = w_sel / jnp.sum(w_sel, axis=-1, keepdims=True) * ROUTED_SCALE
    routed = routed_experts(h, idx, w_sel, p['w_e_gate'], p['w_e_up'], p['w_e_down'])
    shared = swiglu(h, p['w_s_gate'], p['w_s_up'], p['w_s_down'])
    return routed + shared


def trunk_layer(x, mod, p, rope=None, ctx_k=None, ctx_v=None):
    B, S, D = x.shape
    shift1, scale1, gate1, shift2, scale2, gate2 = jnp.split(mod, 6, axis=-1)
    h = rms_norm(x, p['g_pre1']) * (1 + scale1) + shift1
    z = jnp.einsum('bsd,de->bse', h, p['w_in'])
    u, gb, gc, q, k, v, g_cv, g_at = jnp.split(z, SPLIT_POINTS, axis=-1)
    conv_out = jnp.einsum('bsc,cd->bsd', gb * short_conv(gc * u, p['conv_w'], p['conv_b']),
                          p['w_conv_out'])
    q = rms_norm(q.reshape(B, S, N_HEADS, HEAD_DIM), p['g_q'])
    k = rms_norm(k.reshape(B, S, N_KV_HEADS, HEAD_DIM), p['g_k'])
    v = v.reshape(B, S, N_KV_HEADS, HEAD_DIM)
    if rope is None:
        k_all, v_all = k, v
    else:
        cos, sin = rope
        q = apply_rope(q, cos, sin)
        k = apply_rope(k, cos, sin)
        k_all = jnp.concatenate([ctx_k.astype(k.dtype), k], axis=1)
        v_all = jnp.concatenate([ctx_v.astype(v.dtype), v], axis=1)
    attn_out = jnp.einsum('bse,ed->bsd', block_attention(q, k_all, v_all), p['w_attn_out'])
    merged = jax.nn.sigmoid(g_cv) * conv_out + jax.nn.sigmoid(g_at) * attn_out
    mix = jnp.einsum('bsd,de->bse', merged, p['w_out'])
    x = x + gate1 * rms_norm(mix, p['g_post1'])
    h2 = rms_norm(x, p['g_pre2']) * (1 + scale2) + shift2
    ffn = moe_ffn(h2.reshape(B * S, D), p).reshape(B, S, D)
    x = x + gate2 * rms_norm(ffn, p['g_post2'])
    return x, k, v


def setup_inputs(seed: int = 0) -> dict:
    key = jax.random.key(seed)
    ks = jax.random.split(key, 32)
    f32 = jnp.float32

    def nrm(k, shape, scale):
        return jax.random.normal(k, shape, f32) * scale

    def gain(k, shape):
        return 1.0 + 0.05 * jax.random.normal(k, shape, f32)

    L = DEPTH
    return {
        'x_prompt': nrm(ks[0], (BATCH, SEQ, D_MODEL), 1.0),
        'x_sample': nrm(ks[1], (DEC_BATCH, DEC_SEQ, D_MODEL), 1.0),
        'cache_k': nrm(ks[2], (DEC_BATCH, DEPTH, PAST_LEN, N_KV_HEADS, HEAD_DIM), 1.0),
        'cache_v': nrm(ks[3], (DEC_BATCH, DEPTH, PAST_LEN, N_KV_HEADS, HEAD_DIM), 1.0),
        'c': nrm(ks[4], (DEC_BATCH, D_MODEL), 1.0),
        'c_ctx': nrm(ks[5], (D_MODEL,), 1.0),
        'w_mod': nrm(ks[6], (L, D_MODEL, 6 * D_MODEL), 0.5 * D_MODEL ** -0.5),
        'b_mod': nrm(ks[7], (L, 6 * D_MODEL), 0.02),
        'g_pre1': gain(ks[8], (L, D_MODEL)),
        'w_in': nrm(ks[9], (L, D_MODEL, D_IN), D_MODEL ** -0.5),
        'conv_w': nrm(ks[10], (L, CONV_WIDTH, D_CONV), CONV_WIDTH ** -0.5),
        'conv_b': nrm(ks[11], (L, D_CONV), 0.02),
        'g_q': gain(ks[12], (L, HEAD_DIM)),
        'g_k': gain(ks[13], (L, HEAD_DIM)),
        'w_conv_out': nrm(ks[14], (L, D_CONV, D_MODEL), D_CONV ** -0.5),
        'w_attn_out': nrm(ks[15], (L, D_Q, D_MODEL), D_Q ** -0.5),
        'w_out': nrm(ks[16], (L, D_MODEL, D_MODEL), D_MODEL ** -0.5),
        'g_post1': gain(ks[17], (L, D_MODEL)),
        'g_pre2': gain(ks[18], (L, D_MODEL)),
        'w_router': nrm(ks[19], (L, D_MODEL, N_EXPERTS), D_MODEL ** -0.5),
        'b_router': nrm(ks[20], (L, N_EXPERTS), 0.01),
        'w_e_gate': nrm(ks[21], (L, N_EXPERTS, D_MODEL, D_EXPERT), D_MODEL ** -0.5),
        'w_e_up': nrm(ks[22], (L, N_EXPERTS, D_MODEL, D_EXPERT), D_MODEL ** -0.5),
        'w_e_down': nrm(ks[23], (L, N_EXPERTS, D_EXPERT, D_MODEL), D_EXPERT ** -0.5),
        'w_s_gate': nrm(ks[24], (L, D_MODEL, D_SHARED), D_MODEL ** -0.5),
        'w_s_up': nrm(ks[25], (L, D_MODEL, D_SHARED), D_MODEL ** -0.5),
        'w_s_down': nrm(ks[26], (L, D_SHARED, D_MODEL), D_SHARED ** -0.5),
        'g_post2': gain(ks[27], (L, D_MODEL)),
    }


def reference(x_prompt, x_sample, cache_k, cache_v, c, c_ctx, w_mod, b_mod, g_pre1, w_in,
              conv_w, conv_b, g_q, g_k, w_conv_out, w_attn_out, w_out, g_post1, g_pre2,
              w_router, b_router, w_e_gate, w_e_up, w_e_down, w_s_gate, w_s_up, w_s_down,
              g_post2):
    n_rows = x_sample.shape[1] // GRID_W
    rope = axial_rope(n_rows)
    y_prompt, y_sample = x_prompt, x_sample
    new_k, new_v = [], []
    for l in range(DEPTH):
        p = {
            'g_pre1': g_pre1[l], 'w_in': w_in[l], 'conv_w': conv_w[l], 'conv_b': conv_b[l],
            'g_q': g_q[l], 'g_k': g_k[l], 'w_conv_out': w_conv_out[l],
            'w_attn_out': w_attn_out[l], 'w_out': w_out[l], 'g_post1': g_post1[l],
            'g_pre2': g_pre2[l], 'w_router': w_router[l], 'b_router': b_router[l],
            'w_e_gate': w_e_gate[l], 'w_e_up': w_e_up[l], 'w_e_down': w_e_down[l],
            'w_s_gate': w_s_gate[l], 'w_s_up': w_s_up[l], 'w_s_down': w_s_down[l],
            'g_post2': g_post2[l],
        }
        mod_ctx = (jax.nn.silu(c_ctx) @ w_mod[l] + b_mod[l])[None, None, :]
        mod_lat = (jax.nn.silu(c) @ w_mod[l] + b_mod[l])[:, None, :]
        y_prompt, k_ctx, v_ctx = trunk_layer(y_prompt, mod_ctx, p)
        new_k.append(k_ctx)
        new_v.append(v_ctx)
        y_sample, _, _ = trunk_layer(y_sample, mod_lat, p, rope, cache_k[:, l], cache_v[:, l])
    new_cache_k = jnp.stack(new_k, axis=1)
    new_cache_v = jnp.stack(new_v, axis=1)
    return (y_prompt, y_sample, new_cache_k, new_cache_v)
```

```python
import functools

import jax
import jax.numpy as jnp
from jax import lax
from jax.experimental import pallas as pl
from jax.experimental.pallas import tpu as pltpu

GRID_W = 64
ROPE_THETA = 10000.0
N_GROUPS = 8
TOPK_GROUPS = 4
TOP_K = 8
ROUTED_SCALE = 2.5
EPS = 1e-6

LANES = 128
SUBLANES = 8
V7X_VMEM_BYTES = 64 * 1024 * 1024
MIB = 1024 * 1024

F32 = jnp.float32
BF16 = jnp.bfloat16
I32 = jnp.int32
HI_MASK = -65536


def _params(n_grid, vmem_mib):
    assert vmem_mib * MIB < V7X_VMEM_BYTES
    return pltpu.CompilerParams(
        dimension_semantics=("arbitrary",) * n_grid, vmem_limit_bytes=vmem_mib * MIB)


def _silu(x):
    return x * jax.nn.sigmoid(x)


def _rms(x, g):
    return x * lax.rsqrt(jnp.mean(x * x, axis=-1, keepdims=True) + EPS) * g


def _pack_rows(val, out_ref, rows):
    half = val.shape[1] // 2
    for s in range(half // LANES):
        lo = val[:, s * LANES:(s + 1) * LANES].astype(BF16).astype(F32)
        hi = val[:, half + s * LANES:half + (s + 1) * LANES].astype(BF16).astype(F32)
        word = lax.shift_right_logical(pltpu.bitcast(lo, I32), 16) | (pltpu.bitcast(hi, I32) & HI_MASK)
        out_ref[pl.ds(s, rows, stride=SUBLANES), :] = word


def _unpack_word(word):
    lo = pltpu.bitcast(lax.shift_left(word, 16), F32)
    hi = pltpu.bitcast(word & HI_MASK, F32)
    return lo, hi


def _mod_kernel(c_ref, w_ref, b_ref, o_ref):
    s = _silu(c_ref[...]).astype(BF16)
    o_ref[...] = jnp.dot(s, w_ref[...].astype(BF16), preferred_element_type=F32) + b_ref[...]


def _modulation(cond, w, b):
    rows, d = cond.shape
    n = w.shape[1]
    tn = 1024
    return pl.pallas_call(
        _mod_kernel,
        grid=(n // tn,),
        in_specs=[pl.BlockSpec((rows, d), lambda j: (0, 0)),
                  pl.BlockSpec((d, tn), lambda j: (0, j)),
                  pl.BlockSpec((1, tn), lambda j: (0, j))],
        out_specs=pl.BlockSpec((rows, tn), lambda j: (0, j)),
        out_shape=jax.ShapeDtypeStruct((rows, n), F32),
        compiler_params=_params(1, 40),
        name="modulation",
    )(cond, w, b)


def _prenorm_kernel(x_ref, shift_ref, scale_ref, g_ref, o_ref, *, tiles_per_row):
    r = pl.program_id(0) // tiles_per_row
    y = _rms(x_ref[...], g_ref[...])
    o_ref[...] = (y * (1.0 + scale_ref[pl.ds(r, 1), :]) + shift_ref[pl.ds(r, 1), :]).astype(o_ref.dtype)


def _prenorm(x, mod, g, shift_col, scale_col):
    t, d = x.shape
    nb = mod.shape[0]
    tm = 512
    return pl.pallas_call(
        functools.partial(_prenorm_kernel, tiles_per_row=t // nb // tm),
        grid=(t // tm,),
        in_specs=[pl.BlockSpec((tm, d), lambda i: (i, 0)),
                  pl.BlockSpec((nb, d), lambda i: (0, shift_col)),
                  pl.BlockSpec((nb, d), lambda i: (0, scale_col)),
                  pl.BlockSpec((1, d), lambda i: (0, 0))],
        out_specs=pl.BlockSpec((tm, d), lambda i: (i, 0)),
        out_shape=jax.ShapeDtypeStruct((t, d), BF16),
        compiler_params=_params(1, 32),
        name="prenorm",
    )(x, mod, mod, g)


def _mm_wcast_kernel(a_ref, w_ref, o_ref, wb_ref):
    @pl.when(pl.program_id(1) == 0)
    def _():
        wb_ref[...] = w_ref[...].astype(BF16)

    o_ref[...] = jnp.dot(a_ref[...], wb_ref[...], preferred_element_type=F32).astype(o_ref.dtype)


def _matmul_wcast(a, w, col_map, n_out, out_dtype):
    m, k = a.shape
    tm, tn = 1024, 1024
    return pl.pallas_call(
        _mm_wcast_kernel,
        grid=(n_out // tn, m // tm),
        in_specs=[pl.BlockSpec((tm, k), lambda j, i: (i, 0)),
                  pl.BlockSpec((k, tn), lambda j, i: (0, col_map(j)))],
        out_specs=pl.BlockSpec((tm, tn), lambda j, i: (i, j)),
        out_shape=jax.ShapeDtypeStruct((m, n_out), out_dtype),
        scratch_shapes=[pltpu.VMEM((k, tn), BF16)],
        compiler_params=_params(2, 48),
        name="in_proj",
    )(a, w)


def _z_col_map(j):
    return jnp.where(j < 2, j + 3, jnp.where(j < 6, j + 4, j - 6))


def _qk_kernel(*refs, use_rope, n_heads, n_kv, q_scale):
    if use_rope:
        q_ref, kv_ref, gq_ref, gk_ref, cos_ref, sin_ref, qn_ref, kn_ref, vb_ref, k32_ref, v32_ref = refs
        cos = cos_ref[...]
        sin = sin_ref[...]
        even = lax.broadcasted_iota(I32, cos.shape, 1) % 2 == 0
    else:
        q_ref, kv_ref, gq_ref, gk_ref, qn_ref, kn_ref, vb_ref, k32_ref, v32_ref = refs

    def norm_rope(xh, g):
        y = _rms(xh, g)
        if use_rope:
            sw = jnp.where(even, pltpu.roll(y, LANES - 1, 1), pltpu.roll(y, 1, 1))
            y = y * cos + sw * sin
        return y

    gq = gq_ref[...]
    gk = gk_ref[...]
    for h in range(n_heads):
        sl = slice(h * LANES, (h + 1) * LANES)
        qn_ref[:, sl] = (norm_rope(q_ref[:, sl].astype(F32), gq) * q_scale).astype(BF16)
    kw = n_kv * LANES
    for h in range(n_kv):
        sl = slice(h * LANES, (h + 1) * LANES)
        kh = norm_rope(kv_ref[:, sl], gk)
        k32_ref[:, sl] = kh
        kn_ref[:, sl] = kh.astype(BF16)
    v = kv_ref[:, kw:2 * kw]
    v32_ref[...] = v
    vb_ref[...] = v.astype(BF16)


def _qk_prep(z, kv, gq, gk, rope, seq_len, n_heads, n_kv):
    t = z.shape[0]
    dq = n_heads * LANES
    dk = n_kv * LANES
    tm = 256
    in_specs = [pl.BlockSpec((tm, dq), lambda i: (i, 0)),
                pl.BlockSpec((tm, 2 * dk), lambda i: (i, 0)),
                pl.BlockSpec((1, LANES), lambda i: (0, 0)),
                pl.BlockSpec((1, LANES), lambda i: (0, 0))]
    args = [z, kv, gq, gk]
    if rope is not None:
        per_seq = seq_len // tm
        in_specs += [pl.BlockSpec((tm, LANES), lambda i: (i % per_seq, 0))] * 2
        args += list(rope)
    out_specs = [pl.BlockSpec((tm, dq), lambda i: (i, 0))] + [pl.BlockSpec((tm, dk), lambda i: (i, 0))] * 4
    out_shape = [jax.ShapeDtypeStruct((t, dq), BF16), jax.ShapeDtypeStruct((t, dk), BF16),
                 jax.ShapeDtypeStruct((t, dk), BF16), jax.ShapeDtypeStruct((t, dk), F32),
                 jax.ShapeDtypeStruct((t, dk), F32)]
    return pl.pallas_call(
        functools.partial(_qk_kernel, use_rope=rope is not None, n_heads=n_heads, n_kv=n_kv,
                          q_scale=LANES ** -0.5),
        grid=(t // tm,),
        in_specs=in_specs, out_specs=out_specs, out_shape=out_shape,
        compiler_params=_params(1, 32),
        name="qk_prep",
    )(*args)


def _attn_kernel(*refs, has_cache, group):
    if has_cache:
        q_ref, k_ref, v_ref, ck_ref, cv_ref, o_ref = refs
    else:
        q_ref, k_ref, v_ref, o_ref = refs
    tq = q_ref.shape[0]
    nt = (((1,), (1,)), ((), ()))
    q = jnp.concatenate([q_ref[:, g * LANES:(g + 1) * LANES] for g in range(group)], axis=0)
    s_own = lax.dot_general(q, k_ref[...], nt, preferred_element_type=F32)
    m = jnp.max(s_own, axis=-1, keepdims=True)
    if has_cache:
        s_ctx = lax.dot_general(q, ck_ref[...].astype(BF16), nt, preferred_element_type=F32)
        m = jnp.maximum(m, jnp.max(s_ctx, axis=-1, keepdims=True))
    p = jnp.exp(s_own - m)
    denom = jnp.sum(p, axis=-1, keepdims=True)
    acc = jnp.dot(p.astype(BF16), v_ref[...], preferred_element_type=F32)
    if has_cache:
        pc = jnp.exp(s_ctx - m)
        denom = denom + jnp.sum(pc, axis=-1, keepdims=True)
        acc = acc + jnp.dot(pc.astype(BF16), cv_ref[...].astype(BF16), preferred_element_type=F32)
    o = acc / denom
    for g in range(group):
        o_ref[:, g * LANES:(g + 1) * LANES] = o[g * tq:(g + 1) * tq].astype(o_ref.dtype)


def _attention(qn, kn, vb, cache, batch, seq_len, n_heads, n_kv, tq):
    t = qn.shape[0]
    group = n_heads // n_kv
    nq = seq_len // tq
    in_specs = [pl.BlockSpec((tq, group * LANES), lambda b, h, i: (b * nq + i, h)),
                pl.BlockSpec((seq_len, LANES), lambda b, h, i: (b, h)),
                pl.BlockSpec((seq_len, LANES), lambda b, h, i: (b, h))]
    args = [qn, kn, vb]
    if cache is not None:
        past = cache[0].shape[1]
        in_specs += [pl.BlockSpec((None, past, LANES), lambda b, h, i: (b, 0, h))] * 2
        args += list(cache)
    return pl.pallas_call(
        functools.partial(_attn_kernel, has_cache=cache is not None, group=group),
        grid=(batch, n_kv, nq),
        in_specs=in_specs,
        out_specs=pl.BlockSpec((tq, group * LANES), lambda b, h, i: (b * nq + i, h)),
        out_shape=jax.ShapeDtypeStruct((t, n_heads * LANES), BF16),
        compiler_params=_params(3, 40),
        name="attention",
    )(*args)


CONV_HALO = 16


def _conv_kernel(u_ref, b_ref, c_ref, up_ref, cp_ref, un_ref, cn_ref, w_ref, bias_ref, o_ref, *,
                 tiles_per_seq):
    tm = u_ref.shape[0]
    pos = pl.program_id(0) % tiles_per_seq
    cu = c_ref[...].astype(F32) * u_ref[...].astype(F32)
    halo_prev = (cp_ref[...].astype(F32) * up_ref[...].astype(F32))[CONV_HALO - 1:CONV_HALO, :]
    halo_next = (cn_ref[...].astype(F32) * un_ref[...].astype(F32))[0:1, :]
    halo_prev = jnp.where(pos == 0, 0.0, halo_prev)
    halo_next = jnp.where(pos == tiles_per_seq - 1, 0.0, halo_next)
    row = lax.broadcasted_iota(I32, cu.shape, 0)
    prev = jnp.where(row == 0, halo_prev, pltpu.roll(cu, 1, 0))
    nxt = jnp.where(row == tm - 1, halo_next, pltpu.roll(cu, tm - 1, 0))
    w = w_ref[...]
    conv = prev * w[0:1, :] + cu * w[1:2, :] + nxt * w[2:3, :] + bias_ref[...]
    o_ref[...] = (b_ref[...].astype(F32) * conv).astype(o_ref.dtype)


def _gated_conv(z, conv_w, conv_b, seq_len, col0):
    t = z.shape[0]
    dc = conv_w.shape[1]
    tm = 256
    hb = tm // CONV_HALO
    last = t // CONV_HALO - 1
    prev_map = lambda c: (lambda i: (jnp.maximum(i * hb - 1, 0), c))
    next_map = lambda c: (lambda i: (jnp.minimum((i + 1) * hb, last), c))
    in_specs = [pl.BlockSpec((tm, dc), lambda i: (i, col0)),
                pl.BlockSpec((tm, dc), lambda i: (i, col0 + 1)),
                pl.BlockSpec((tm, dc), lambda i: (i, col0 + 2)),
                pl.BlockSpec((CONV_HALO, dc), prev_map(col0)),
                pl.BlockSpec((CONV_HALO, dc), prev_map(col0 + 2)),
                pl.BlockSpec((CONV_HALO, dc), next_map(col0)),
                pl.BlockSpec((CONV_HALO, dc), next_map(col0 + 2)),
                pl.BlockSpec(conv_w.shape, lambda i: (0, 0)),
                pl.BlockSpec((1, dc), lambda i: (0, 0))]
    return pl.pallas_call(
        functools.partial(_conv_kernel, tiles_per_seq=seq_len // tm),
        grid=(t // tm,),
        in_specs=in_specs,
        out_specs=pl.BlockSpec((tm, dc), lambda i: (i, 0)),
        out_shape=jax.ShapeDtypeStruct((t, dc), BF16),
        compiler_params=_params(1, 32),
        name="gated_conv",
    )(z, z, z, z, z, z, z, conv_w, conv_b)


def _merge_kernel(cv_ref, at_ref, gc_ref, ga_ref, wc_ref, wa_ref, o_ref, wcb_ref, wab_ref):
    @pl.when(pl.program_id(1) == 0)
    def _():
        wcb_ref[...] = wc_ref[...].astype(BF16)
        wab_ref[...] = wa_ref[...].astype(BF16)

    conv_out = jnp.dot(cv_ref[...], wcb_ref[...], preferred_element_type=F32)
    attn_out = jnp.dot(at_ref[...], wab_ref[...], preferred_element_type=F32)
    merged = (jax.nn.sigmoid(gc_ref[...].astype(F32)) * conv_out
              + jax.nn.sigmoid(ga_ref[...].astype(F32)) * attn_out)
    o_ref[...] = merged.astype(o_ref.dtype)


def _merge(cvg, attn, z, w_conv_out, w_attn_out, gc_col0, ga_col0):
    t, dc = cvg.shape
    dq = attn.shape[1]
    d = w_conv_out.shape[1]
    tm, tn = 512, 512
    return pl.pallas_call(
        _merge_kernel,
        grid=(d // tn, t // tm),
        in_specs=[pl.BlockSpec((tm, dc), lambda j, i: (i, 0)),
                  pl.BlockSpec((tm, dq), lambda j, i: (i, 0)),
                  pl.BlockSpec((tm, tn), lambda j, i: (i, gc_col0 + j)),
                  pl.BlockSpec((tm, tn), lambda j, i: (i, ga_col0 + j)),
                  pl.BlockSpec((dc, tn), lambda j, i: (0, j)),
                  pl.BlockSpec((dq, tn), lambda j, i: (0, j))],
        out_specs=pl.BlockSpec((tm, tn), lambda j, i: (i, j)),
        out_shape=jax.ShapeDtypeStruct((t, d), BF16),
        scratch_shapes=[pltpu.VMEM((dc, tn), BF16), pltpu.VMEM((dq, tn), BF16)],
        compiler_params=_params(2, 40),
        name="merge",
    )(cvg, attn, z, z, w_conv_out, w_attn_out)


def _route(logits_t, bias_col):
    n_exp, tm = logits_t.shape
    per = n_exp // N_GROUPS
    assert per == SUBLANES
    neg = -jnp.inf
    scores = jax.nn.sigmoid(logits_t)
    biased = scores + bias_col
    sub = lax.broadcasted_iota(I32, (per, tm), 0).astype(F32)
    xs = [biased[g * per:(g + 1) * per, :] for g in range(N_GROUPS)]
    sc = [scores[g * per:(g + 1) * per, :] for g in range(N_GROUPS)]
    ids = [sub + float(g * per) for g in range(N_GROUPS)]

    def colmax(a):
        return jnp.max(a, axis=0, keepdims=True)

    def colmin(a):
        return jnp.min(a, axis=0, keepdims=True)

    rows = []
    for g in range(N_GROUPS):
        m1 = colmax(xs[g])
        j1 = colmin(jnp.where(xs[g] == m1, sub, float(per)))
        m2 = colmax(jnp.where(sub == j1, neg, xs[g]))
        rows.append(m1 + m2)
    gs = jnp.concatenate(rows, axis=0)
    gsel = jnp.zeros_like(gs)
    for _ in range(TOPK_GROUPS):
        m = colmax(gs)
        j = colmin(jnp.where(gs == m, sub, float(N_GROUPS)))
        hit = sub == j
        gsel = jnp.where(hit, 1.0, gsel)
        gs = jnp.where(hit, neg, gs)
    masked = [jnp.where(gsel[g:g + 1, :] > 0.0, xs[g], neg) for g in range(N_GROUPS)]
    idx_rows, w_rows = [], []
    for _ in range(TOP_K):
        mm = masked[0]
        for g in range(1, N_GROUPS):
            mm = jnp.maximum(mm, masked[g])
        m = colmax(mm)
        idx = colmin(jnp.where(masked[0] == m, ids[0], float(n_exp)))
        for g in range(1, N_GROUPS):
            idx = jnp.minimum(idx, colmin(jnp.where(masked[g] == m, ids[g], float(n_exp))))
        wk = jnp.zeros_like(idx)
        for g in range(N_GROUPS):
            hit = ids[g] == idx
            wk = wk + jnp.sum(jnp.where(hit, sc[g], 0.0), axis=0, keepdims=True)
            masked[g] = jnp.where(hit, neg, masked[g])
        idx_rows.append(idx)
        w_rows.append(wk)
    idx = jnp.concatenate(idx_rows, axis=0).astype(I32)
    w = jnp.concatenate(w_rows, axis=0)
    w = w / jnp.sum(w, axis=0, keepdims=True) * ROUTED_SCALE
    return idx, w


def _out_kernel(mg_ref, x_ref, gate_ref, shift_ref, scale_ref, gpost_ref, gpre_ref, wo_ref, wr_ref, br_ref,
                x1_ref, h2b_ref, h2p_ref, idx_ref, wsel_ref, *, tiles_per_row):
    tm = x_ref.shape[0]
    r = pl.program_id(0) // tiles_per_row
    mix = jnp.dot(mg_ref[...], wo_ref[...], preferred_element_type=F32)
    x1 = x_ref[...] + gate_ref[pl.ds(r, 1), :] * _rms(mix, gpost_ref[...])
    x1_ref[...] = x1
    h2 = _rms(x1, gpre_ref[...]) * (1.0 + scale_ref[pl.ds(r, 1), :]) + shift_ref[pl.ds(r, 1), :]
    h2b_ref[...] = h2.astype(BF16)
    _pack_rows(h2, h2p_ref, tm)
    logits_t = lax.dot_general(wr_ref[...], h2, (((1,), (1,)), ((), ())), preferred_element_type=F32,
                               precision=lax.Precision.HIGHEST)
    idx, w = _route(logits_t, br_ref[...])
    idx_ref[...] = idx
    wsel_ref[...] = w


def _out_proj(merged, x, mod, g_post, g_pre, w_out_b, w_router_t, b_router_col):
    t, d = x.shape
    nb = mod.shape[0]
    n_exp = w_router_t.shape[0]
    tm = 256
    row = lambda i: (i, 0)
    fixed = lambda i: (0, 0)
    in_specs = [pl.BlockSpec((tm, d), row), pl.BlockSpec((tm, d), row),
                pl.BlockSpec((nb, d), lambda i: (0, 2)),
                pl.BlockSpec((nb, d), lambda i: (0, 3)),
                pl.BlockSpec((nb, d), lambda i: (0, 4)),
                pl.BlockSpec((1, d), fixed), pl.BlockSpec((1, d), fixed),
                pl.BlockSpec((d, d), fixed),
                pl.BlockSpec((n_exp, d), fixed), pl.BlockSpec((n_exp, 1), fixed)]
    out_specs = [pl.BlockSpec((tm, d), row), pl.BlockSpec((tm, d), row),
                 pl.BlockSpec((tm * SUBLANES, LANES), row),
                 pl.BlockSpec((TOP_K, tm), lambda i: (0, i)), pl.BlockSpec((TOP_K, tm), lambda i: (0, i))]
    out_shape = [jax.ShapeDtypeStruct((t, d), F32), jax.ShapeDtypeStruct((t, d), BF16),
                 jax.ShapeDtypeStruct((t * SUBLANES, LANES), I32),
                 jax.ShapeDtypeStruct((TOP_K, t), I32), jax.ShapeDtypeStruct((TOP_K, t), F32)]
    return pl.pallas_call(
        functools.partial(_out_kernel, tiles_per_row=t // nb // tm),
        grid=(t // tm,),
        in_specs=in_specs, out_specs=out_specs, out_shape=out_shape,
        compiler_params=_params(1, 48),
        name="out_proj_router",
    )(merged, x, mod, mod, mod, g_post, g_pre, w_out_b, w_router_t, b_router_col)


GATHER_UNROLL = 8


def _gather_kernel(tok_ref, src_hbm, o_ref, sem):
    n = tok_ref.shape[2]

    def issue(c, carry):
        for u in range(GATHER_UNROLL):
            r = c * GATHER_UNROLL + u
            src = pl.multiple_of(tok_ref[0, 0, r] * SUBLANES, SUBLANES)
            dst = pl.multiple_of(r * SUBLANES, SUBLANES)
            pltpu.make_async_copy(src_hbm.at[pl.ds(src, SUBLANES), :], o_ref.at[pl.ds(dst, SUBLANES), :],
                                  sem).start()
        return carry

    lax.fori_loop(0, n // GATHER_UNROLL, issue, 0)
    pltpu.make_async_copy(src_hbm.at[pl.ds(0, n * SUBLANES), :], o_ref, sem).wait()


def _gather_rows(slot_tok, h2p, gb):
    n_slots = slot_tok.shape[0]
    nblk = n_slots // gb
    return pl.pallas_call(
        _gather_kernel,
        grid=(nblk,),
        in_specs=[pl.BlockSpec((1, 1, gb), lambda b: (b, 0, 0), memory_space=pltpu.SMEM),
                  pl.BlockSpec(memory_space=pl.ANY)],
        out_specs=pl.BlockSpec((gb * SUBLANES, LANES), lambda b: (b, 0)),
        out_shape=jax.ShapeDtypeStruct((n_slots * SUBLANES, LANES), I32),
        scratch_shapes=[pltpu.SemaphoreType.DMA],
        compiler_params=_params(1, 16),
        name="dispatch_gather",
    )(slot_tok.reshape(nblk, 1, gb), h2p)


def _expert_kernel(be_ref, nu_ref, x_ref, wg_ref, wu_ref, wd_ref, o_ref, wgb_ref, wub_ref, wdb_ref):
    b = pl.program_id(0)
    bm = x_ref.shape[0] // SUBLANES
    changed = jnp.logical_or(b == 0, be_ref[b] != be_ref[jnp.maximum(b - 1, 0)])

    @pl.when(changed)
    def _():
        wgb_ref[...] = wg_ref[...].astype(BF16)
        wub_ref[...] = wu_ref[...].astype(BF16)
        wdb_ref[...] = wd_ref[...].astype(BF16)

    @pl.when(b < nu_ref[0])
    def _():
        los, his = [], []
        for s in range(SUBLANES):
            lo, hi = _unpack_word(x_ref[pl.ds(s, bm, stride=SUBLANES), :])
            los.append(lo.astype(BF16))
            his.append(hi.astype(BF16))
        x = jnp.concatenate(los + his, axis=1)
        g = jnp.dot(x, wgb_ref[...], preferred_element_type=F32)
        u = jnp.dot(x, wub_ref[...], preferred_element_type=F32)
        a = (_silu(g) * u).astype(BF16)
        y = jnp.dot(a, wdb_ref[...], preferred_element_type=F32)
        _pack_rows(y, o_ref, bm)

    @pl.when(b >= nu_ref[0])
    def _():
        o_ref[...] = jnp.zeros_like(o_ref)


def _experts(block_expert, n_used, xs, w_gate, w_up, w_down, bm):
    n_exp, d, de = w_gate.shape
    nblk = block_expert.shape[0]
    grid_spec = pltpu.PrefetchScalarGridSpec(
        num_scalar_prefetch=2,
        grid=(nblk,),
        in_specs=[pl.BlockSpec((bm * SUBLANES, LANES), lambda b, be, nu: (b, 0)),
                  pl.BlockSpec((None, d, de), lambda b, be, nu: (be[b], 0, 0)),
                  pl.BlockSpec((None, d, de), lambda b, be, nu: (be[b], 0, 0)),
                  pl.BlockSpec((None, de, d), lambda b, be, nu: (be[b], 0, 0))],
        out_specs=pl.BlockSpec((bm * SUBLANES, LANES), lambda b, be, nu: (b, 0)),
        scratch_shapes=[pltpu.VMEM((d, de), BF16), pltpu.VMEM((d, de), BF16), pltpu.VMEM((de, d), BF16)],
    )
    return pl.pallas_call(
        _expert_kernel,
        grid_spec=grid_spec,
        out_shape=jax.ShapeDtypeStruct(xs.shape, I32),
        compiler_params=_params(1, 48),
        name="routed_experts",
    )(block_expert, n_used, xs, w_gate, w_up, w_down)


def _final_kernel(dest_ref, w_ref, yb_hbm, x1_ref, h2_ref, gate_ref, gpost_ref, wsg_ref, wsu_ref, wsd_ref,
                  o_ref, buf_ref, sem, *, tiles_per_row):
    tm = x1_ref.shape[0]
    r = pl.program_id(0) // tiles_per_row

    def issue(t, carry):
        for k in range(TOP_K):
            src = pl.multiple_of(dest_ref[0, k, t] * SUBLANES, SUBLANES)
            dst = pl.multiple_of((k * tm + t) * SUBLANES, SUBLANES)
            pltpu.make_async_copy(yb_hbm.at[pl.ds(src, SUBLANES), :], buf_ref.at[pl.ds(dst, SUBLANES), :],
                                  sem).start()
        return carry

    lax.fori_loop(0, tm, issue, 0)

    h = h2_ref[...]
    g = jnp.dot(h, wsg_ref[...], preferred_element_type=F32)
    u = jnp.dot(h, wsu_ref[...], preferred_element_type=F32)
    shared = jnp.dot((_silu(g) * u).astype(BF16), wsd_ref[...], preferred_element_type=F32)

    pltpu.make_async_copy(yb_hbm.at[pl.ds(0, TOP_K * tm * SUBLANES), :], buf_ref, sem).wait()

    w = w_ref[...]
    los, his = [], []
    for s in range(SUBLANES):
        lo_acc = jnp.zeros((tm, LANES), F32)
        hi_acc = jnp.zeros((tm, LANES), F32)
        for k in range(TOP_K):
            lo, hi = _unpack_word(buf_ref[pl.ds(k * tm * SUBLANES + s, tm, stride=SUBLANES), :])
            wk = w[:, k:k + 1]
            lo_acc = lo_acc + wk * lo
            hi_acc = hi_acc + wk * hi
        los.append(lo_acc)
        his.append(hi_acc)
    ffn = jnp.concatenate(los + his, axis=1) + shared
    o_ref[...] = x1_ref[...] + gate_ref[pl.ds(r, 1), :] * _rms(ffn, gpost_ref[...])


def _final(dest, w_tok, yb, x1, h2b, mod, g_post, ws_gate, ws_up, ws_down):
    t, d = x1.shape
    nb = mod.shape[0]
    ds_ = ws_gate.shape[1]
    tm = dest.shape[2]
    row = lambda i: (i, 0)
    fixed = lambda i: (0, 0)
    return pl.pallas_call(
        functools.partial(_final_kernel, tiles_per_row=t // nb // tm),
        grid=(t // tm,),
        in_specs=[pl.BlockSpec((1, TOP_K, tm), lambda i: (i, 0, 0), memory_space=pltpu.SMEM),
                  pl.BlockSpec((tm, TOP_K), row),
                  pl.BlockSpec(memory_space=pl.ANY),
                  pl.BlockSpec((tm, d), row), pl.BlockSpec((tm, d), row),
                  pl.BlockSpec((nb, d), lambda i: (0, 5)),
                  pl.BlockSpec((1, d), fixed),
                  pl.BlockSpec((d, ds_), fixed), pl.BlockSpec((d, ds_), fixed), pl.BlockSpec((ds_, d), fixed)],
        out_specs=pl.BlockSpec((tm, d), row),
        out_shape=jax.ShapeDtypeStruct((t, d), F32),
        scratch_shapes=[pltpu.VMEM((TOP_K * tm * SUBLANES, LANES), I32), pltpu.SemaphoreType.DMA],
        compiler_params=_params(1, 40),
        name="combine_final",
    )(dest, w_tok, yb, x1, h2b, mod, g_post, ws_gate, ws_up, ws_down)


def _rope_tables(n_pos):
    half = LANES // 2
    pos = jnp.arange(n_pos)
    row = (pos // GRID_W).astype(F32)
    col = (pos % GRID_W).astype(F32)
    inv = ROPE_THETA ** (-jnp.arange(0, half, 2, dtype=F32) / half)
    ang = jnp.concatenate([row[:, None] * inv, col[:, None] * inv], axis=-1)
    cos = jnp.repeat(jnp.cos(ang), 2, axis=-1)
    sin = jnp.repeat(jnp.sin(ang), 2, axis=-1) * jnp.tile(jnp.array([-1.0, 1.0], F32), half)
    return cos, sin


def _sublayer1(x, mod, p, seq_len, rope, cache, n_heads, n_kv, tq):
    t = x.shape[0]
    h = _prenorm(x, mod, p['g_pre1'], 0, 1)
    z = _matmul_wcast(h, p['w_in'], _z_col_map, 9 * 1024, BF16)
    kv = _matmul_wcast(h, p['w_in'], lambda j: j + 5, 1024, F32)
    qn, kn, vb, k32, v32 = _qk_prep(z, kv, p['g_q'], p['g_k'], rope, seq_len, n_heads, n_kv)
    attn = _attention(qn, kn, vb, cache, t // seq_len, seq_len, n_heads, n_kv, tq)
    cvg = _gated_conv(z, p['conv_w'], p['conv_b'], seq_len, 6)
    merged = _merge(cvg, attn, z, p['w_conv_out'], p['w_attn_out'], 4, 8)
    x1, h2b, h2p, idx, wsel = _out_proj(merged, x, mod, p['g_post1'], p['g_pre2'], p['w_out_b'],
                                        p['w_router_t'], p['b_router_col'])
    return x1, h2b, h2p, idx, wsel, k32, v32


def kernel(x_prompt, x_sample, cache_k, cache_v, c, c_ctx, w_mod, b_mod, g_pre1, w_in, conv_w, conv_b, g_q, g_k, w_conv_out, w_attn_out, w_out, g_post1, g_pre2, w_router, b_router, w_e_gate, w_e_up, w_e_down, w_s_gate, w_s_up, w_s_down, g_post2):
    batch, seq, d = x_prompt.shape
    dec_batch, dec_seq, _ = x_sample.shape
    depth = w_mod.shape[0]
    assert depth == 1
    past, n_kv, head_dim = cache_k.shape[2:]
    assert head_dim == LANES
    n_heads = w_attn_out.shape[1] // head_dim
    n_exp = w_router.shape[2]
    t_ctx, t_lat = batch * seq, dec_batch * dec_seq
    t_all = t_ctx + t_lat
    l = 0

    p = {
        'g_pre1': g_pre1[l][None], 'w_in': w_in[l], 'conv_w': conv_w[l], 'conv_b': conv_b[l][None],
        'g_q': g_q[l][None], 'g_k': g_k[l][None], 'w_conv_out': w_conv_out[l], 'w_attn_out': w_attn_out[l],
        'w_out_b': w_out[l].astype(BF16), 'g_post1': g_post1[l][None], 'g_pre2': g_pre2[l][None],
        'w_router_t': w_router[l].T, 'b_router_col': b_router[l][:, None],
    }
    cond = jnp.concatenate([c_ctx[None], c, jnp.zeros((SUBLANES - 1 - dec_batch, d), F32)], axis=0)
    mod = _modulation(cond, w_mod[l], b_mod[l][None])
    mod_ctx, mod_lat = mod[0:1], mod[1:1 + dec_batch]

    xc = x_prompt.reshape(t_ctx, d)
    xl = x_sample.reshape(t_lat, d)
    cache = (cache_k[:, l].reshape(dec_batch, past, n_kv * head_dim),
             cache_v[:, l].reshape(dec_batch, past, n_kv * head_dim))
    rope = _rope_tables(dec_seq)

    x1c, h2bc, h2pc, idxc, wc, k32, v32 = _sublayer1(xc, mod_ctx, p, seq, None, None, n_heads, n_kv, 256)
    x1l, h2bl, h2pl, idxl, wl, _, _ = _sublayer1(xl, mod_lat, p, dec_seq, rope, cache, n_heads, n_kv, 128)

    bm = 256
    idx_tok = jnp.concatenate([idxc, idxl], axis=1).T
    w_tok = jnp.concatenate([wc, wl], axis=1).T
    member = jnp.sum(jax.nn.one_hot(idx_tok, n_exp, dtype=I32), axis=1)
    cum = jnp.cumsum(member, axis=0)
    counts = cum[-1]
    padded = (counts + bm - 1) // bm * bm
    pend = jnp.cumsum(padded)
    pstart = pend - padded
    dest = jnp.take_along_axis(cum - member + pstart[None, :], idx_tok, axis=1)
    n_blocks = (t_all * TOP_K + n_exp * (bm - 1)) // bm + 1
    n_slots = n_blocks * bm
    tok_ids = jnp.broadcast_to(jnp.arange(t_all, dtype=I32)[:, None], (t_all, TOP_K))
    slot_tok = jnp.zeros((n_slots,), I32).at[dest.reshape(-1)].set(tok_ids.reshape(-1))
    block_expert = jnp.minimum(
        jnp.searchsorted(pend, jnp.arange(n_blocks, dtype=I32) * bm, side='right'), n_exp - 1).astype(I32)
    n_used = (pend[-1:] // bm).astype(I32)

    h2p = jnp.concatenate([h2pc, h2pl], axis=0)
    xs = _gather_rows(slot_tok, h2p, bm)
    yb = _experts(block_expert, n_used, xs, w_e_gate[l], w_e_up[l], w_e_down[l], bm)

    tmf = 128
    ws = (w_s_gate[l].astype(BF16), w_s_up[l].astype(BF16), w_s_down[l].astype(BF16))

    def finish(lo, n, x1, h2b, mod_g):
        dest_g = dest[lo:lo + n].reshape(n // tmf, tmf, TOP_K).transpose(0, 2, 1)
        return _final(dest_g, w_tok[lo:lo + n], yb, x1, h2b, mod_g, g_post2[l][None], *ws)

    y_ctx = finish(0, t_ctx, x1c, h2bc, mod_ctx)
    y_lat = finish(t_ctx, t_lat, x1l, h2bl, mod_lat)

    new_k = k32.reshape(batch, 1, seq, n_kv, head_dim)
    new_v = v32.reshape(batch, 1, seq, n_kv, head_dim)
    return (y_ctx.reshape(batch, seq, d), y_lat.reshape(dec_batch, dec_seq, d), new_k, new_v)
```

```python
import functools

import jax
import jax.numpy as jnp
from jax import lax
from jax.experimental import pallas as pl
from jax.experimental.pallas import tpu as pltpu

GRID_W = 64
ROPE_THETA = 10000.0
N_GROUPS = 8
TOPK_GROUPS = 4
TOP_K = 8
TOP_K_LOG2 = 3
ROUTED_SCALE = 2.5
EPS = 1e-6

LANES = 128
SUBLANES = 8
V7X_VMEM_BYTES = 64 * 1024 * 1024
MIB = 1024 * 1024

F32 = jnp.float32
BF16 = jnp.bfloat16
I32 = jnp.int32


def _params(n_grid, vmem_mib):
    assert vmem_mib * MIB < V7X_VMEM_BYTES
    return pltpu.CompilerParams(
        dimension_semantics=("arbitrary",) * n_grid, vmem_limit_bytes=vmem_mib * MIB)


def _silu(x):
    return x * jax.nn.sigmoid(x)


def _rms(x, g):
    return x * lax.rsqrt(jnp.mean(x * x, axis=-1, keepdims=True) + EPS) * g


def _pack_rows(val, out_ref, stage_ref, rows, row0=0, stage0=0):
    half = val.shape[1] // 2
    for s in range(half // LANES):
        base = stage0 + s * 2 * rows
        stage_ref[pl.ds(base, rows, stride=2), :] = val[:, s * LANES:(s + 1) * LANES]
        stage_ref[pl.ds(base + 1, rows, stride=2), :] = val[:, half + s * LANES:half + (s + 1) * LANES]
        pair = stage_ref[pl.ds(base, 2 * rows), :].astype(BF16)
        out_ref[pl.ds(row0 + s, rows, stride=SUBLANES), :] = pltpu.bitcast(pair, I32)


def _unpack_word(word, stage_ref, slot):
    rows = word.shape[0]
    base = slot * 2 * rows
    stage_ref[pl.ds(base, 2 * rows), :] = pltpu.bitcast(word, BF16).astype(F32)
    return stage_ref[pl.ds(base, rows, stride=2), :], stage_ref[pl.ds(base + 1, rows, stride=2), :]


def _mod_kernel(c_ref, w_ref, b_ref, o_ref):
    s = _silu(c_ref[...]).astype(BF16)
    o_ref[...] = jnp.dot(s, w_ref[...].astype(BF16), preferred_element_type=F32) + b_ref[...]


def _modulation(cond, w, b):
    rows, d = cond.shape
    n = w.shape[1]
    tn = 1024
    return pl.pallas_call(
        _mod_kernel,
        grid=(n // tn,),
        in_specs=[pl.BlockSpec((rows, d), lambda j: (0, 0)),
                  pl.BlockSpec((d, tn), lambda j: (0, j)),
                  pl.BlockSpec((1, tn), lambda j: (0, j))],
        out_specs=pl.BlockSpec((rows, tn), lambda j: (0, j)),
        out_shape=jax.ShapeDtypeStruct((rows, n), F32),
        compiler_params=_params(1, 40),
        name="modulation",
    )(cond, w, b)


def _prenorm_kernel(x_ref, shift_ref, scale_ref, g_ref, o_ref, *, tiles_per_row):
    r = pl.program_id(0) // tiles_per_row
    y = _rms(x_ref[...], g_ref[...])
    o_ref[...] = (y * (1.0 + scale_ref[pl.ds(r, 1), :]) + shift_ref[pl.ds(r, 1), :]).astype(o_ref.dtype)


def _prenorm(x, mod, g, shift_col, scale_col):
    t, d = x.shape
    nb = mod.shape[0]
    tm = 512
    return pl.pallas_call(
        functools.partial(_prenorm_kernel, tiles_per_row=t // nb // tm),
        grid=(t // tm,),
        in_specs=[pl.BlockSpec((tm, d), lambda i: (i, 0)),
                  pl.BlockSpec((nb, d), lambda i: (0, shift_col)),
                  pl.BlockSpec((nb, d), lambda i: (0, scale_col)),
                  pl.BlockSpec((1, d), lambda i: (0, 0))],
        out_specs=pl.BlockSpec((tm, d), lambda i: (i, 0)),
        out_shape=jax.ShapeDtypeStruct((t, d), BF16),
        compiler_params=_params(1, 32),
        name="prenorm",
    )(x, mod, mod, g)


def _mm_wcast_kernel(a_ref, w_ref, o_ref, wb_ref):
    @pl.when(pl.program_id(1) == 0)
    def _():
        wb_ref[...] = w_ref[...].astype(BF16)

    o_ref[...] = jnp.dot(a_ref[...], wb_ref[...], preferred_element_type=F32).astype(o_ref.dtype)


def _matmul_wcast(a, w, col_map, n_out, out_dtype):
    m, k = a.shape
    tm, tn = 1024, 1024
    return pl.pallas_call(
        _mm_wcast_kernel,
        grid=(n_out // tn, m // tm),
        in_specs=[pl.BlockSpec((tm, k), lambda j, i: (i, 0)),
                  pl.BlockSpec((k, tn), lambda j, i: (0, col_map(j)))],
        out_specs=pl.BlockSpec((tm, tn), lambda j, i: (i, j)),
        out_shape=jax.ShapeDtypeStruct((m, n_out), out_dtype),
        scratch_shapes=[pltpu.VMEM((k, tn), BF16)],
        compiler_params=_params(2, 48),
        name="in_proj",
    )(a, w)


def _z_col_map(j):
    return jnp.where(j < 2, j + 3, jnp.where(j < 6, j + 4, j - 6))


def _qk_kernel(*refs, use_rope, n_heads, n_kv, q_scale):
    if use_rope:
        q_ref, kv_ref, gq_ref, gk_ref, cos_ref, sin_ref, qn_ref, kn_ref, vb_ref, k32_ref, v32_ref = refs
        cos = cos_ref[...]
        sin = sin_ref[...]
        even = lax.broadcasted_iota(I32, cos.shape, 1) % 2 == 0
    else:
        q_ref, kv_ref, gq_ref, gk_ref, qn_ref, kn_ref, vb_ref, k32_ref, v32_ref = refs

    def norm_rope(xh, g):
        y = _rms(xh, g)
        if use_rope:
            sw = jnp.where(even, pltpu.roll(y, LANES - 1, 1), pltpu.roll(y, 1, 1))
            y = y * cos + sw * sin
        return y

    gq = gq_ref[...]
    gk = gk_ref[...]
    for h in range(n_heads):
        sl = slice(h * LANES, (h + 1) * LANES)
        qn_ref[:, sl] = (norm_rope(q_ref[:, sl].astype(F32), gq) * q_scale).astype(BF16)
    kw = n_kv * LANES
    for h in range(n_kv):
        sl = slice(h * LANES, (h + 1) * LANES)
        kh = norm_rope(kv_ref[:, sl], gk)
        k32_ref[:, sl] = kh
        kn_ref[:, sl] = kh.astype(BF16)
    v = kv_ref[:, kw:2 * kw]
    v32_ref[...] = v
    vb_ref[...] = v.astype(BF16)


def _qk_prep(z, kv, gq, gk, rope, seq_len, n_heads, n_kv):
    t = z.shape[0]
    dq = n_heads * LANES
    dk = n_kv * LANES
    tm = 256
    in_specs = [pl.BlockSpec((tm, dq), lambda i: (i, 0)),
                pl.BlockSpec((tm, 2 * dk), lambda i: (i, 0)),
                pl.BlockSpec((1, LANES), lambda i: (0, 0)),
                pl.BlockSpec((1, LANES), lambda i: (0, 0))]
    args = [z, kv, gq, gk]
    if rope is not None:
        per_seq = seq_len // tm
        in_specs += [pl.BlockSpec((tm, LANES), lambda i: (i % per_seq, 0))] * 2
        args += list(rope)
    out_specs = [pl.BlockSpec((tm, dq), lambda i: (i, 0))] + [pl.BlockSpec((tm, dk), lambda i: (i, 0))] * 4
    out_shape = [jax.ShapeDtypeStruct((t, dq), BF16), jax.ShapeDtypeStruct((t, dk), BF16),
                 jax.ShapeDtypeStruct((t, dk), BF16), jax.ShapeDtypeStruct((t, dk), F32),
                 jax.ShapeDtypeStruct((t, dk), F32)]
    return pl.pallas_call(
        functools.partial(_qk_kernel, use_rope=rope is not None, n_heads=n_heads, n_kv=n_kv,
                          q_scale=LANES ** -0.5),
        grid=(t // tm,),
        in_specs=in_specs, out_specs=out_specs, out_shape=out_shape,
        compiler_params=_params(1, 32),
        name="qk_prep",
    )(*args)


def _attn_kernel(*refs, has_cache, group):
    if has_cache:
        q_ref, k_ref, v_ref, ck_ref, cv_ref, o_ref = refs
    else:
        q_ref, k_ref, v_ref, o_ref = refs
    tq = q_ref.shape[0]
    nt = (((1,), (1,)), ((), ()))
    q = jnp.concatenate([q_ref[:, g * LANES:(g + 1) * LANES] for g in range(group)], axis=0)
    s_own = lax.dot_general(q, k_ref[...], nt, preferred_element_type=F32)
    m = jnp.max(s_own, axis=-1, keepdims=True)
    if has_cache:
        s_ctx = lax.dot_general(q, ck_ref[...].astype(BF16), nt, preferred_element_type=F32)
        m = jnp.maximum(m, jnp.max(s_ctx, axis=-1, keepdims=True))
    p = jnp.exp(s_own - m)
    denom = jnp.sum(p, axis=-1, keepdims=True)
    acc = jnp.dot(p.astype(BF16), v_ref[...], preferred_element_type=F32)
    if has_cache:
        pc = jnp.exp(s_ctx - m)
        denom = denom + jnp.sum(pc, axis=-1, keepdims=True)
        acc = acc + jnp.dot(pc.astype(BF16), cv_ref[...].astype(BF16), preferred_element_type=F32)
    o = acc / denom
    for g in range(group):
        o_ref[:, g * LANES:(g + 1) * LANES] = o[g * tq:(g + 1) * tq].astype(o_ref.dtype)


def _attention(qn, kn, vb, cache, batch, seq_len, n_heads, n_kv, tq):
    t = qn.shape[0]
    group = n_heads // n_kv
    nq = seq_len // tq
    in_specs = [pl.BlockSpec((tq, group * LANES), lambda b, h, i: (b * nq + i, h)),
                pl.BlockSpec((seq_len, LANES), lambda b, h, i: (b, h)),
                pl.BlockSpec((seq_len, LANES), lambda b, h, i: (b, h))]
    args = [qn, kn, vb]
    if cache is not None:
        past = cache[0].shape[1]
        in_specs += [pl.BlockSpec((None, past, LANES), lambda b, h, i: (b, 0, h))] * 2
        args += list(cache)
    return pl.pallas_call(
        functools.partial(_attn_kernel, has_cache=cache is not None, group=group),
        grid=(batch, n_kv, nq),
        in_specs=in_specs,
        out_specs=pl.BlockSpec((tq, group * LANES), lambda b, h, i: (b * nq + i, h)),
        out_shape=jax.ShapeDtypeStruct((t, n_heads * LANES), BF16),
        compiler_params=_params(3, 40),
        name="attention",
    )(*args)


CONV_HALO = 16


def _conv_kernel(u_ref, b_ref, c_ref, up_ref, cp_ref, un_ref, cn_ref, w_ref, bias_ref, o_ref, *,
                 tiles_per_seq):
    tm = u_ref.shape[0]
    pos = pl.program_id(0) % tiles_per_seq
    cu = c_ref[...].astype(F32) * u_ref[...].astype(F32)
    halo_prev = (cp_ref[...].astype(F32) * up_ref[...].astype(F32))[CONV_HALO - 1:CONV_HALO, :]
    halo_next = (cn_ref[...].astype(F32) * un_ref[...].astype(F32))[0:1, :]
    halo_prev = jnp.where(pos == 0, 0.0, halo_prev)
    halo_next = jnp.where(pos == tiles_per_seq - 1, 0.0, halo_next)
    row = lax.broadcasted_iota(I32, cu.shape, 0)
    prev = jnp.where(row == 0, halo_prev, pltpu.roll(cu, 1, 0))
    nxt = jnp.where(row == tm - 1, halo_next, pltpu.roll(cu, tm - 1, 0))
    w = w_ref[...]
    conv = prev * w[0:1, :] + cu * w[1:2, :] + nxt * w[2:3, :] + bias_ref[...]
    o_ref[...] = (b_ref[...].astype(F32) * conv).astype(o_ref.dtype)


def _gated_conv(z, conv_w, conv_b, seq_len, col0):
    t = z.shape[0]
    dc = conv_w.shape[1]
    tm = 256
    hb = tm // CONV_HALO
    last = t // CONV_HALO - 1
    prev_map = lambda c: (lambda i: (jnp.maximum(i * hb - 1, 0), c))
    next_map = lambda c: (lambda i: (jnp.minimum((i + 1) * hb, last), c))
    in_specs = [pl.BlockSpec((tm, dc), lambda i: (i, col0)),
                pl.BlockSpec((tm, dc), lambda i: (i, col0 + 1)),
                pl.BlockSpec((tm, dc), lambda i: (i, col0 + 2)),
                pl.BlockSpec((CONV_HALO, dc), prev_map(col0)),
                pl.BlockSpec((CONV_HALO, dc), prev_map(col0 + 2)),
                pl.BlockSpec((CONV_HALO, dc), next_map(col0)),
                pl.BlockSpec((CONV_HALO, dc), next_map(col0 + 2)),
                pl.BlockSpec(conv_w.shape, lambda i: (0, 0)),
                pl.BlockSpec((1, dc), lambda i: (0, 0))]
    return pl.pallas_call(
        functools.partial(_conv_kernel, tiles_per_seq=seq_len // tm),
        grid=(t // tm,),
        in_specs=in_specs,
        out_specs=pl.BlockSpec((tm, dc), lambda i: (i, 0)),
        out_shape=jax.ShapeDtypeStruct((t, dc), BF16),
        compiler_params=_params(1, 32),
        name="gated_conv",
    )(z, z, z, z, z, z, z, conv_w, conv_b)


def _merge_kernel(cv_ref, at_ref, gc_ref, ga_ref, wc_ref, wa_ref, o_ref, wcb_ref, wab_ref):
    @pl.when(pl.program_id(1) == 0)
    def _():
        wcb_ref[...] = wc_ref[...].astype(BF16)
        wab_ref[...] = wa_ref[...].astype(BF16)

    conv_out = jnp.dot(cv_ref[...], wcb_ref[...], preferred_element_type=F32)
    attn_out = jnp.dot(at_ref[...], wab_ref[...], preferred_element_type=F32)
    merged = (jax.nn.sigmoid(gc_ref[...].astype(F32)) * conv_out
              + jax.nn.sigmoid(ga_ref[...].astype(F32)) * attn_out)
    o_ref[...] = merged.astype(o_ref.dtype)


def _merge(cvg, attn, z, w_conv_out, w_attn_out, gc_col0, ga_col0):
    t, dc = cvg.shape
    dq = attn.shape[1]
    d = w_conv_out.shape[1]
    tm, tn = 512, 512
    return pl.pallas_call(
        _merge_kernel,
        grid=(d // tn, t // tm),
        in_specs=[pl.BlockSpec((tm, dc), lambda j, i: (i, 0)),
                  pl.BlockSpec((tm, dq), lambda j, i: (i, 0)),
                  pl.BlockSpec((tm, tn), lambda j, i: (i, gc_col0 + j)),
                  pl.BlockSpec((tm, tn), lambda j, i: (i, ga_col0 + j)),
                  pl.BlockSpec((dc, tn), lambda j, i: (0, j)),
                  pl.BlockSpec((dq, tn), lambda j, i: (0, j))],
        out_specs=pl.BlockSpec((tm, tn), lambda j, i: (i, j)),
        out_shape=jax.ShapeDtypeStruct((t, d), BF16),
        scratch_shapes=[pltpu.VMEM((dc, tn), BF16), pltpu.VMEM((dq, tn), BF16)],
        compiler_params=_params(2, 40),
        name="merge",
    )(cvg, attn, z, z, w_conv_out, w_attn_out)


def _route(logits_t, bias_col, carry):
    n_exp, tm = logits_t.shape
    per = n_exp // N_GROUPS
    assert per == SUBLANES
    neg = -jnp.inf
    scores = jax.nn.sigmoid(logits_t)
    biased = scores + bias_col
    sub = lax.broadcasted_iota(I32, (per, tm), 0).astype(F32)
    xs = [biased[g * per:(g + 1) * per, :] for g in range(N_GROUPS)]
    sc = [scores[g * per:(g + 1) * per, :] for g in range(N_GROUPS)]
    ids = [sub + float(g * per) for g in range(N_GROUPS)]

    def colmax(a):
        return jnp.max(a, axis=0, keepdims=True)

    def colmin(a):
        return jnp.min(a, axis=0, keepdims=True)

    rows = []
    for g in range(N_GROUPS):
        m1 = colmax(xs[g])
        j1 = colmin(jnp.where(xs[g] == m1, sub, float(per)))
        m2 = colmax(jnp.where(sub == j1, neg, xs[g]))
        rows.append(m1 + m2)
    gs = jnp.concatenate(rows, axis=0)
    gsel = jnp.zeros_like(gs)
    for _ in range(TOPK_GROUPS):
        m = colmax(gs)
        j = colmin(jnp.where(gs == m, sub, float(N_GROUPS)))
        hit = sub == j
        gsel = jnp.where(hit, 1.0, gsel)
        gs = jnp.where(hit, neg, gs)
    masked = [jnp.where(gsel[g:g + 1, :] > 0.0, xs[g], neg) for g in range(N_GROUPS)]
    idx_rows, w_rows = [], []
    member = [jnp.zeros((per, tm), F32) for _ in range(N_GROUPS)]
    for _ in range(TOP_K):
        mm = masked[0]
        for g in range(1, N_GROUPS):
            mm = jnp.maximum(mm, masked[g])
        m = colmax(mm)
        idx = colmin(jnp.where(masked[0] == m, ids[0], float(n_exp)))
        for g in range(1, N_GROUPS):
            idx = jnp.minimum(idx, colmin(jnp.where(masked[g] == m, ids[g], float(n_exp))))
        wk = jnp.zeros_like(idx)
        for g in range(N_GROUPS):
            hit = ids[g] == idx
            wk = wk + jnp.sum(jnp.where(hit, sc[g], 0.0), axis=0, keepdims=True)
            masked[g] = jnp.where(hit, neg, masked[g])
            member[g] = jnp.where(hit, 1.0, member[g])
        idx_rows.append(idx)
        w_rows.append(wk)
    w = jnp.concatenate(w_rows, axis=0)
    w = w / jnp.sum(w, axis=0, keepdims=True) * ROUTED_SCALE

    earlier = (lax.broadcasted_iota(I32, (tm, tm), 0) < lax.broadcasted_iota(I32, (tm, tm), 1)).astype(BF16)
    before = jnp.dot(jnp.concatenate(member, axis=0).astype(BF16), earlier, preferred_element_type=F32)
    rank_rows = []
    for k in range(TOP_K):
        rk = jnp.zeros_like(idx_rows[k])
        for g in range(N_GROUPS):
            pos = before[g * per:(g + 1) * per, :] + carry[g]
            rk = rk + jnp.sum(jnp.where(ids[g] == idx_rows[k], pos, 0.0), axis=0, keepdims=True)
        rank_rows.append(rk)
    new_carry = [carry[g] + jnp.sum(member[g], axis=1, keepdims=True) for g in range(N_GROUPS)]
    idx = jnp.concatenate(idx_rows, axis=0).astype(I32)
    rank = jnp.concatenate(rank_rows, axis=0).astype(I32)
    return idx, w, rank, new_carry


def _out_kernel(mg_ref, x_ref, gate_ref, shift_ref, scale_ref, gpost_ref, gpre_ref, wo_ref, wr_ref,
                x1_ref, h2b_ref, h2p_ref, logit_ref, stage_ref, *, tiles_per_row):
    tm = x_ref.shape[0]
    r = pl.program_id(0) // tiles_per_row
    gate = gate_ref[pl.ds(r, 1), :]
    scale = 1.0 + scale_ref[pl.ds(r, 1), :]
    shift = shift_ref[pl.ds(r, 1), :]
    rows = tm // OUT_CHUNKS
    for c in range(OUT_CHUNKS):
        sl = pl.ds(c * rows, rows)
        mix = jnp.dot(mg_ref[sl, :], wo_ref[...], preferred_element_type=F32)
        x1 = x_ref[sl, :] + gate * _rms(mix, gpost_ref[...])
        x1_ref[sl, :] = x1
        h2 = _rms(x1, gpre_ref[...]) * scale + shift
        h2b_ref[sl, :] = h2.astype(BF16)
        _pack_rows(h2, h2p_ref, stage_ref, rows, row0=c * rows * SUBLANES, stage0=c * rows * 2 * SUBLANES)
        logit_ref[:, sl] = lax.dot_general(wr_ref[...], h2, (((1,), (1,)), ((), ())),
                                           preferred_element_type=F32, precision=lax.Precision.HIGHEST)


OUT_CHUNKS = 2


def _out_proj(merged, x, mod, g_post, g_pre, w_out_b, w_router_t):
    t, d = x.shape
    nb = mod.shape[0]
    n_exp = w_router_t.shape[0]
    tm = 512
    row = lambda i: (i, 0)
    fixed = lambda i: (0, 0)
    once = pl.Buffered(1)
    in_specs = [pl.BlockSpec((tm, d), row), pl.BlockSpec((tm, d), row),
                pl.BlockSpec((nb, d), lambda i: (0, 2)),
                pl.BlockSpec((nb, d), lambda i: (0, 3)),
                pl.BlockSpec((nb, d), lambda i: (0, 4)),
                pl.BlockSpec((1, d), fixed), pl.BlockSpec((1, d), fixed),
                pl.BlockSpec((d, d), fixed, pipeline_mode=once),
                pl.BlockSpec((n_exp, d), fixed, pipeline_mode=once)]
    out_specs = [pl.BlockSpec((tm, d), row), pl.BlockSpec((tm, d), row),
                 pl.BlockSpec((tm * SUBLANES, LANES), row),
                 pl.BlockSpec((n_exp, tm), lambda i: (0, i))]
    out_shape = [jax.ShapeDtypeStruct((t, d), F32), jax.ShapeDtypeStruct((t, d), BF16),
                 jax.ShapeDtypeStruct((t * SUBLANES, LANES), I32),
                 jax.ShapeDtypeStruct((n_exp, t), F32)]
    return pl.pallas_call(
        functools.partial(_out_kernel, tiles_per_row=t // nb // tm),
        grid=(t // tm,),
        in_specs=in_specs, out_specs=out_specs, out_shape=out_shape,
        scratch_shapes=[pltpu.VMEM((SUBLANES * 2 * tm, LANES), F32)],
        compiler_params=_params(1, 56),
        name="out_proj",
    )(merged, x, mod, mod, mod, g_post, g_pre, w_out_b, w_router_t)


def _router_kernel(logit_ref, br_ref, idx_ref, wsel_ref, rank_ref, cnt_ref):
    per = SUBLANES

    @pl.when(pl.program_id(0) == 0)
    def _():
        cnt_ref[...] = jnp.zeros_like(cnt_ref)

    carry = [cnt_ref[g * per:(g + 1) * per, 0:1] for g in range(N_GROUPS)]
    idx, w, rank, carry = _route(logit_ref[...], br_ref[...], carry)
    idx_ref[...] = idx
    wsel_ref[...] = w
    rank_ref[...] = rank
    for g in range(N_GROUPS):
        cnt_ref[g * per:(g + 1) * per, :] = jnp.broadcast_to(carry[g], (per, LANES))


def _router(logits_t, b_router_col):
    n_exp, t = logits_t.shape
    tr = 1024
    tile = lambda i: (0, i)
    fixed = lambda i: (0, 0)
    return pl.pallas_call(
        _router_kernel,
        grid=(t // tr,),
        in_specs=[pl.BlockSpec((n_exp, tr), tile), pl.BlockSpec((n_exp, 1), fixed)],
        out_specs=[pl.BlockSpec((TOP_K, tr), tile), pl.BlockSpec((TOP_K, tr), tile),
                   pl.BlockSpec((TOP_K, tr), tile), pl.BlockSpec((n_exp, LANES), fixed)],
        out_shape=[jax.ShapeDtypeStruct((TOP_K, t), I32), jax.ShapeDtypeStruct((TOP_K, t), F32),
                   jax.ShapeDtypeStruct((TOP_K, t), I32), jax.ShapeDtypeStruct((n_exp, LANES), F32)],
        compiler_params=_params(1, 32),
        name="router",
    )(logits_t, b_router_col)


def _dest_kernel(pstart_ref, idx_ref, rank_ref, o_ref):
    idx = idx_ref[...]
    acc = rank_ref[...]
    for e in range(pstart_ref.shape[0]):
        acc = acc + jnp.where(idx == e, pstart_ref[e], 0)
    o_ref[...] = acc


def _dest_slots(pstart, idx, rank):
    return pl.pallas_call(
        _dest_kernel,
        in_specs=[pl.BlockSpec(memory_space=pltpu.SMEM),
                  pl.BlockSpec(memory_space=pltpu.VMEM), pl.BlockSpec(memory_space=pltpu.VMEM)],
        out_specs=pl.BlockSpec(memory_space=pltpu.VMEM),
        out_shape=jax.ShapeDtypeStruct(idx.shape, I32),
        name="dest_slots",
    )(pstart, idx, rank)


SLOT_FILL_UNROLL = 8


def _slot_table_kernel(dest_ref, tab_ref):
    i = pl.program_id(0)
    tm = dest_ref.shape[2]

    @pl.when(i == 0)
    def _():
        def clear(c, carry):
            for u in range(SLOT_FILL_UNROLL):
                tab_ref[c * SLOT_FILL_UNROLL + u] = 0
            return carry
        lax.fori_loop(0, tab_ref.shape[0] // SLOT_FILL_UNROLL, clear, 0)

    def fill(t, carry):
        for k in range(TOP_K):
            tab_ref[dest_ref[0, k, t]] = (i * tm + t) * TOP_K + k
        return carry

    lax.fori_loop(0, tm, fill, 0)


def _slot_table(dest3, n_slots):
    nt, _, tm = dest3.shape
    return pl.pallas_call(
        _slot_table_kernel,
        grid=(nt,),
        in_specs=[pl.BlockSpec((1, TOP_K, tm), lambda i: (i, 0, 0), memory_space=pltpu.SMEM)],
        out_specs=pl.BlockSpec(memory_space=pltpu.SMEM),
        out_shape=jax.ShapeDtypeStruct((n_slots,), I32),
        compiler_params=_params(1, 16),
        name="slot_table",
    )(dest3)


GATHER_UNROLL = 8
GATHER_BLOCK = 1024


def _gather_kernel(tab_ref, src_ref, o_ref):
    n = tab_ref.shape[2]

    def move(c, carry):
        for u in range(GATHER_UNROLL):
            r = c * GATHER_UNROLL + u
            tok = lax.shift_right_logical(tab_ref[0, 0, r], TOP_K_LOG2)
            src = pl.multiple_of(tok * SUBLANES, SUBLANES)
            dst = pl.multiple_of(r * SUBLANES, SUBLANES)
            o_ref[pl.ds(dst, SUBLANES), :] = src_ref[pl.ds(src, SUBLANES), :]
        return carry

    lax.fori_loop(0, n // GATHER_UNROLL, move, 0)


def _gather_rows(slot_tab, h2p, gb):
    n_slots = slot_tab.shape[0]
    nblk = n_slots // gb
    resident = h2p.size * h2p.dtype.itemsize
    return pl.pallas_call(
        _gather_kernel,
        grid=(nblk,),
        in_specs=[pl.BlockSpec((1, 1, gb), lambda b: (b, 0, 0), memory_space=pltpu.SMEM),
                  pl.BlockSpec(memory_space=pltpu.VMEM)],
        out_specs=pl.BlockSpec((gb * SUBLANES, LANES), lambda b: (b, 0)),
        out_shape=jax.ShapeDtypeStruct((n_slots * SUBLANES, LANES), I32),
        compiler_params=_params(1, resident // MIB + 8),
        name="dispatch_gather",
    )(slot_tab.reshape(nblk, 1, gb), h2p)


ROW_DMA_PRIORITY = 1


def _expert_kernel(first_ref, nblk_ref, xs_hbm, wg_ref, wu_ref, wd_ref, yb_hbm,
                   xbuf_ref, obuf_ref, in_sem, out_sem, wgb_ref, wub_ref, wdb_ref, stage_in_ref, stage_out_ref,
                   *, bm):
    e = pl.program_id(0)
    nb = nblk_ref[e]
    b0 = first_ref[e]
    rows = bm * SUBLANES

    def x_copy(j, slot):
        return pltpu.make_async_copy(
            xs_hbm.at[pl.ds(pl.multiple_of((b0 + j) * rows, rows), rows), :],
            xbuf_ref.at[pl.ds(pl.multiple_of(slot * rows, rows), rows), :], in_sem.at[slot])

    def y_copy(j, slot):
        return pltpu.make_async_copy(
            obuf_ref.at[pl.ds(pl.multiple_of(slot * rows, rows), rows), :],
            yb_hbm.at[pl.ds(pl.multiple_of((b0 + j) * rows, rows), rows), :], out_sem.at[slot])

    @pl.when(nb > 0)
    def _():
        x_copy(0, 0).start(priority=ROW_DMA_PRIORITY)
        wgb_ref[...] = wg_ref[...].astype(BF16)
        wub_ref[...] = wu_ref[...].astype(BF16)
        wdb_ref[...] = wd_ref[...].astype(BF16)

    def block(j, carry):
        slot = j % 2
        x_copy(j, slot).wait()

        @pl.when(j + 1 < nb)
        def _():
            x_copy(j + 1, 1 - slot).start(priority=ROW_DMA_PRIORITY)

        @pl.when(j >= 2)
        def _():
            y_copy(j - 2, slot).wait()

        xbase = slot * rows
        los, his = [], []
        for s in range(SUBLANES):
            lo, hi = _unpack_word(xbuf_ref[pl.ds(xbase + s, bm, stride=SUBLANES), :], stage_in_ref, s)
            los.append(lo.astype(BF16))
            his.append(hi.astype(BF16))
        x = jnp.concatenate(los + his, axis=1)
        g = jnp.dot(x, wgb_ref[...], preferred_element_type=F32)
        u = jnp.dot(x, wub_ref[...], preferred_element_type=F32)
        a = (_silu(g) * u).astype(BF16)
        y = jnp.dot(a, wdb_ref[...], preferred_element_type=F32)
        _pack_rows(y, obuf_ref, stage_out_ref, bm, row0=xbase)
        y_copy(j, slot).start()
        return carry

    lax.fori_loop(0, nb, block, 0)

    @pl.when(nb >= 2)
    def _():
        y_copy(nb - 2, nb % 2).wait()

    @pl.when(nb >= 1)
    def _():
        y_copy(nb - 1, (nb - 1) % 2).wait()

    @pl.when(e == pl.num_programs(0) - 1)
    def _():
        obuf_ref[pl.ds(0, rows), :] = pltpu.bitcast(jnp.zeros((2 * rows, LANES), BF16), I32)

        def clear(j, carry):
            cp = pltpu.make_async_copy(obuf_ref.at[pl.ds(0, rows), :],
                                       yb_hbm.at[pl.ds(pl.multiple_of(j * rows, rows), rows), :], out_sem.at[0])
            cp.start()
            cp.wait()
            return carry

        lax.fori_loop(b0 + nb, yb_hbm.shape[0] // rows, clear, 0)


def _experts(first_block, n_blocks_e, xs, w_gate, w_up, w_down, bm):
    n_exp, d, de = w_gate.shape
    rows = bm * SUBLANES
    grid_spec = pltpu.PrefetchScalarGridSpec(
        num_scalar_prefetch=2,
        grid=(n_exp,),
        in_specs=[pl.BlockSpec(memory_space=pl.ANY),
                  pl.BlockSpec((None, d, de), lambda e, fb, nb: (e, 0, 0)),
                  pl.BlockSpec((None, d, de), lambda e, fb, nb: (e, 0, 0)),
                  pl.BlockSpec((None, de, d), lambda e, fb, nb: (e, 0, 0))],
        out_specs=pl.BlockSpec(memory_space=pl.ANY),
        scratch_shapes=[pltpu.VMEM((2 * rows, LANES), I32), pltpu.VMEM((2 * rows, LANES), I32),
                        pltpu.SemaphoreType.DMA((2,)), pltpu.SemaphoreType.DMA((2,)),
                        pltpu.VMEM((d, de), BF16), pltpu.VMEM((d, de), BF16), pltpu.VMEM((de, d), BF16),
                        pltpu.VMEM((SUBLANES * 2 * bm, LANES), F32), pltpu.VMEM((SUBLANES * 2 * bm, LANES), F32)],
    )
    return pl.pallas_call(
        functools.partial(_expert_kernel, bm=bm),
        grid_spec=grid_spec,
        out_shape=jax.ShapeDtypeStruct(xs.shape, I32),
        compiler_params=_params(1, 48),
        name="routed_experts",
    )(first_block, n_blocks_e, xs, w_gate, w_up, w_down)


def _final_kernel(dest_ref, w_ref, yb_hbm, x1_ref, h2_ref, gate_ref, gpost_ref, wsg_ref, wsu_ref, wsd_ref,
                  o_ref, buf_ref, sem, stage_ref, *, tiles_per_row):
    tm = x1_ref.shape[0]
    r = pl.program_id(0) // tiles_per_row

    def issue(t, carry):
        for k in range(TOP_K):
            src = pl.multiple_of(dest_ref[0, k, t] * SUBLANES, SUBLANES)
            dst = pl.multiple_of((k * tm + t) * SUBLANES, SUBLANES)
            pltpu.make_async_copy(yb_hbm.at[pl.ds(src, SUBLANES), :], buf_ref.at[pl.ds(dst, SUBLANES), :],
                                  sem).start()
        return carry

    lax.fori_loop(0, tm, issue, 0)

    h = h2_ref[...]
    g = jnp.dot(h, wsg_ref[...], preferred_element_type=F32)
    u = jnp.dot(h, wsu_ref[...], preferred_element_type=F32)
    shared = jnp.dot((_silu(g) * u).astype(BF16), wsd_ref[...], preferred_element_type=F32)

    pltpu.make_async_copy(yb_hbm.at[pl.ds(0, TOP_K * tm * SUBLANES), :], buf_ref, sem).wait()

    w = w_ref[...]
    los, his = [], []
    for s in range(SUBLANES):
        lo_acc = jnp.zeros((tm, LANES), F32)
        hi_acc = jnp.zeros((tm, LANES), F32)
        for k in range(TOP_K):
            lo, hi = _unpack_word(buf_ref[pl.ds(k * tm * SUBLANES + s, tm, stride=SUBLANES), :], stage_ref,
                                  s * TOP_K + k)
            wk = w[:, k:k + 1]
            lo_acc = lo_acc + wk * lo
            hi_acc = hi_acc + wk * hi
        los.append(lo_acc)
        his.append(hi_acc)
    ffn = jnp.concatenate(los + his, axis=1) + shared
    o_ref[...] = x1_ref[...] + gate_ref[pl.ds(r, 1), :] * _rms(ffn, gpost_ref[...])


def _final(dest, w_tok, yb, x1, h2b, mod, g_post, ws_gate, ws_up, ws_down):
    t, d = x1.shape
    nb = mod.shape[0]
    ds_ = ws_gate.shape[1]
    tm = dest.shape[2]
    row = lambda i: (i, 0)
    fixed = lambda i: (0, 0)
    return pl.pallas_call(
        functools.partial(_final_kernel, tiles_per_row=t // nb // tm),
        grid=(t // tm,),
        in_specs=[pl.BlockSpec((1, TOP_K, tm), lambda i: (i, 0, 0), memory_space=pltpu.SMEM),
                  pl.BlockSpec((tm, TOP_K), row),
                  pl.BlockSpec(memory_space=pl.ANY),
                  pl.BlockSpec((tm, d), row), pl.BlockSpec((tm, d), row),
                  pl.BlockSpec((nb, d), lambda i: (0, 5)),
                  pl.BlockSpec((1, d), fixed),
                  pl.BlockSpec((d, ds_), fixed), pl.BlockSpec((d, ds_), fixed), pl.BlockSpec((ds_, d), fixed)],
        out_specs=pl.BlockSpec((tm, d), row),
        out_shape=jax.ShapeDtypeStruct((t, d), F32),
        scratch_shapes=[pltpu.VMEM((TOP_K * tm * SUBLANES, LANES), I32), pltpu.SemaphoreType.DMA,
                        pltpu.VMEM((SUBLANES * TOP_K * 2 * tm, LANES), F32)],
        compiler_params=_params(1, 48),
        name="combine_final",
    )(dest, w_tok, yb, x1, h2b, mod, g_post, ws_gate, ws_up, ws_down)


def _rope_tables(n_pos):
    half = LANES // 2
    pos = jnp.arange(n_pos)
    row = (pos // GRID_W).astype(F32)
    col = (pos % GRID_W).astype(F32)
    inv = ROPE_THETA ** (-jnp.arange(0, half, 2, dtype=F32) / half)
    ang = jnp.concatenate([row[:, None] * inv, col[:, None] * inv], axis=-1)
    cos = jnp.repeat(jnp.cos(ang), 2, axis=-1)
    sin = jnp.repeat(jnp.sin(ang), 2, axis=-1) * jnp.tile(jnp.array([-1.0, 1.0], F32), half)
    return cos, sin


def _sublayer1(x, mod, p, seq_len, rope, cache, n_heads, n_kv, tq):
    t = x.shape[0]
    h = _prenorm(x, mod, p['g_pre1'], 0, 1)
    z = _matmul_wcast(h, p['w_in'], _z_col_map, 9 * 1024, BF16)
    kv = _matmul_wcast(h, p['w_in'], lambda j: j + 5, 1024, F32)
    qn, kn, vb, k32, v32 = _qk_prep(z, kv, p['g_q'], p['g_k'], rope, seq_len, n_heads, n_kv)
    attn = _attention(qn, kn, vb, cache, t // seq_len, seq_len, n_heads, n_kv, tq)
    cvg = _gated_conv(z, p['conv_w'], p['conv_b'], seq_len, 6)
    merged = _merge(cvg, attn, z, p['w_conv_out'], p['w_attn_out'], 4, 8)
    x1, h2b, h2p, logits_t = _out_proj(merged, x, mod, p['g_post1'], p['g_pre2'], p['w_out_b'], p['w_router_t'])
    return x1, h2b, h2p, logits_t, k32, v32


def kernel(x_prompt, x_sample, cache_k, cache_v, c, c_ctx, w_mod, b_mod, g_pre1, w_in, conv_w, conv_b, g_q, g_k, w_conv_out, w_attn_out, w_out, g_post1, g_pre2, w_router, b_router, w_e_gate, w_e_up, w_e_down, w_s_gate, w_s_up, w_s_down, g_post2):
    batch, seq, d = x_prompt.shape
    dec_batch, dec_seq, _ = x_sample.shape
    depth = w_mod.shape[0]
    assert depth == 1
    past, n_kv, head_dim = cache_k.shape[2:]
    assert head_dim == LANES
    n_heads = w_attn_out.shape[1] // head_dim
    n_exp = w_router.shape[2]
    t_ctx, t_lat = batch * seq, dec_batch * dec_seq
    t_all = t_ctx + t_lat
    l = 0

    p = {
        'g_pre1': g_pre1[l][None], 'w_in': w_in[l], 'conv_w': conv_w[l], 'conv_b': conv_b[l][None],
        'g_q': g_q[l][None], 'g_k': g_k[l][None], 'w_conv_out': w_conv_out[l], 'w_attn_out': w_attn_out[l],
        'w_out_b': w_out[l].astype(BF16), 'g_post1': g_post1[l][None], 'g_pre2': g_pre2[l][None],
        'w_router_t': w_router[l].T,
    }
    cond = jnp.concatenate([c_ctx[None], c, jnp.zeros((SUBLANES - 1 - dec_batch, d), F32)], axis=0)
    mod = _modulation(cond, w_mod[l], b_mod[l][None])
    mod_ctx, mod_lat = mod[0:1], mod[1:1 + dec_batch]

    xc = x_prompt.reshape(t_ctx, d)
    xl = x_sample.reshape(t_lat, d)
    cache = (cache_k[:, l].reshape(dec_batch, past, n_kv * head_dim),
             cache_v[:, l].reshape(dec_batch, past, n_kv * head_dim))
    rope = _rope_tables(dec_seq)

    x1c, h2bc, h2pc, logc, k32, v32 = _sublayer1(xc, mod_ctx, p, seq, None, None, n_heads, n_kv, 256)
    x1l, h2bl, h2pl, logl, _, _ = _sublayer1(xl, mod_lat, p, dec_seq, rope, cache, n_heads, n_kv, 128)
    idx_all, w_all, rank_all, cnt = _router(jnp.concatenate([logc, logl], axis=1), b_router[l][:, None])

    bm = 256
    n_blocks = (t_all * TOP_K + n_exp * (bm - 1)) // bm + 1
    n_slots = n_blocks * bm
    blocks_e = jnp.floor((cnt[:, 0] + (bm - 1)) / bm)
    first_e = jnp.cumsum(blocks_e) - blocks_e
    pstart = (first_e * bm).astype(I32)

    tmf = 128
    w_tok = w_all.T
    dest = _dest_slots(pstart, idx_all, rank_all)
    dest3 = dest.reshape(TOP_K, t_all // tmf, tmf).transpose(1, 0, 2)
    slot_tab = _slot_table(dest3, n_slots)

    h2p = jnp.concatenate([h2pc, h2pl], axis=0)
    xs = _gather_rows(slot_tab, h2p, GATHER_BLOCK)
    yb = _experts(first_e.astype(I32), blocks_e.astype(I32), xs, w_e_gate[l], w_e_up[l], w_e_down[l], bm)

    ws = (w_s_gate[l].astype(BF16), w_s_up[l].astype(BF16), w_s_down[l].astype(BF16))

    def finish(lo, n, x1, h2b, mod_g):
        dest_g = dest3[lo // tmf:(lo + n) // tmf]
        return _final(dest_g, w_tok[lo:lo + n], yb, x1, h2b, mod_g, g_post2[l][None], *ws)

    y_ctx = finish(0, t_ctx, x1c, h2bc, mod_ctx)
    y_lat = finish(t_ctx, t_lat, x1l, h2bl, mod_lat)

    new_k = k32.reshape(batch, 1, seq, n_kv, head_dim)
    new_v = v32.reshape(batch, 1, seq, n_kv, head_dim)
    return (y_ctx.reshape(batch, seq, d), y_lat.reshape(dec_batch, dec_seq, d), new_k, new_v)
```

```python
import functools

import jax
import jax.numpy as jnp
from jax import lax
from jax.experimental import pallas as pl
from jax.experimental.pallas import tpu as pltpu

GRID_W = 64
ROPE_THETA = 10000.0
N_GROUPS = 8
TOPK_GROUPS = 4
TOP_K = 8
TOP_K_LOG2 = 3
ROUTED_SCALE = 2.5
EPS = 1e-6

LANES = 128
SUBLANES = 8
V7X_VMEM_BYTES = 64 * 1024 * 1024
MIB = 1024 * 1024

F32 = jnp.float32
BF16 = jnp.bfloat16
I32 = jnp.int32


def _params(n_grid, vmem_mib):
    assert vmem_mib * MIB < V7X_VMEM_BYTES
    return pltpu.CompilerParams(
        dimension_semantics=("arbitrary",) * n_grid, vmem_limit_bytes=vmem_mib * MIB)


def _silu(x):
    return x * jax.nn.sigmoid(x)


def _rms(x, g):
    return x * lax.rsqrt(jnp.mean(x * x, axis=-1, keepdims=True) + EPS) * g


def _pack_rows(val, out_ref, stage_ref, rows, row0=0, stage0=0):
    half = val.shape[1] // 2
    for s in range(half // LANES):
        base = stage0 + s * 2 * rows
        stage_ref[pl.ds(base, rows, stride=2), :] = val[:, s * LANES:(s + 1) * LANES]
        stage_ref[pl.ds(base + 1, rows, stride=2), :] = val[:, half + s * LANES:half + (s + 1) * LANES]
        pair = stage_ref[pl.ds(base, 2 * rows), :].astype(BF16)
        out_ref[pl.ds(row0 + s, rows, stride=SUBLANES), :] = pltpu.bitcast(pair, I32)


def _unpack_word(word, stage_ref, slot):
    rows = word.shape[0]
    base = slot * 2 * rows
    stage_ref[pl.ds(base, 2 * rows), :] = pltpu.bitcast(word, BF16).astype(F32)
    return stage_ref[pl.ds(base, rows, stride=2), :], stage_ref[pl.ds(base + 1, rows, stride=2), :]


def _mod_kernel(c_ref, w_ref, b_ref, o_ref):
    s = _silu(c_ref[...]).astype(BF16)
    o_ref[...] = jnp.dot(s, w_ref[...].astype(BF16), preferred_element_type=F32) + b_ref[...]


def _modulation(cond, w, b):
    rows, d = cond.shape
    n = w.shape[1]
    tn = 1024
    return pl.pallas_call(
        _mod_kernel,
        grid=(n // tn,),
        in_specs=[pl.BlockSpec((rows, d), lambda j: (0, 0)),
                  pl.BlockSpec((d, tn), lambda j: (0, j)),
                  pl.BlockSpec((1, tn), lambda j: (0, j))],
        out_specs=pl.BlockSpec((rows, tn), lambda j: (0, j)),
        out_shape=jax.ShapeDtypeStruct((rows, n), F32),
        compiler_params=_params(1, 40),
        name="modulation",
    )(cond, w, b)


def _prenorm_kernel(x_ref, shift_ref, scale_ref, g_ref, o_ref, *, tiles_per_row):
    r = pl.program_id(0) // tiles_per_row
    y = _rms(x_ref[...], g_ref[...])
    o_ref[...] = (y * (1.0 + scale_ref[pl.ds(r, 1), :]) + shift_ref[pl.ds(r, 1), :]).astype(o_ref.dtype)


def _prenorm(x, mod, g, shift_col, scale_col):
    t, d = x.shape
    nb = mod.shape[0]
    tm = 512
    return pl.pallas_call(
        functools.partial(_prenorm_kernel, tiles_per_row=t // nb // tm),
        grid=(t // tm,),
        in_specs=[pl.BlockSpec((tm, d), lambda i: (i, 0)),
                  pl.BlockSpec((nb, d), lambda i: (0, shift_col)),
                  pl.BlockSpec((nb, d), lambda i: (0, scale_col)),
                  pl.BlockSpec((1, d), lambda i: (0, 0))],
        out_specs=pl.BlockSpec((tm, d), lambda i: (i, 0)),
        out_shape=jax.ShapeDtypeStruct((t, d), BF16),
        compiler_params=_params(1, 32),
        name="prenorm",
    )(x, mod, mod, g)


def _mm_wcast_kernel(a_ref, w_ref, o_ref, wb_ref):
    @pl.when(pl.program_id(1) == 0)
    def _():
        wb_ref[...] = w_ref[...].astype(BF16)

    o_ref[...] = jnp.dot(a_ref[...], wb_ref[...], preferred_element_type=F32).astype(o_ref.dtype)


def _matmul_wcast(a, w, col_map, n_out, out_dtype):
    m, k = a.shape
    tm, tn = 1024, 1024
    return pl.pallas_call(
        _mm_wcast_kernel,
        grid=(n_out // tn, m // tm),
        in_specs=[pl.BlockSpec((tm, k), lambda j, i: (i, 0)),
                  pl.BlockSpec((k, tn), lambda j, i: (0, col_map(j)))],
        out_specs=pl.BlockSpec((tm, tn), lambda j, i: (i, j)),
        out_shape=jax.ShapeDtypeStruct((m, n_out), out_dtype),
        scratch_shapes=[pltpu.VMEM((k, tn), BF16)],
        compiler_params=_params(2, 48),
        name="in_proj",
    )(a, w)


def _z_col_map(j):
    return jnp.where(j < 2, j + 3, jnp.where(j < 6, j + 4, j - 6))


def _qk_kernel(*refs, use_rope, n_heads, n_kv, q_scale):
    if use_rope:
        q_ref, kv_ref, gq_ref, gk_ref, cos_ref, sin_ref, qn_ref, kn_ref, vb_ref, k32_ref, v32_ref = refs
        cos = cos_ref[...]
        sin = sin_ref[...]
        even = lax.broadcasted_iota(I32, cos.shape, 1) % 2 == 0
    else:
        q_ref, kv_ref, gq_ref, gk_ref, qn_ref, kn_ref, vb_ref, k32_ref, v32_ref = refs

    def norm_rope(xh, g):
        y = _rms(xh, g)
        if use_rope:
            sw = jnp.where(even, pltpu.roll(y, LANES - 1, 1), pltpu.roll(y, 1, 1))
            y = y * cos + sw * sin
        return y

    gq = gq_ref[...]
    gk = gk_ref[...]
    for h in range(n_heads):
        sl = slice(h * LANES, (h + 1) * LANES)
        qn_ref[:, sl] = (norm_rope(q_ref[:, sl].astype(F32), gq) * q_scale).astype(BF16)
    kw = n_kv * LANES
    for h in range(n_kv):
        sl = slice(h * LANES, (h + 1) * LANES)
        kh = norm_rope(kv_ref[:, sl], gk)
        k32_ref[:, sl] = kh
        kn_ref[:, sl] = kh.astype(BF16)
    v = kv_ref[:, kw:2 * kw]
    v32_ref[...] = v
    vb_ref[...] = v.astype(BF16)


def _qk_prep(z, kv, gq, gk, rope, seq_len, n_heads, n_kv):
    t = z.shape[0]
    dq = n_heads * LANES
    dk = n_kv * LANES
    tm = 256
    in_specs = [pl.BlockSpec((tm, dq), lambda i: (i, 0)),
                pl.BlockSpec((tm, 2 * dk), lambda i: (i, 0)),
                pl.BlockSpec((1, LANES), lambda i: (0, 0)),
                pl.BlockSpec((1, LANES), lambda i: (0, 0))]
    args = [z, kv, gq, gk]
    if rope is not None:
        per_seq = seq_len // tm
        in_specs += [pl.BlockSpec((tm, LANES), lambda i: (i % per_seq, 0))] * 2
        args += list(rope)
    out_specs = [pl.BlockSpec((tm, dq), lambda i: (i, 0))] + [pl.BlockSpec((tm, dk), lambda i: (i, 0))] * 4
    out_shape = [jax.ShapeDtypeStruct((t, dq), BF16), jax.ShapeDtypeStruct((t, dk), BF16),
                 jax.ShapeDtypeStruct((t, dk), BF16), jax.ShapeDtypeStruct((t, dk), F32),
                 jax.ShapeDtypeStruct((t, dk), F32)]
    return pl.pallas_call(
        functools.partial(_qk_kernel, use_rope=rope is not None, n_heads=n_heads, n_kv=n_kv,
                          q_scale=LANES ** -0.5),
        grid=(t // tm,),
        in_specs=in_specs, out_specs=out_specs, out_shape=out_shape,
        compiler_params=_params(1, 32),
        name="qk_prep",
    )(*args)


def _attn_kernel(*refs, has_cache, group):
    if has_cache:
        q_ref, k_ref, v_ref, ck_ref, cv_ref, o_ref = refs
    else:
        q_ref, k_ref, v_ref, o_ref = refs
    tq = q_ref.shape[0]
    nt = (((1,), (1,)), ((), ()))
    q = jnp.concatenate([q_ref[:, g * LANES:(g + 1) * LANES] for g in range(group)], axis=0)
    s_own = lax.dot_general(q, k_ref[...], nt, preferred_element_type=F32)
    m = jnp.max(s_own, axis=-1, keepdims=True)
    if has_cache:
        s_ctx = lax.dot_general(q, ck_ref[...].astype(BF16), nt, preferred_element_type=F32)
        m = jnp.maximum(m, jnp.max(s_ctx, axis=-1, keepdims=True))
    p = jnp.exp(s_own - m)
    denom = jnp.sum(p, axis=-1, keepdims=True)
    acc = jnp.dot(p.astype(BF16), v_ref[...], preferred_element_type=F32)
    if has_cache:
        pc = jnp.exp(s_ctx - m)
        denom = denom + jnp.sum(pc, axis=-1, keepdims=True)
        acc = acc + jnp.dot(pc.astype(BF16), cv_ref[...].astype(BF16), preferred_element_type=F32)
    o = acc / denom
    for g in range(group):
        o_ref[:, g * LANES:(g + 1) * LANES] = o[g * tq:(g + 1) * tq].astype(o_ref.dtype)


def _attention(qn, kn, vb, cache, batch, seq_len, n_heads, n_kv, tq):
    t = qn.shape[0]
    group = n_heads // n_kv
    nq = seq_len // tq
    in_specs = [pl.BlockSpec((tq, group * LANES), lambda b, h, i: (b * nq + i, h)),
                pl.BlockSpec((seq_len, LANES), lambda b, h, i: (b, h)),
                pl.BlockSpec((seq_len, LANES), lambda b, h, i: (b, h))]
    args = [qn, kn, vb]
    if cache is not None:
        past = cache[0].shape[1]
        in_specs += [pl.BlockSpec((None, past, LANES), lambda b, h, i: (b, 0, h))] * 2
        args += list(cache)
    return pl.pallas_call(
        functools.partial(_attn_kernel, has_cache=cache is not None, group=group),
        grid=(batch, n_kv, nq),
        in_specs=in_specs,
        out_specs=pl.BlockSpec((tq, group * LANES), lambda b, h, i: (b * nq + i, h)),
        out_shape=jax.ShapeDtypeStruct((t, n_heads * LANES), BF16),
        compiler_params=_params(3, 40),
        name="attention",
    )(*args)


CONV_HALO = 16


def _conv_kernel(u_ref, b_ref, c_ref, up_ref, cp_ref, un_ref, cn_ref, w_ref, bias_ref, o_ref, *,
                 tiles_per_seq):
    tm = u_ref.shape[0]
    pos = pl.program_id(0) % tiles_per_seq
    cu = c_ref[...].astype(F32) * u_ref[...].astype(F32)
    halo_prev = (cp_ref[...].astype(F32) * up_ref[...].astype(F32))[CONV_HALO - 1:CONV_HALO, :]
    halo_next = (cn_ref[...].astype(F32) * un_ref[...].astype(F32))[0:1, :]
    halo_prev = jnp.where(pos == 0, 0.0, halo_prev)
    halo_next = jnp.where(pos == tiles_per_seq - 1, 0.0, halo_next)
    row = lax.broadcasted_iota(I32, cu.shape, 0)
    prev = jnp.where(row == 0, halo_prev, pltpu.roll(cu, 1, 0))
    nxt = jnp.where(row == tm - 1, halo_next, pltpu.roll(cu, tm - 1, 0))
    w = w_ref[...]
    conv = prev * w[0:1, :] + cu * w[1:2, :] + nxt * w[2:3, :] + bias_ref[...]
    o_ref[...] = (b_ref[...].astype(F32) * conv).astype(o_ref.dtype)


def _gated_conv(z, conv_w, conv_b, seq_len, col0):
    t = z.shape[0]
    dc = conv_w.shape[1]
    tm = 256
    hb = tm // CONV_HALO
    last = t // CONV_HALO - 1
    prev_map = lambda c: (lambda i: (jnp.maximum(i * hb - 1, 0), c))
    next_map = lambda c: (lambda i: (jnp.minimum((i + 1) * hb, last), c))
    in_specs = [pl.BlockSpec((tm, dc), lambda i: (i, col0)),
                pl.BlockSpec((tm, dc), lambda i: (i, col0 + 1)),
                pl.BlockSpec((tm, dc), lambda i: (i, col0 + 2)),
                pl.BlockSpec((CONV_HALO, dc), prev_map(col0)),
                pl.BlockSpec((CONV_HALO, dc), prev_map(col0 + 2)),
                pl.BlockSpec((CONV_HALO, dc), next_map(col0)),
                pl.BlockSpec((CONV_HALO, dc), next_map(col0 + 2)),
                pl.BlockSpec(conv_w.shape, lambda i: (0, 0)),
                pl.BlockSpec((1, dc), lambda i: (0, 0))]
    return pl.pallas_call(
        functools.partial(_conv_kernel, tiles_per_seq=seq_len // tm),
        grid=(t // tm,),
        in_specs=in_specs,
        out_specs=pl.BlockSpec((tm, dc), lambda i: (i, 0)),
        out_shape=jax.ShapeDtypeStruct((t, dc), BF16),
        compiler_params=_params(1, 32),
        name="gated_conv",
    )(z, z, z, z, z, z, z, conv_w, conv_b)


def _merge_kernel(cv_ref, at_ref, gc_ref, ga_ref, wc_ref, wa_ref, o_ref, wcb_ref, wab_ref):
    @pl.when(pl.program_id(1) == 0)
    def _():
        wcb_ref[...] = wc_ref[...].astype(BF16)
        wab_ref[...] = wa_ref[...].astype(BF16)

    conv_out = jnp.dot(cv_ref[...], wcb_ref[...], preferred_element_type=F32)
    attn_out = jnp.dot(at_ref[...], wab_ref[...], preferred_element_type=F32)
    merged = (jax.nn.sigmoid(gc_ref[...].astype(F32)) * conv_out
              + jax.nn.sigmoid(ga_ref[...].astype(F32)) * attn_out)
    o_ref[...] = merged.astype(o_ref.dtype)


def _merge(cvg, attn, z, w_conv_out, w_attn_out, gc_col0, ga_col0):
    t, dc = cvg.shape
    dq = attn.shape[1]
    d = w_conv_out.shape[1]
    tm, tn = 512, 512
    return pl.pallas_call(
        _merge_kernel,
        grid=(d // tn, t // tm),
        in_specs=[pl.BlockSpec((tm, dc), lambda j, i: (i, 0)),
                  pl.BlockSpec((tm, dq), lambda j, i: (i, 0)),
                  pl.BlockSpec((tm, tn), lambda j, i: (i, gc_col0 + j)),
                  pl.BlockSpec((tm, tn), lambda j, i: (i, ga_col0 + j)),
                  pl.BlockSpec((dc, tn), lambda j, i: (0, j)),
                  pl.BlockSpec((dq, tn), lambda j, i: (0, j))],
        out_specs=pl.BlockSpec((tm, tn), lambda j, i: (i, j)),
        out_shape=jax.ShapeDtypeStruct((t, d), BF16),
        scratch_shapes=[pltpu.VMEM((dc, tn), BF16), pltpu.VMEM((dq, tn), BF16)],
        compiler_params=_params(2, 40),
        name="merge",
    )(cvg, attn, z, z, w_conv_out, w_attn_out)


def _route(logits_t, bias_col, carry):
    n_exp, tm = logits_t.shape
    per = n_exp // N_GROUPS
    assert per == SUBLANES
    neg = -jnp.inf
    scores = jax.nn.sigmoid(logits_t)
    biased = scores + bias_col
    sub = lax.broadcasted_iota(I32, (per, tm), 0).astype(F32)
    xs = [biased[g * per:(g + 1) * per, :] for g in range(N_GROUPS)]
    sc = [scores[g * per:(g + 1) * per, :] for g in range(N_GROUPS)]
    ids = [sub + float(g * per) for g in range(N_GROUPS)]

    def colmax(a):
        return jnp.max(a, axis=0, keepdims=True)

    def colmin(a):
        return jnp.min(a, axis=0, keepdims=True)

    rows = []
    for g in range(N_GROUPS):
        m1 = colmax(xs[g])
        j1 = colmin(jnp.where(xs[g] == m1, sub, float(per)))
        m2 = colmax(jnp.where(sub == j1, neg, xs[g]))
        rows.append(m1 + m2)
    gs = jnp.concatenate(rows, axis=0)
    gsel = jnp.zeros_like(gs)
    for _ in range(TOPK_GROUPS):
        m = colmax(gs)
        j = colmin(jnp.where(gs == m, sub, float(N_GROUPS)))
        hit = sub == j
        gsel = jnp.where(hit, 1.0, gsel)
        gs = jnp.where(hit, neg, gs)
    masked = [jnp.where(gsel[g:g + 1, :] > 0.0, xs[g], neg) for g in range(N_GROUPS)]
    idx_rows, w_rows = [], []
    member = [jnp.zeros((per, tm), F32) for _ in range(N_GROUPS)]
    for _ in range(TOP_K):
        mm = masked[0]
        for g in range(1, N_GROUPS):
            mm = jnp.maximum(mm, masked[g])
        m = colmax(mm)
        idx = colmin(jnp.where(masked[0] == m, ids[0], float(n_exp)))
        for g in range(1, N_GROUPS):
            idx = jnp.minimum(idx, colmin(jnp.where(masked[g] == m, ids[g], float(n_exp))))
        wk = jnp.zeros_like(idx)
        for g in range(N_GROUPS):
            hit = ids[g] == idx
            wk = wk + jnp.sum(jnp.where(hit, sc[g], 0.0), axis=0, keepdims=True)
            masked[g] = jnp.where(hit, neg, masked[g])
            member[g] = jnp.where(hit, 1.0, member[g])
        idx_rows.append(idx)
        w_rows.append(wk)
    w = jnp.concatenate(w_rows, axis=0)
    w = w / jnp.sum(w, axis=0, keepdims=True) * ROUTED_SCALE

    earlier = (lax.broadcasted_iota(I32, (tm, tm), 0) < lax.broadcasted_iota(I32, (tm, tm), 1)).astype(BF16)
    before = jnp.dot(jnp.concatenate(member, axis=0).astype(BF16), earlier, preferred_element_type=F32)
    rank_rows = []
    for k in range(TOP_K):
        rk = jnp.zeros_like(idx_rows[k])
        for g in range(N_GROUPS):
            pos = before[g * per:(g + 1) * per, :] + carry[g]
            rk = rk + jnp.sum(jnp.where(ids[g] == idx_rows[k], pos, 0.0), axis=0, keepdims=True)
        rank_rows.append(rk)
    new_carry = [carry[g] + jnp.sum(member[g], axis=1, keepdims=True) for g in range(N_GROUPS)]
    idx = jnp.concatenate(idx_rows, axis=0).astype(I32)
    rank = jnp.concatenate(rank_rows, axis=0).astype(I32)
    return idx, w, rank, new_carry


def _out_kernel(mg_ref, x_ref, gate_ref, shift_ref, scale_ref, gpost_ref, gpre_ref, wo_ref, wr_ref,
                x1_ref, h2b_ref, h2p_ref, logit_ref, stage_ref, *, tiles_per_row):
    tm = x_ref.shape[0]
    r = pl.program_id(0) // tiles_per_row
    gate = gate_ref[pl.ds(r, 1), :]
    scale = 1.0 + scale_ref[pl.ds(r, 1), :]
    shift = shift_ref[pl.ds(r, 1), :]
    rows = tm // OUT_CHUNKS
    for c in range(OUT_CHUNKS):
        sl = pl.ds(c * rows, rows)
        mix = jnp.dot(mg_ref[sl, :], wo_ref[...], preferred_element_type=F32)
        x1 = x_ref[sl, :] + gate * _rms(mix, gpost_ref[...])
        x1_ref[sl, :] = x1
        h2 = _rms(x1, gpre_ref[...]) * scale + shift
        h2b_ref[sl, :] = h2.astype(BF16)
        _pack_rows(h2, h2p_ref, stage_ref, rows, row0=c * rows * SUBLANES, stage0=c * rows * 2 * SUBLANES)
        logit_ref[:, sl] = lax.dot_general(wr_ref[...], h2, (((1,), (1,)), ((), ())),
                                           preferred_element_type=F32, precision=lax.Precision.HIGHEST)


OUT_CHUNKS = 2


def _out_proj(merged, x, mod, g_post, g_pre, w_out_b, w_router_t):
    t, d = x.shape
    nb = mod.shape[0]
    n_exp = w_router_t.shape[0]
    tm = 512
    row = lambda i: (i, 0)
    fixed = lambda i: (0, 0)
    once = pl.Buffered(1)
    in_specs = [pl.BlockSpec((tm, d), row), pl.BlockSpec((tm, d), row),
                pl.BlockSpec((nb, d), lambda i: (0, 2)),
                pl.BlockSpec((nb, d), lambda i: (0, 3)),
                pl.BlockSpec((nb, d), lambda i: (0, 4)),
                pl.BlockSpec((1, d), fixed), pl.BlockSpec((1, d), fixed),
                pl.BlockSpec((d, d), fixed, pipeline_mode=once),
                pl.BlockSpec((n_exp, d), fixed, pipeline_mode=once)]
    out_specs = [pl.BlockSpec((tm, d), row), pl.BlockSpec((tm, d), row),
                 pl.BlockSpec((tm * SUBLANES, LANES), row),
                 pl.BlockSpec((n_exp, tm), lambda i: (0, i))]
    out_shape = [jax.ShapeDtypeStruct((t, d), F32), jax.ShapeDtypeStruct((t, d), BF16),
                 jax.ShapeDtypeStruct((t * SUBLANES, LANES), I32),
                 jax.ShapeDtypeStruct((n_exp, t), F32)]
    return pl.pallas_call(
        functools.partial(_out_kernel, tiles_per_row=t // nb // tm),
        grid=(t // tm,),
        in_specs=in_specs, out_specs=out_specs, out_shape=out_shape,
        scratch_shapes=[pltpu.VMEM((SUBLANES * 2 * tm, LANES), F32)],
        compiler_params=_params(1, 56),
        name="out_proj",
    )(merged, x, mod, mod, mod, g_post, g_pre, w_out_b, w_router_t)


def _router_kernel(logit_ref, br_ref, idx_ref, wsel_ref, rank_ref, cnt_ref):
    per = SUBLANES

    @pl.when(pl.program_id(0) == 0)
    def _():
        cnt_ref[...] = jnp.zeros_like(cnt_ref)

    carry = [cnt_ref[g * per:(g + 1) * per, 0:1] for g in range(N_GROUPS)]
    idx, w, rank, carry = _route(logit_ref[...], br_ref[...], carry)
    idx_ref[...] = idx
    wsel_ref[...] = w
    rank_ref[...] = rank
    for g in range(N_GROUPS):
        cnt_ref[g * per:(g + 1) * per, :] = jnp.broadcast_to(carry[g], (per, LANES))


def _router(logits_t, b_router_col):
    n_exp, t = logits_t.shape
    tr = 1024
    tile = lambda i: (0, i)
    fixed = lambda i: (0, 0)
    return pl.pallas_call(
        _router_kernel,
        grid=(t // tr,),
        in_specs=[pl.BlockSpec((n_exp, tr), tile), pl.BlockSpec((n_exp, 1), fixed)],
        out_specs=[pl.BlockSpec((TOP_K, tr), tile), pl.BlockSpec((TOP_K, tr), tile),
                   pl.BlockSpec((TOP_K, tr), tile), pl.BlockSpec((n_exp, LANES), fixed)],
        out_shape=[jax.ShapeDtypeStruct((TOP_K, t), I32), jax.ShapeDtypeStruct((TOP_K, t), F32),
                   jax.ShapeDtypeStruct((TOP_K, t), I32), jax.ShapeDtypeStruct((n_exp, LANES), F32)],
        compiler_params=_params(1, 32),
        name="router",
    )(logits_t, b_router_col)


def _dest_kernel(pstart_ref, idx_ref, rank_ref, o_ref):
    idx = idx_ref[...]
    acc = rank_ref[...]
    for e in range(pstart_ref.shape[0]):
        acc = acc + jnp.where(idx == e, pstart_ref[e], 0)
    o_ref[...] = acc


def _dest_slots(pstart, idx, rank):
    return pl.pallas_call(
        _dest_kernel,
        in_specs=[pl.BlockSpec(memory_space=pltpu.SMEM),
                  pl.BlockSpec(memory_space=pltpu.VMEM), pl.BlockSpec(memory_space=pltpu.VMEM)],
        out_specs=pl.BlockSpec(memory_space=pltpu.VMEM),
        out_shape=jax.ShapeDtypeStruct(idx.shape, I32),
        name="dest_slots",
    )(pstart, idx, rank)


SLOT_FILL_UNROLL = 8


def _slot_table_kernel(dest_ref, tab_ref):
    i = pl.program_id(0)
    tm = dest_ref.shape[2]

    @pl.when(i == 0)
    def _():
        def clear(c, carry):
            for u in range(SLOT_FILL_UNROLL):
                tab_ref[c * SLOT_FILL_UNROLL + u] = 0
            return carry
        lax.fori_loop(0, tab_ref.shape[0] // SLOT_FILL_UNROLL, clear, 0)

    def fill(t, carry):
        for k in range(TOP_K):
            tab_ref[dest_ref[0, k, t]] = (i * tm + t) * TOP_K + k
        return carry

    lax.fori_loop(0, tm, fill, 0)


def _slot_table(dest3, n_slots):
    nt, _, tm = dest3.shape
    return pl.pallas_call(
        _slot_table_kernel,
        grid=(nt,),
        in_specs=[pl.BlockSpec((1, TOP_K, tm), lambda i: (i, 0, 0), memory_space=pltpu.SMEM)],
        out_specs=pl.BlockSpec(memory_space=pltpu.SMEM),
        out_shape=jax.ShapeDtypeStruct((n_slots,), I32),
        compiler_params=_params(1, 16),
        name="slot_table",
    )(dest3)


GATHER_UNROLL = 8
GATHER_BLOCK = 1024


def _gather_kernel(tab_ref, src_ref, o_ref):
    n = tab_ref.shape[2]

    def move(c, carry):
        for u in range(GATHER_UNROLL):
            r = c * GATHER_UNROLL + u
            tok = lax.shift_right_logical(tab_ref[0, 0, r], TOP_K_LOG2)
            src = pl.multiple_of(tok * SUBLANES, SUBLANES)
            dst = pl.multiple_of(r * SUBLANES, SUBLANES)
            o_ref[pl.ds(dst, SUBLANES), :] = src_ref[pl.ds(src, SUBLANES), :]
        return carry

    lax.fori_loop(0, n // GATHER_UNROLL, move, 0)


def _gather_rows(slot_tab, h2p, gb):
    n_slots = slot_tab.shape[0]
    nblk = n_slots // gb
    resident = h2p.size * h2p.dtype.itemsize
    return pl.pallas_call(
        _gather_kernel,
        grid=(nblk,),
        in_specs=[pl.BlockSpec((1, 1, gb), lambda b: (b, 0, 0), memory_space=pltpu.SMEM),
                  pl.BlockSpec(memory_space=pltpu.VMEM)],
        out_specs=pl.BlockSpec((gb * SUBLANES, LANES), lambda b: (b, 0)),
        out_shape=jax.ShapeDtypeStruct((n_slots * SUBLANES, LANES), I32),
        compiler_params=_params(1, resident // MIB + 8),
        name="dispatch_gather",
    )(slot_tab.reshape(nblk, 1, gb), h2p)


ROW_DMA_PRIORITY = 1
ROW_SLOTS = 4


def _expert_kernel(first_ref, nblk_ref, xs_hbm, wg_ref, wu_ref, wd_ref, yb_hbm,
                   xbuf_ref, obuf_ref, in_sem, out_sem, wgb_ref, wub_ref, wdb_ref, stage_in_ref, stage_out_ref,
                   *, bm):
    e = pl.program_id(0)
    nb = nblk_ref[e]
    b0 = first_ref[e]
    rows = bm * SUBLANES

    def x_copy(j, slot, base=None):
        blk = (b0 if base is None else base) + j
        return pltpu.make_async_copy(
            xs_hbm.at[pl.ds(pl.multiple_of(blk * rows, rows), rows), :],
            xbuf_ref.at[pl.ds(pl.multiple_of(slot * rows, rows), rows), :], in_sem.at[slot])

    def y_copy(j, slot):
        return pltpu.make_async_copy(
            obuf_ref.at[pl.ds(pl.multiple_of(slot * rows, rows), rows), :],
            yb_hbm.at[pl.ds(pl.multiple_of((b0 + j) * rows, rows), rows), :], out_sem.at[slot])

    def start_head(count, base=None):
        for j in range(ROW_SLOTS - 1):
            @pl.when(count > j)
            def _(j=j):
                x_copy(j, j, base).start(priority=ROW_DMA_PRIORITY)

    @pl.when(e == 0)
    def _():
        start_head(nb)

    @pl.when(nb > 0)
    def _():
        wgb_ref[...] = wg_ref[...].astype(BF16)
        wub_ref[...] = wu_ref[...].astype(BF16)
        wdb_ref[...] = wd_ref[...].astype(BF16)

    def block(j, carry):
        slot = lax.rem(j, ROW_SLOTS)
        x_copy(j, slot).wait()

        ahead = j + (ROW_SLOTS - 1)

        @pl.when(ahead < nb)
        def _():
            x_copy(ahead, lax.rem(ahead, ROW_SLOTS)).start(priority=ROW_DMA_PRIORITY)

        @pl.when(j >= ROW_SLOTS)
        def _():
            y_copy(j - ROW_SLOTS, slot).wait()

        xbase = slot * rows
        los, his = [], []
        for s in range(SUBLANES):
            lo, hi = _unpack_word(xbuf_ref[pl.ds(xbase + s, bm, stride=SUBLANES), :], stage_in_ref, s)
            los.append(lo.astype(BF16))
            his.append(hi.astype(BF16))
        x = jnp.concatenate(los + his, axis=1)
        g = jnp.dot(x, wgb_ref[...], preferred_element_type=F32)
        u = jnp.dot(x, wub_ref[...], preferred_element_type=F32)
        a = (_silu(g) * u).astype(BF16)
        y = jnp.dot(a, wdb_ref[...], preferred_element_type=F32)
        _pack_rows(y, obuf_ref, stage_out_ref, bm, row0=xbase)
        y_copy(j, slot).start()
        return carry

    lax.fori_loop(0, nb, block, 0)

    @pl.when(e + 1 < pl.num_programs(0))
    def _():
        nxt = jnp.minimum(e + 1, pl.num_programs(0) - 1)
        start_head(nblk_ref[nxt], first_ref[nxt])

    for back in range(ROW_SLOTS, 0, -1):
        @pl.when(nb >= back)
        def _(back=back):
            y_copy(nb - back, lax.rem(nb - back, ROW_SLOTS)).wait()

    @pl.when(e == pl.num_programs(0) - 1)
    def _():
        obuf_ref[pl.ds(0, rows), :] = pltpu.bitcast(jnp.zeros((2 * rows, LANES), BF16), I32)

        def clear(j, carry):
            cp = pltpu.make_async_copy(obuf_ref.at[pl.ds(0, rows), :],
                                       yb_hbm.at[pl.ds(pl.multiple_of(j * rows, rows), rows), :], out_sem.at[0])
            cp.start()
            cp.wait()
            return carry

        lax.fori_loop(b0 + nb, yb_hbm.shape[0] // rows, clear, 0)


def _experts(first_block, n_blocks_e, xs, w_gate, w_up, w_down, bm):
    n_exp, d, de = w_gate.shape
    rows = bm * SUBLANES
    grid_spec = pltpu.PrefetchScalarGridSpec(
        num_scalar_prefetch=2,
        grid=(n_exp,),
        in_specs=[pl.BlockSpec(memory_space=pl.ANY),
                  pl.BlockSpec((None, d, de), lambda e, fb, nb: (e, 0, 0)),
                  pl.BlockSpec((None, d, de), lambda e, fb, nb: (e, 0, 0)),
                  pl.BlockSpec((None, de, d), lambda e, fb, nb: (e, 0, 0))],
        out_specs=pl.BlockSpec(memory_space=pl.ANY),
        scratch_shapes=[pltpu.VMEM((ROW_SLOTS * rows, LANES), I32), pltpu.VMEM((ROW_SLOTS * rows, LANES), I32),
                        pltpu.SemaphoreType.DMA((ROW_SLOTS,)), pltpu.SemaphoreType.DMA((ROW_SLOTS,)),
                        pltpu.VMEM((d, de), BF16), pltpu.VMEM((d, de), BF16), pltpu.VMEM((de, d), BF16),
                        pltpu.VMEM((SUBLANES * 2 * bm, LANES), F32), pltpu.VMEM((SUBLANES * 2 * bm, LANES), F32)],
    )
    return pl.pallas_call(
        functools.partial(_expert_kernel, bm=bm),
        grid_spec=grid_spec,
        out_shape=jax.ShapeDtypeStruct(xs.shape, I32),
        compiler_params=_params(1, 56),
        name="routed_experts",
    )(first_block, n_blocks_e, xs, w_gate, w_up, w_down)


def _final_kernel(dest_ref, dnext_ref, w_ref, yb_hbm, x1_ref, h2_ref, gate_ref, gpost_ref, wsg_ref, wsu_ref,
                  wsd_ref, o_ref, buf_ref, sem, stage_ref, *, tiles_per_row):
    tm = x1_ref.shape[0]
    i = pl.program_id(0)
    last = pl.num_programs(0) - 1
    r = i // tiles_per_row
    tile_rows = TOP_K * tm * SUBLANES
    cur = (i % 2) * tile_rows
    nxt = tile_rows - cur

    def request(dref, base, slot, t, k):
        src = pl.multiple_of(dref[0, k, t] * SUBLANES, SUBLANES)
        dst = pl.multiple_of(base + (k * tm + t) * SUBLANES, SUBLANES)
        pltpu.make_async_copy(yb_hbm.at[pl.ds(src, SUBLANES), :], buf_ref.at[pl.ds(dst, SUBLANES), :],
                              sem.at[slot]).start()

    def wait_tile(base, slot):
        pltpu.make_async_copy(yb_hbm.at[pl.ds(0, tile_rows), :],
                              buf_ref.at[pl.ds(pl.multiple_of(base, tile_rows), tile_rows), :], sem.at[slot]).wait()

    @pl.when(i == 0)
    def _():
        def first(t, carry):
            for k in range(TOP_K):
                request(dest_ref, 0, 0, t, k)
            return carry
        lax.fori_loop(0, tm, first, 0)

    wait_tile(cur, i % 2)

    pending = [(t, k) for t in range(tm) for k in range(TOP_K)]
    per_piece = len(pending) // (SUBLANES * TOP_K)

    def request_piece(n):
        for t, k in pending[n * per_piece:(n + 1) * per_piece]:
            request(dnext_ref, nxt, (i + 1) % 2, t, k)

    h = h2_ref[...]
    g = jnp.dot(h, wsg_ref[...], preferred_element_type=F32)
    u = jnp.dot(h, wsu_ref[...], preferred_element_type=F32)
    shared = jnp.dot((_silu(g) * u).astype(BF16), wsd_ref[...], preferred_element_type=F32)

    w = w_ref[...]
    los, his = [], []
    for s in range(SUBLANES):
        lo_acc = jnp.zeros((tm, LANES), F32)
        hi_acc = jnp.zeros((tm, LANES), F32)
        for k in range(TOP_K):
            lo, hi = _unpack_word(buf_ref[pl.ds(cur + k * tm * SUBLANES + s, tm, stride=SUBLANES), :],
                                  stage_ref, s * TOP_K + k)
            wk = w[:, k:k + 1]
            lo_acc = lo_acc + wk * lo
            hi_acc = hi_acc + wk * hi
            request_piece(s * TOP_K + k)
        los.append(lo_acc)
        his.append(hi_acc)
    ffn = jnp.concatenate(los + his, axis=1) + shared
    o_ref[...] = x1_ref[...] + gate_ref[pl.ds(r, 1), :] * _rms(ffn, gpost_ref[...])

    @pl.when(i == last)
    def _():
        wait_tile(nxt, (i + 1) % 2)


def _final(dest, w_tok, yb, x1, h2b, mod, g_post, ws_gate, ws_up, ws_down):
    t, d = x1.shape
    nb = mod.shape[0]
    ds_ = ws_gate.shape[1]
    tm = dest.shape[2]
    n_tiles = t // tm
    row = lambda i: (i, 0)
    fixed = lambda i: (0, 0)
    return pl.pallas_call(
        functools.partial(_final_kernel, tiles_per_row=t // nb // tm),
        grid=(n_tiles,),
        in_specs=[pl.BlockSpec((1, TOP_K, tm), lambda i: (i, 0, 0), memory_space=pltpu.SMEM),
                  pl.BlockSpec((1, TOP_K, tm), lambda i: (jnp.minimum(i + 1, n_tiles - 1), 0, 0),
                               memory_space=pltpu.SMEM),
                  pl.BlockSpec((tm, TOP_K), row),
                  pl.BlockSpec(memory_space=pl.ANY),
                  pl.BlockSpec((tm, d), row), pl.BlockSpec((tm, d), row),
                  pl.BlockSpec((nb, d), lambda i: (0, 5)),
                  pl.BlockSpec((1, d), fixed),
                  pl.BlockSpec((d, ds_), fixed), pl.BlockSpec((d, ds_), fixed), pl.BlockSpec((ds_, d), fixed)],
        out_specs=pl.BlockSpec((tm, d), row),
        out_shape=jax.ShapeDtypeStruct((t, d), F32),
        scratch_shapes=[pltpu.VMEM((2 * TOP_K * tm * SUBLANES, LANES), I32), pltpu.SemaphoreType.DMA((2,)),
                        pltpu.VMEM((SUBLANES * TOP_K * 2 * tm, LANES), F32)],
        compiler_params=_params(1, 48),
        name="combine_final",
    )(dest, dest, w_tok, yb, x1, h2b, mod, g_post, ws_gate, ws_up, ws_down)


def _rope_tables(n_pos):
    half = LANES // 2
    pos = jnp.arange(n_pos)
    row = (pos // GRID_W).astype(F32)
    col = (pos % GRID_W).astype(F32)
    inv = ROPE_THETA ** (-jnp.arange(0, half, 2, dtype=F32) / half)
    ang = jnp.concatenate([row[:, None] * inv, col[:, None] * inv], axis=-1)
    cos = jnp.repeat(jnp.cos(ang), 2, axis=-1)
    sin = jnp.repeat(jnp.sin(ang), 2, axis=-1) * jnp.tile(jnp.array([-1.0, 1.0], F32), half)
    return cos, sin


def _sublayer1(x, mod, p, seq_len, rope, cache, n_heads, n_kv, tq):
    t = x.shape[0]
    h = _prenorm(x, mod, p['g_pre1'], 0, 1)
    z = _matmul_wcast(h, p['w_in'], _z_col_map, 9 * 1024, BF16)
    kv = _matmul_wcast(h, p['w_in'], lambda j: j + 5, 1024, F32)
    qn, kn, vb, k32, v32 = _qk_prep(z, kv, p['g_q'], p['g_k'], rope, seq_len, n_heads, n_kv)
    attn = _attention(qn, kn, vb, cache, t // seq_len, seq_len, n_heads, n_kv, tq)
    cvg = _gated_conv(z, p['conv_w'], p['conv_b'], seq_len, 6)
    merged = _merge(cvg, attn, z, p['w_conv_out'], p['w_attn_out'], 4, 8)
    x1, h2b, h2p, logits_t = _out_proj(merged, x, mod, p['g_post1'], p['g_pre2'], p['w_out_b'], p['w_router_t'])
    return x1, h2b, h2p, logits_t, k32, v32


def kernel(x_prompt, x_sample, cache_k, cache_v, c, c_ctx, w_mod, b_mod, g_pre1, w_in, conv_w, conv_b, g_q, g_k, w_conv_out, w_attn_out, w_out, g_post1, g_pre2, w_router, b_router, w_e_gate, w_e_up, w_e_down, w_s_gate, w_s_up, w_s_down, g_post2):
    batch, seq, d = x_prompt.shape
    dec_batch, dec_seq, _ = x_sample.shape
    depth = w_mod.shape[0]
    assert depth == 1
    past, n_kv, head_dim = cache_k.shape[2:]
    assert head_dim == LANES
    n_heads = w_attn_out.shape[1] // head_dim
    n_exp = w_router.shape[2]
    t_ctx, t_lat = batch * seq, dec_batch * dec_seq
    t_all = t_ctx + t_lat
    l = 0

    p = {
        'g_pre1': g_pre1[l][None], 'w_in': w_in[l], 'conv_w': conv_w[l], 'conv_b': conv_b[l][None],
        'g_q': g_q[l][None], 'g_k': g_k[l][None], 'w_conv_out': w_conv_out[l], 'w_attn_out': w_attn_out[l],
        'w_out_b': w_out[l].astype(BF16), 'g_post1': g_post1[l][None], 'g_pre2': g_pre2[l][None],
        'w_router_t': w_router[l].T,
    }
    cond = jnp.concatenate([c_ctx[None], c, jnp.zeros((SUBLANES - 1 - dec_batch, d), F32)], axis=0)
    mod = _modulation(cond, w_mod[l], b_mod[l][None])
    mod_ctx, mod_lat = mod[0:1], mod[1:1 + dec_batch]

    xc = x_prompt.reshape(t_ctx, d)
    xl = x_sample.reshape(t_lat, d)
    cache = (cache_k[:, l].reshape(dec_batch, past, n_kv * head_dim),
             cache_v[:, l].reshape(dec_batch, past, n_kv * head_dim))
    rope = _rope_tables(dec_seq)

    x1c, h2bc, h2pc, logc, k32, v32 = _sublayer1(xc, mod_ctx, p, seq, None, None, n_heads, n_kv, 256)
    x1l, h2bl, h2pl, logl, _, _ = _sublayer1(xl, mod_lat, p, dec_seq, rope, cache, n_heads, n_kv, 256)
    idx_all, w_all, rank_all, cnt = _router(jnp.concatenate([logc, logl], axis=1), b_router[l][:, None])

    bm = 256
    n_blocks = (t_all * TOP_K + n_exp * (bm - 1)) // bm + 1
    n_slots = n_blocks * bm
    blocks_e = jnp.floor((cnt[:, 0] + (bm - 1)) / bm)
    first_e = jnp.cumsum(blocks_e) - blocks_e
    pstart = (first_e * bm).astype(I32)

    tmf = 128
    w_tok = w_all.T
    dest = _dest_slots(pstart, idx_all, rank_all)
    dest3 = dest.reshape(TOP_K, t_all // tmf, tmf).transpose(1, 0, 2)
    slot_tab = _slot_table(dest3, n_slots)

    h2p = jnp.concatenate([h2pc, h2pl], axis=0)
    xs = _gather_rows(slot_tab, h2p, GATHER_BLOCK)
    yb = _experts(first_e.astype(I32), blocks_e.astype(I32), xs, w_e_gate[l], w_e_up[l], w_e_down[l], bm)

    ws = (w_s_gate[l].astype(BF16), w_s_up[l].astype(BF16), w_s_down[l].astype(BF16))

    def finish(lo, n, x1, h2b, mod_g):
        dest_g = dest3[lo // tmf:(lo + n) // tmf]
        return _final(dest_g, w_tok[lo:lo + n], yb, x1, h2b, mod_g, g_post2[l][None], *ws)

    y_ctx = finish(0, t_ctx, x1c, h2bc, mod_ctx)
    y_lat = finish(t_ctx, t_lat, x1l, h2bl, mod_lat)

    new_k = k32.reshape(batch, 1, seq, n_kv, head_dim)
    new_v = v32.reshape(batch, 1, seq, n_kv, head_dim)
    return (y_ctx.reshape(batch, seq, d), y_lat.reshape(dec_batch, dec_seq, d), new_k, new_v)
```

```python
import functools

import jax
import jax.numpy as jnp
from jax import lax
from jax.experimental import pallas as pl
from jax.experimental.pallas import tpu as pltpu

GRID_W = 64
ROPE_THETA = 10000.0
N_GROUPS = 8
TOPK_GROUPS = 4
TOP_K = 8
TOP_K_LOG2 = 3
ROUTED_SCALE = 2.5
EPS = 1e-6

LANES = 128
SUBLANES = 8
V7X_VMEM_BYTES = 64 * 1024 * 1024
MIB = 1024 * 1024

F32 = jnp.float32
BF16 = jnp.bfloat16
I32 = jnp.int32


def _params(n_grid, vmem_mib):
    assert vmem_mib * MIB < V7X_VMEM_BYTES
    return pltpu.CompilerParams(
        dimension_semantics=("arbitrary",) * n_grid, vmem_limit_bytes=vmem_mib * MIB)


def _silu(x):
    return x * jax.nn.sigmoid(x)


def _rms(x, g):
    return x * lax.rsqrt(jnp.mean(x * x, axis=-1, keepdims=True) + EPS) * g


def _pack_rows(val, out_ref, stage_ref, rows, row0=0, stage0=0):
    half = val.shape[1] // 2
    for s in range(half // LANES):
        base = stage0 + s * 2 * rows
        stage_ref[pl.ds(base, rows, stride=2), :] = val[:, s * LANES:(s + 1) * LANES]
        stage_ref[pl.ds(base + 1, rows, stride=2), :] = val[:, half + s * LANES:half + (s + 1) * LANES]
        pair = stage_ref[pl.ds(base, 2 * rows), :].astype(BF16)
        out_ref[pl.ds(row0 + s, rows, stride=SUBLANES), :] = pltpu.bitcast(pair, I32)


def _unpack_word(word, stage_ref, slot):
    rows = word.shape[0]
    base = slot * 2 * rows
    stage_ref[pl.ds(base, 2 * rows), :] = pltpu.bitcast(word, BF16).astype(F32)
    return stage_ref[pl.ds(base, rows, stride=2), :], stage_ref[pl.ds(base + 1, rows, stride=2), :]


def _mod_kernel(c_ref, w_ref, b_ref, o_ref):
    s = _silu(c_ref[...]).astype(BF16)
    o_ref[...] = jnp.dot(s, w_ref[...].astype(BF16), preferred_element_type=F32) + b_ref[...]


def _modulation(cond, w, b):
    rows, d = cond.shape
    n = w.shape[1]
    tn = 1024
    return pl.pallas_call(
        _mod_kernel,
        grid=(n // tn,),
        in_specs=[pl.BlockSpec((rows, d), lambda j: (0, 0)),
                  pl.BlockSpec((d, tn), lambda j: (0, j)),
                  pl.BlockSpec((1, tn), lambda j: (0, j))],
        out_specs=pl.BlockSpec((rows, tn), lambda j: (0, j)),
        out_shape=jax.ShapeDtypeStruct((rows, n), F32),
        compiler_params=_params(1, 40),
        name="modulation",
    )(cond, w, b)


def _prenorm_kernel(x_ref, shift_ref, scale_ref, g_ref, o_ref, *, tiles_per_row):
    r = pl.program_id(0) // tiles_per_row
    y = _rms(x_ref[...], g_ref[...])
    o_ref[...] = (y * (1.0 + scale_ref[pl.ds(r, 1), :]) + shift_ref[pl.ds(r, 1), :]).astype(o_ref.dtype)


def _prenorm(x, mod, g, shift_col, scale_col):
    t, d = x.shape
    nb = mod.shape[0]
    tm = 512
    return pl.pallas_call(
        functools.partial(_prenorm_kernel, tiles_per_row=t // nb // tm),
        grid=(t // tm,),
        in_specs=[pl.BlockSpec((tm, d), lambda i: (i, 0)),
                  pl.BlockSpec((nb, d), lambda i: (0, shift_col)),
                  pl.BlockSpec((nb, d), lambda i: (0, scale_col)),
                  pl.BlockSpec((1, d), lambda i: (0, 0))],
        out_specs=pl.BlockSpec((tm, d), lambda i: (i, 0)),
        out_shape=jax.ShapeDtypeStruct((t, d), BF16),
        compiler_params=_params(1, 32),
        name="prenorm",
    )(x, mod, mod, g)


def _mm_wcast_kernel(a_ref, w_ref, o_ref, wb_ref):
    @pl.when(pl.program_id(1) == 0)
    def _():
        wb_ref[...] = w_ref[...].astype(BF16)

    o_ref[...] = jnp.dot(a_ref[...], wb_ref[...], preferred_element_type=F32).astype(o_ref.dtype)


def _matmul_wcast(a, w, col_map, n_out, out_dtype):
    m, k = a.shape
    tm, tn = 1024, 1024
    return pl.pallas_call(
        _mm_wcast_kernel,
        grid=(n_out // tn, m // tm),
        in_specs=[pl.BlockSpec((tm, k), lambda j, i: (i, 0)),
                  pl.BlockSpec((k, tn), lambda j, i: (0, col_map(j)))],
        out_specs=pl.BlockSpec((tm, tn), lambda j, i: (i, j)),
        out_shape=jax.ShapeDtypeStruct((m, n_out), out_dtype),
        scratch_shapes=[pltpu.VMEM((k, tn), BF16)],
        compiler_params=_params(2, 48),
        name="in_proj",
    )(a, w)


def _z_col_map(j):
    return jnp.where(j < 2, j + 3, jnp.where(j < 6, j + 4, j - 6))


def _qk_kernel(*refs, use_rope, n_heads, n_kv, q_scale):
    if use_rope:
        q_ref, kv_ref, gq_ref, gk_ref, cos_ref, sin_ref, qn_ref, kn_ref, vb_ref, k32_ref, v32_ref = refs
        cos = cos_ref[...]
        sin = sin_ref[...]
        even = lax.broadcasted_iota(I32, cos.shape, 1) % 2 == 0
    else:
        q_ref, kv_ref, gq_ref, gk_ref, qn_ref, kn_ref, vb_ref, k32_ref, v32_ref = refs

    def norm_rope(xh, g):
        y = _rms(xh, g)
        if use_rope:
            sw = jnp.where(even, pltpu.roll(y, LANES - 1, 1), pltpu.roll(y, 1, 1))
            y = y * cos + sw * sin
        return y

    gq = gq_ref[...]
    gk = gk_ref[...]
    for h in range(n_heads):
        sl = slice(h * LANES, (h + 1) * LANES)
        qn_ref[:, sl] = (norm_rope(q_ref[:, sl].astype(F32), gq) * q_scale).astype(BF16)
    kw = n_kv * LANES
    for h in range(n_kv):
        sl = slice(h * LANES, (h + 1) * LANES)
        kh = norm_rope(kv_ref[:, sl], gk)
        k32_ref[:, sl] = kh
        kn_ref[:, sl] = kh.astype(BF16)
    v = kv_ref[:, kw:2 * kw]
    v32_ref[...] = v
    vb_ref[...] = v.astype(BF16)


def _qk_prep(z, kv, gq, gk, rope, seq_len, n_heads, n_kv):
    t = z.shape[0]
    dq = n_heads * LANES
    dk = n_kv * LANES
    tm = 256
    in_specs = [pl.BlockSpec((tm, dq), lambda i: (i, 0)),
                pl.BlockSpec((tm, 2 * dk), lambda i: (i, 0)),
                pl.BlockSpec((1, LANES), lambda i: (0, 0)),
                pl.BlockSpec((1, LANES), lambda i: (0, 0))]
    args = [z, kv, gq, gk]
    if rope is not None:
        per_seq = seq_len // tm
        in_specs += [pl.BlockSpec((tm, LANES), lambda i: (i % per_seq, 0))] * 2
        args += list(rope)
    out_specs = [pl.BlockSpec((tm, dq), lambda i: (i, 0))] + [pl.BlockSpec((tm, dk), lambda i: (i, 0))] * 4
    out_shape = [jax.ShapeDtypeStruct((t, dq), BF16), jax.ShapeDtypeStruct((t, dk), BF16),
                 jax.ShapeDtypeStruct((t, dk), BF16), jax.ShapeDtypeStruct((t, dk), F32),
                 jax.ShapeDtypeStruct((t, dk), F32)]
    return pl.pallas_call(
        functools.partial(_qk_kernel, use_rope=rope is not None, n_heads=n_heads, n_kv=n_kv,
                          q_scale=LANES ** -0.5),
        grid=(t // tm,),
        in_specs=in_specs, out_specs=out_specs, out_shape=out_shape,
        compiler_params=_params(1, 32),
        name="qk_prep",
    )(*args)


def _attn_kernel(*refs, has_cache, group):
    if has_cache:
        q_ref, k_ref, v_ref, ck_ref, cv_ref, o_ref = refs
    else:
        q_ref, k_ref, v_ref, o_ref = refs
    tq = q_ref.shape[0]
    nt = (((1,), (1,)), ((), ()))
    q = jnp.concatenate([q_ref[:, g * LANES:(g + 1) * LANES] for g in range(group)], axis=0)
    s_own = lax.dot_general(q, k_ref[...], nt, preferred_element_type=F32)
    m = jnp.max(s_own, axis=-1, keepdims=True)
    if has_cache:
        s_ctx = lax.dot_general(q, ck_ref[...].astype(BF16), nt, preferred_element_type=F32)
        m = jnp.maximum(m, jnp.max(s_ctx, axis=-1, keepdims=True))
    p = jnp.exp(s_own - m)
    denom = jnp.sum(p, axis=-1, keepdims=True)
    acc = jnp.dot(p.astype(BF16), v_ref[...], preferred_element_type=F32)
    if has_cache:
        pc = jnp.exp(s_ctx - m)
        denom = denom + jnp.sum(pc, axis=-1, keepdims=True)
        acc = acc + jnp.dot(pc.astype(BF16), cv_ref[...].astype(BF16), preferred_element_type=F32)
    o = acc / denom
    for g in range(group):
        o_ref[:, g * LANES:(g + 1) * LANES] = o[g * tq:(g + 1) * tq].astype(o_ref.dtype)


def _attention(qn, kn, vb, cache, batch, seq_len, n_heads, n_kv, tq):
    t = qn.shape[0]
    group = n_heads // n_kv
    nq = seq_len // tq
    in_specs = [pl.BlockSpec((tq, group * LANES), lambda b, h, i: (b * nq + i, h)),
                pl.BlockSpec((seq_len, LANES), lambda b, h, i: (b, h)),
                pl.BlockSpec((seq_len, LANES), lambda b, h, i: (b, h))]
    args = [qn, kn, vb]
    if cache is not None:
        past = cache[0].shape[1]
        in_specs += [pl.BlockSpec((None, past, LANES), lambda b, h, i: (b, 0, h))] * 2
        args += list(cache)
    return pl.pallas_call(
        functools.partial(_attn_kernel, has_cache=cache is not None, group=group),
        grid=(batch, n_kv, nq),
        in_specs=in_specs,
        out_specs=pl.BlockSpec((tq, group * LANES), lambda b, h, i: (b * nq + i, h)),
        out_shape=jax.ShapeDtypeStruct((t, n_heads * LANES), BF16),
        compiler_params=_params(3, 40),
        name="attention",
    )(*args)


CONV_HALO = 16


def _conv_kernel(u_ref, b_ref, c_ref, up_ref, cp_ref, un_ref, cn_ref, w_ref, bias_ref, o_ref, *,
                 tiles_per_seq):
    tm = u_ref.shape[0]
    pos = pl.program_id(0) % tiles_per_seq
    cu = c_ref[...].astype(F32) * u_ref[...].astype(F32)
    halo_prev = (cp_ref[...].astype(F32) * up_ref[...].astype(F32))[CONV_HALO - 1:CONV_HALO, :]
    halo_next = (cn_ref[...].astype(F32) * un_ref[...].astype(F32))[0:1, :]
    halo_prev = jnp.where(pos == 0, 0.0, halo_prev)
    halo_next = jnp.where(pos == tiles_per_seq - 1, 0.0, halo_next)
    row = lax.broadcasted_iota(I32, cu.shape, 0)
    prev = jnp.where(row == 0, halo_prev, pltpu.roll(cu, 1, 0))
    nxt = jnp.where(row == tm - 1, halo_next, pltpu.roll(cu, tm - 1, 0))
    w = w_ref[...]
    conv = prev * w[0:1, :] + cu * w[1:2, :] + nxt * w[2:3, :] + bias_ref[...]
    o_ref[...] = (b_ref[...].astype(F32) * conv).astype(o_ref.dtype)


def _gated_conv(z, conv_w, conv_b, seq_len, col0):
    t = z.shape[0]
    dc = conv_w.shape[1]
    tm = 256
    hb = tm // CONV_HALO
    last = t // CONV_HALO - 1
    prev_map = lambda c: (lambda i: (jnp.maximum(i * hb - 1, 0), c))
    next_map = lambda c: (lambda i: (jnp.minimum((i + 1) * hb, last), c))
    in_specs = [pl.BlockSpec((tm, dc), lambda i: (i, col0)),
                pl.BlockSpec((tm, dc), lambda i: (i, col0 + 1)),
                pl.BlockSpec((tm, dc), lambda i: (i, col0 + 2)),
                pl.BlockSpec((CONV_HALO, dc), prev_map(col0)),
                pl.BlockSpec((CONV_HALO, dc), prev_map(col0 + 2)),
                pl.BlockSpec((CONV_HALO, dc), next_map(col0)),
                pl.BlockSpec((CONV_HALO, dc), next_map(col0 + 2)),
                pl.BlockSpec(conv_w.shape, lambda i: (0, 0)),
                pl.BlockSpec((1, dc), lambda i: (0, 0))]
    return pl.pallas_call(
        functools.partial(_conv_kernel, tiles_per_seq=seq_len // tm),
        grid=(t // tm,),
        in_specs=in_specs,
        out_specs=pl.BlockSpec((tm, dc), lambda i: (i, 0)),
        out_shape=jax.ShapeDtypeStruct((t, dc), BF16),
        compiler_params=_params(1, 32),
        name="gated_conv",
    )(z, z, z, z, z, z, z, conv_w, conv_b)


def _merge_kernel(cv_ref, at_ref, gc_ref, ga_ref, wc_ref, wa_ref, o_ref, wcb_ref, wab_ref):
    @pl.when(pl.program_id(1) == 0)
    def _():
        wcb_ref[...] = wc_ref[...].astype(BF16)
        wab_ref[...] = wa_ref[...].astype(BF16)

    conv_out = jnp.dot(cv_ref[...], wcb_ref[...], preferred_element_type=F32)
    attn_out = jnp.dot(at_ref[...], wab_ref[...], preferred_element_type=F32)
    merged = (jax.nn.sigmoid(gc_ref[...].astype(F32)) * conv_out
              + jax.nn.sigmoid(ga_ref[...].astype(F32)) * attn_out)
    o_ref[...] = merged.astype(o_ref.dtype)


def _merge(cvg, attn, z, w_conv_out, w_attn_out, gc_col0, ga_col0):
    t, dc = cvg.shape
    dq = attn.shape[1]
    d = w_conv_out.shape[1]
    tm, tn = 512, 512
    return pl.pallas_call(
        _merge_kernel,
        grid=(d // tn, t // tm),
        in_specs=[pl.BlockSpec((tm, dc), lambda j, i: (i, 0)),
                  pl.BlockSpec((tm, dq), lambda j, i: (i, 0)),
                  pl.BlockSpec((tm, tn), lambda j, i: (i, gc_col0 + j)),
                  pl.BlockSpec((tm, tn), lambda j, i: (i, ga_col0 + j)),
                  pl.BlockSpec((dc, tn), lambda j, i: (0, j)),
                  pl.BlockSpec((dq, tn), lambda j, i: (0, j))],
        out_specs=pl.BlockSpec((tm, tn), lambda j, i: (i, j)),
        out_shape=jax.ShapeDtypeStruct((t, d), BF16),
        scratch_shapes=[pltpu.VMEM((dc, tn), BF16), pltpu.VMEM((dq, tn), BF16)],
        compiler_params=_params(2, 40),
        name="merge",
    )(cvg, attn, z, z, w_conv_out, w_attn_out)


def _route(logits_t, bias_col, carry):
    n_exp, tm = logits_t.shape
    per = n_exp // N_GROUPS
    assert per == SUBLANES
    neg = -jnp.inf
    scores = jax.nn.sigmoid(logits_t)
    biased = scores + bias_col
    sub = lax.broadcasted_iota(I32, (per, tm), 0).astype(F32)
    xs = [biased[g * per:(g + 1) * per, :] for g in range(N_GROUPS)]
    sc = [scores[g * per:(g + 1) * per, :] for g in range(N_GROUPS)]
    ids = [sub + float(g * per) for g in range(N_GROUPS)]

    def colmax(a):
        return jnp.max(a, axis=0, keepdims=True)

    def colmin(a):
        return jnp.min(a, axis=0, keepdims=True)

    rows = []
    for g in range(N_GROUPS):
        m1 = colmax(xs[g])
        j1 = colmin(jnp.where(xs[g] == m1, sub, float(per)))
        m2 = colmax(jnp.where(sub == j1, neg, xs[g]))
        rows.append(m1 + m2)
    gs = jnp.concatenate(rows, axis=0)
    gsel = jnp.zeros_like(gs)
    for _ in range(TOPK_GROUPS):
        m = colmax(gs)
        j = colmin(jnp.where(gs == m, sub, float(N_GROUPS)))
        hit = sub == j
        gsel = jnp.where(hit, 1.0, gsel)
        gs = jnp.where(hit, neg, gs)
    masked = [jnp.where(gsel[g:g + 1, :] > 0.0, xs[g], neg) for g in range(N_GROUPS)]
    idx_rows, w_rows = [], []
    member = [jnp.zeros((per, tm), F32) for _ in range(N_GROUPS)]
    for _ in range(TOP_K):
        mm = masked[0]
        for g in range(1, N_GROUPS):
            mm = jnp.maximum(mm, masked[g])
        m = colmax(mm)
        idx = colmin(jnp.where(masked[0] == m, ids[0], float(n_exp)))
        for g in range(1, N_GROUPS):
            idx = jnp.minimum(idx, colmin(jnp.where(masked[g] == m, ids[g], float(n_exp))))
        wk = jnp.zeros_like(idx)
        for g in range(N_GROUPS):
            hit = ids[g] == idx
            wk = wk + jnp.sum(jnp.where(hit, sc[g], 0.0), axis=0, keepdims=True)
            masked[g] = jnp.where(hit, neg, masked[g])
            member[g] = jnp.where(hit, 1.0, member[g])
        idx_rows.append(idx)
        w_rows.append(wk)
    w = jnp.concatenate(w_rows, axis=0)
    w = w / jnp.sum(w, axis=0, keepdims=True) * ROUTED_SCALE

    earlier = (lax.broadcasted_iota(I32, (tm, tm), 0) < lax.broadcasted_iota(I32, (tm, tm), 1)).astype(BF16)
    before = jnp.dot(jnp.concatenate(member, axis=0).astype(BF16), earlier, preferred_element_type=F32)
    rank_rows = []
    for k in range(TOP_K):
        rk = jnp.zeros_like(idx_rows[k])
        for g in range(N_GROUPS):
            pos = before[g * per:(g + 1) * per, :] + carry[g]
            rk = rk + jnp.sum(jnp.where(ids[g] == idx_rows[k], pos, 0.0), axis=0, keepdims=True)
        rank_rows.append(rk)
    new_carry = [carry[g] + jnp.sum(member[g], axis=1, keepdims=True) for g in range(N_GROUPS)]
    idx = jnp.concatenate(idx_rows, axis=0).astype(I32)
    rank = jnp.concatenate(rank_rows, axis=0).astype(I32)
    return idx, w, rank, new_carry


def _out_kernel(mg_ref, x_ref, gate_ref, shift_ref, scale_ref, gpost_ref, gpre_ref, wo_ref, wr_ref,
                x1_ref, h2b_ref, h2p_ref, logit_ref, stage_ref, *, tiles_per_row):
    tm = x_ref.shape[0]
    r = pl.program_id(0) // tiles_per_row
    gate = gate_ref[pl.ds(r, 1), :]
    scale = 1.0 + scale_ref[pl.ds(r, 1), :]
    shift = shift_ref[pl.ds(r, 1), :]
    rows = tm // OUT_CHUNKS
    for c in range(OUT_CHUNKS):
        sl = pl.ds(c * rows, rows)
        mix = jnp.dot(mg_ref[sl, :], wo_ref[...], preferred_element_type=F32)
        x1 = x_ref[sl, :] + gate * _rms(mix, gpost_ref[...])
        x1_ref[sl, :] = x1
        h2 = _rms(x1, gpre_ref[...]) * scale + shift
        h2b_ref[sl, :] = h2.astype(BF16)
        _pack_rows(h2, h2p_ref, stage_ref, rows, row0=c * rows * SUBLANES, stage0=c * rows * 2 * SUBLANES)
        logit_ref[:, sl] = lax.dot_general(wr_ref[...], h2, (((1,), (1,)), ((), ())),
                                           preferred_element_type=F32, precision=lax.Precision.HIGHEST)


OUT_CHUNKS = 2


def _out_proj(merged, x, mod, g_post, g_pre, w_out_b, w_router_t):
    t, d = x.shape
    nb = mod.shape[0]
    n_exp = w_router_t.shape[0]
    tm = 512
    row = lambda i: (i, 0)
    fixed = lambda i: (0, 0)
    once = pl.Buffered(1)
    in_specs = [pl.BlockSpec((tm, d), row), pl.BlockSpec((tm, d), row),
                pl.BlockSpec((nb, d), lambda i: (0, 2)),
                pl.BlockSpec((nb, d), lambda i: (0, 3)),
                pl.BlockSpec((nb, d), lambda i: (0, 4)),
                pl.BlockSpec((1, d), fixed), pl.BlockSpec((1, d), fixed),
                pl.BlockSpec((d, d), fixed, pipeline_mode=once),
                pl.BlockSpec((n_exp, d), fixed, pipeline_mode=once)]
    out_specs = [pl.BlockSpec((tm, d), row), pl.BlockSpec((tm, d), row),
                 pl.BlockSpec((tm * SUBLANES, LANES), row),
                 pl.BlockSpec((n_exp, tm), lambda i: (0, i))]
    out_shape = [jax.ShapeDtypeStruct((t, d), F32), jax.ShapeDtypeStruct((t, d), BF16),
                 jax.ShapeDtypeStruct((t * SUBLANES, LANES), I32),
                 jax.ShapeDtypeStruct((n_exp, t), F32)]
    return pl.pallas_call(
        functools.partial(_out_kernel, tiles_per_row=t // nb // tm),
        grid=(t // tm,),
        in_specs=in_specs, out_specs=out_specs, out_shape=out_shape,
        scratch_shapes=[pltpu.VMEM((SUBLANES * 2 * tm, LANES), F32)],
        compiler_params=_params(1, 56),
        name="out_proj",
    )(merged, x, mod, mod, mod, g_post, g_pre, w_out_b, w_router_t)


def _router_kernel(logit_ref, br_ref, idx_ref, wsel_ref, rank_ref, cnt_ref):
    per = SUBLANES

    @pl.when(pl.program_id(0) == 0)
    def _():
        cnt_ref[...] = jnp.zeros_like(cnt_ref)

    carry = [cnt_ref[g * per:(g + 1) * per, 0:1] for g in range(N_GROUPS)]
    idx, w, rank, carry = _route(logit_ref[...], br_ref[...], carry)
    idx_ref[...] = idx
    wsel_ref[...] = w
    rank_ref[...] = rank
    for g in range(N_GROUPS):
        cnt_ref[g * per:(g + 1) * per, :] = jnp.broadcast_to(carry[g], (per, LANES))


def _router(logits_t, b_router_col):
    n_exp, t = logits_t.shape
    tr = 1024
    tile = lambda i: (0, i)
    fixed = lambda i: (0, 0)
    return pl.pallas_call(
        _router_kernel,
        grid=(t // tr,),
        in_specs=[pl.BlockSpec((n_exp, tr), tile), pl.BlockSpec((n_exp, 1), fixed)],
        out_specs=[pl.BlockSpec((TOP_K, tr), tile), pl.BlockSpec((TOP_K, tr), tile),
                   pl.BlockSpec((TOP_K, tr), tile), pl.BlockSpec((n_exp, LANES), fixed)],
        out_shape=[jax.ShapeDtypeStruct((TOP_K, t), I32), jax.ShapeDtypeStruct((TOP_K, t), F32),
                   jax.ShapeDtypeStruct((TOP_K, t), I32), jax.ShapeDtypeStruct((n_exp, LANES), F32)],
        compiler_params=_params(1, 32),
        name="router",
    )(logits_t, b_router_col)


def _dest_kernel(pstart_ref, idx_ref, rank_ref, o_ref):
    idx = idx_ref[...]
    acc = rank_ref[...]
    for e in range(pstart_ref.shape[0]):
        acc = acc + jnp.where(idx == e, pstart_ref[e], 0)
    o_ref[...] = acc


def _dest_slots(pstart, idx, rank):
    return pl.pallas_call(
        _dest_kernel,
        in_specs=[pl.BlockSpec(memory_space=pltpu.SMEM),
                  pl.BlockSpec(memory_space=pltpu.VMEM), pl.BlockSpec(memory_space=pltpu.VMEM)],
        out_specs=pl.BlockSpec(memory_space=pltpu.VMEM),
        out_shape=jax.ShapeDtypeStruct(idx.shape, I32),
        name="dest_slots",
    )(pstart, idx, rank)


SLOT_FILL_UNROLL = 2


def _slot_table_kernel(pad_lo_ref, pad_hi_ref, dest_ref, tab_ref):
    i = pl.program_id(0)
    tm = dest_ref.shape[2]

    @pl.when(i == 0)
    def _():
        def clear_range(e, carry):
            def clear(s, c):
                tab_ref[s] = 0
                return c
            return lax.fori_loop(pad_lo_ref[e], pad_hi_ref[e], clear, carry)
        lax.fori_loop(0, pad_lo_ref.shape[0], clear_range, 0)

    def fill(c, carry):
        t0 = c * SLOT_FILL_UNROLL
        slots = [dest_ref[0, k, t0 + u] for u in range(SLOT_FILL_UNROLL) for k in range(TOP_K)]
        vals = [(i * tm + t0 + u) * TOP_K + k for u in range(SLOT_FILL_UNROLL) for k in range(TOP_K)]
        for slot, val in zip(slots, vals):
            tab_ref[slot] = val
        return carry

    lax.fori_loop(0, tm // SLOT_FILL_UNROLL, fill, 0)


def _slot_table(pad_lo, pad_hi, dest3, n_slots):
    nt, _, tm = dest3.shape
    grid_spec = pltpu.PrefetchScalarGridSpec(
        num_scalar_prefetch=2,
        grid=(nt,),
        in_specs=[pl.BlockSpec((1, TOP_K, tm), lambda i, lo, hi: (i, 0, 0), memory_space=pltpu.SMEM)],
        out_specs=pl.BlockSpec(memory_space=pltpu.SMEM),
    )
    return pl.pallas_call(
        _slot_table_kernel,
        grid_spec=grid_spec,
        out_shape=jax.ShapeDtypeStruct((n_slots,), I32),
        compiler_params=_params(1, 16),
        name="slot_table",
    )(pad_lo, pad_hi, dest3)


GATHER_UNROLL = 8
GATHER_BLOCK = 1024


def _gather_kernel(tab_ref, src_ref, o_ref):
    n = tab_ref.shape[2]

    def move(c, carry):
        for u in range(GATHER_UNROLL):
            r = c * GATHER_UNROLL + u
            tok = lax.shift_right_logical(tab_ref[0, 0, r], TOP_K_LOG2)
            src = pl.multiple_of(tok * SUBLANES, SUBLANES)
            dst = pl.multiple_of(r * SUBLANES, SUBLANES)
            o_ref[pl.ds(dst, SUBLANES), :] = src_ref[pl.ds(src, SUBLANES), :]
        return carry

    lax.fori_loop(0, n // GATHER_UNROLL, move, 0)


def _gather_rows(slot_tab, h2p, gb):
    n_slots = slot_tab.shape[0]
    nblk = n_slots // gb
    resident = h2p.size * h2p.dtype.itemsize
    return pl.pallas_call(
        _gather_kernel,
        grid=(nblk,),
        in_specs=[pl.BlockSpec((1, 1, gb), lambda b: (b, 0, 0), memory_space=pltpu.SMEM),
                  pl.BlockSpec(memory_space=pltpu.VMEM)],
        out_specs=pl.BlockSpec((gb * SUBLANES, LANES), lambda b: (b, 0)),
        out_shape=jax.ShapeDtypeStruct((n_slots * SUBLANES, LANES), I32),
        compiler_params=_params(1, resident // MIB + 8),
        name="dispatch_gather",
    )(slot_tab.reshape(nblk, 1, gb), h2p)


ROW_DMA_PRIORITY = 1
ROW_SLOTS = 4


def _expert_kernel(first_ref, nblk_ref, xs_hbm, wg_ref, wu_ref, wd_ref, yb_hbm,
                   xbuf_ref, obuf_ref, in_sem, out_sem, wgb_ref, wub_ref, wdb_ref, stage_in_ref, stage_out_ref,
                   *, bm):
    e = pl.program_id(0)
    nb = nblk_ref[e]
    b0 = first_ref[e]
    rows = bm * SUBLANES

    def x_copy(j, slot, base=None):
        blk = (b0 if base is None else base) + j
        return pltpu.make_async_copy(
            xs_hbm.at[pl.ds(pl.multiple_of(blk * rows, rows), rows), :],
            xbuf_ref.at[pl.ds(pl.multiple_of(slot * rows, rows), rows), :], in_sem.at[slot])

    def y_copy(j, slot):
        return pltpu.make_async_copy(
            obuf_ref.at[pl.ds(pl.multiple_of(slot * rows, rows), rows), :],
            yb_hbm.at[pl.ds(pl.multiple_of((b0 + j) * rows, rows), rows), :], out_sem.at[slot])

    def start_head(count, base=None):
        for j in range(ROW_SLOTS - 1):
            @pl.when(count > j)
            def _(j=j):
                x_copy(j, j, base).start(priority=ROW_DMA_PRIORITY)

    @pl.when(e == 0)
    def _():
        start_head(nb)

    @pl.when(nb > 0)
    def _():
        wgb_ref[...] = wg_ref[...].astype(BF16)
        wub_ref[...] = wu_ref[...].astype(BF16)
        wdb_ref[...] = wd_ref[...].astype(BF16)

    def block(j, carry):
        slot = lax.rem(j, ROW_SLOTS)
        x_copy(j, slot).wait()

        ahead = j + (ROW_SLOTS - 1)

        @pl.when(ahead < nb)
        def _():
            x_copy(ahead, lax.rem(ahead, ROW_SLOTS)).start(priority=ROW_DMA_PRIORITY)

        @pl.when(j >= ROW_SLOTS)
        def _():
            y_copy(j - ROW_SLOTS, slot).wait()

        xbase = slot * rows
        los, his = [], []
        for s in range(SUBLANES):
            lo, hi = _unpack_word(xbuf_ref[pl.ds(xbase + s, bm, stride=SUBLANES), :], stage_in_ref, s)
            los.append(lo.astype(BF16))
            his.append(hi.astype(BF16))
        x = jnp.concatenate(los + his, axis=1)
        g = jnp.dot(x, wgb_ref[...], preferred_element_type=F32)
        u = jnp.dot(x, wub_ref[...], preferred_element_type=F32)
        a = (_silu(g) * u).astype(BF16)
        y = jnp.dot(a, wdb_ref[...], preferred_element_type=F32)
        _pack_rows(y, obuf_ref, stage_out_ref, bm, row0=xbase)
        y_copy(j, slot).start()
        return carry

    lax.fori_loop(0, nb, block, 0)

    @pl.when(e + 1 < pl.num_programs(0))
    def _():
        nxt = jnp.minimum(e + 1, pl.num_programs(0) - 1)
        start_head(nblk_ref[nxt], first_ref[nxt])

    for back in range(ROW_SLOTS, 0, -1):
        @pl.when(nb >= back)
        def _(back=back):
            y_copy(nb - back, lax.rem(nb - back, ROW_SLOTS)).wait()

    @pl.when(e == pl.num_programs(0) - 1)
    def _():
        obuf_ref[pl.ds(0, rows), :] = pltpu.bitcast(jnp.zeros((2 * rows, LANES), BF16), I32)

        def clear(j, carry):
            cp = pltpu.make_async_copy(obuf_ref.at[pl.ds(0, rows), :],
                                       yb_hbm.at[pl.ds(pl.multiple_of(j * rows, rows), rows), :], out_sem.at[0])
            cp.start()
            cp.wait()
            return carry

        lax.fori_loop(b0 + nb, yb_hbm.shape[0] // rows, clear, 0)


def _experts(first_block, n_blocks_e, xs, w_gate, w_up, w_down, bm):
    n_exp, d, de = w_gate.shape
    rows = bm * SUBLANES
    grid_spec = pltpu.PrefetchScalarGridSpec(
        num_scalar_prefetch=2,
        grid=(n_exp,),
        in_specs=[pl.BlockSpec(memory_space=pl.ANY),
                  pl.BlockSpec((None, d, de), lambda e, fb, nb: (e, 0, 0)),
                  pl.BlockSpec((None, d, de), lambda e, fb, nb: (e, 0, 0)),
                  pl.BlockSpec((None, de, d), lambda e, fb, nb: (e, 0, 0))],
        out_specs=pl.BlockSpec(memory_space=pl.ANY),
        scratch_shapes=[pltpu.VMEM((ROW_SLOTS * rows, LANES), I32), pltpu.VMEM((ROW_SLOTS * rows, LANES), I32),
                        pltpu.SemaphoreType.DMA((ROW_SLOTS,)), pltpu.SemaphoreType.DMA((ROW_SLOTS,)),
                        pltpu.VMEM((d, de), BF16), pltpu.VMEM((d, de), BF16), pltpu.VMEM((de, d), BF16),
                        pltpu.VMEM((SUBLANES * 2 * bm, LANES), F32), pltpu.VMEM((SUBLANES * 2 * bm, LANES), F32)],
    )
    return pl.pallas_call(
        functools.partial(_expert_kernel, bm=bm),
        grid_spec=grid_spec,
        out_shape=jax.ShapeDtypeStruct(xs.shape, I32),
        compiler_params=_params(1, 56),
        name="routed_experts",
    )(first_block, n_blocks_e, xs, w_gate, w_up, w_down)


def _final_kernel(dest_ref, dnext_ref, w_ref, yb_hbm, x1_ref, h2_ref, gate_ref, gpost_ref, wsg_ref, wsu_ref,
                  wsd_ref, o_ref, buf_ref, sem, stage_ref, *, tiles_per_row):
    tm = x1_ref.shape[0]
    i = pl.program_id(0)
    last = pl.num_programs(0) - 1
    r = i // tiles_per_row
    tile_rows = TOP_K * tm * SUBLANES
    cur = (i % 2) * tile_rows
    nxt = tile_rows - cur

    def request(dref, base, slot, t, k):
        src = pl.multiple_of(dref[0, k, t] * SUBLANES, SUBLANES)
        dst = pl.multiple_of(base + (k * tm + t) * SUBLANES, SUBLANES)
        pltpu.make_async_copy(yb_hbm.at[pl.ds(src, SUBLANES), :], buf_ref.at[pl.ds(dst, SUBLANES), :],
                              sem.at[slot]).start(priority=k % 2)

    def wait_tile(base, slot):
        pltpu.make_async_copy(yb_hbm.at[pl.ds(0, tile_rows), :],
                              buf_ref.at[pl.ds(pl.multiple_of(base, tile_rows), tile_rows), :], sem.at[slot]).wait()

    @pl.when(i == 0)
    def _():
        def first(t, carry):
            for k in range(TOP_K):
                request(dest_ref, 0, 0, t, k)
            return carry
        lax.fori_loop(0, tm, first, 0)

    wait_tile(cur, i % 2)

    pending = [(t, k) for t in range(tm) for k in range(TOP_K)]
    per_piece = len(pending) // (SUBLANES * TOP_K)

    def request_piece(n):
        for t, k in pending[n * per_piece:(n + 1) * per_piece]:
            request(dnext_ref, nxt, (i + 1) % 2, t, k)

    h = h2_ref[...]
    g = jnp.dot(h, wsg_ref[...], preferred_element_type=F32)
    u = jnp.dot(h, wsu_ref[...], preferred_element_type=F32)
    shared = jnp.dot((_silu(g) * u).astype(BF16), wsd_ref[...], preferred_element_type=F32)

    w = w_ref[...]
    los, his = [], []
    for s in range(SUBLANES):
        lo_acc = jnp.zeros((tm, LANES), F32)
        hi_acc = jnp.zeros((tm, LANES), F32)
        for k in range(TOP_K):
            lo, hi = _unpack_word(buf_ref[pl.ds(cur + k * tm * SUBLANES + s, tm, stride=SUBLANES), :],
                                  stage_ref, s * TOP_K + k)
            wk = w[:, k:k + 1]
            lo_acc = lo_acc + wk * lo
            hi_acc = hi_acc + wk * hi
            request_piece(s * TOP_K + k)
        los.append(lo_acc)
        his.append(hi_acc)
    ffn = jnp.concatenate(los + his, axis=1) + shared
    o_ref[...] = x1_ref[...] + gate_ref[pl.ds(r, 1), :] * _rms(ffn, gpost_ref[...])

    @pl.when(i == last)
    def _():
        wait_tile(nxt, (i + 1) % 2)


def _final(dest, w_tok, yb, x1, h2b, mod, g_post, ws_gate, ws_up, ws_down):
    t, d = x1.shape
    nb = mod.shape[0]
    ds_ = ws_gate.shape[1]
    tm = dest.shape[2]
    n_tiles = t // tm
    row = lambda i: (i, 0)
    fixed = lambda i: (0, 0)
    return pl.pallas_call(
        functools.partial(_final_kernel, tiles_per_row=t // nb // tm),
        grid=(n_tiles,),
        in_specs=[pl.BlockSpec((1, TOP_K, tm), lambda i: (i, 0, 0), memory_space=pltpu.SMEM),
                  pl.BlockSpec((1, TOP_K, tm), lambda i: (jnp.minimum(i + 1, n_tiles - 1), 0, 0),
                               memory_space=pltpu.SMEM),
                  pl.BlockSpec((tm, TOP_K), row),
                  pl.BlockSpec(memory_space=pl.ANY),
                  pl.BlockSpec((tm, d), row), pl.BlockSpec((tm, d), row),
                  pl.BlockSpec((nb, d), lambda i: (0, 5)),
                  pl.BlockSpec((1, d), fixed),
                  pl.BlockSpec((d, ds_), fixed), pl.BlockSpec((d, ds_), fixed), pl.BlockSpec((ds_, d), fixed)],
        out_specs=pl.BlockSpec((tm, d), row),
        out_shape=jax.ShapeDtypeStruct((t, d), F32),
        scratch_shapes=[pltpu.VMEM((2 * TOP_K * tm * SUBLANES, LANES), I32), pltpu.SemaphoreType.DMA((2,)),
                        pltpu.VMEM((SUBLANES * TOP_K * 2 * tm, LANES), F32)],
        compiler_params=_params(1, 48),
        name="combine_final",
    )(dest, dest, w_tok, yb, x1, h2b, mod, g_post, ws_gate, ws_up, ws_down)


def _rope_tables(n_pos):
    half = LANES // 2
    pos = jnp.arange(n_pos)
    row = (pos // GRID_W).astype(F32)
    col = (pos % GRID_W).astype(F32)
    inv = ROPE_THETA ** (-jnp.arange(0, half, 2, dtype=F32) / half)
    ang = jnp.concatenate([row[:, None] * inv, col[:, None] * inv], axis=-1)
    cos = jnp.repeat(jnp.cos(ang), 2, axis=-1)
    sin = jnp.repeat(jnp.sin(ang), 2, axis=-1) * jnp.tile(jnp.array([-1.0, 1.0], F32), half)
    return cos, sin


def _sublayer1(x, mod, p, seq_len, rope, cache, n_heads, n_kv, tq):
    t = x.shape[0]
    h = _prenorm(x, mod, p['g_pre1'], 0, 1)
    z = _matmul_wcast(h, p['w_in'], _z_col_map, 9 * 1024, BF16)
    kv = _matmul_wcast(h, p['w_in'], lambda j: j + 5, 1024, F32)
    qn, kn, vb, k32, v32 = _qk_prep(z, kv, p['g_q'], p['g_k'], rope, seq_len, n_heads, n_kv)
    attn = _attention(qn, kn, vb, cache, t // seq_len, seq_len, n_heads, n_kv, tq)
    cvg = _gated_conv(z, p['conv_w'], p['conv_b'], seq_len, 6)
    merged = _merge(cvg, attn, z, p['w_conv_out'], p['w_attn_out'], 4, 8)
    x1, h2b, h2p, logits_t = _out_proj(merged, x, mod, p['g_post1'], p['g_pre2'], p['w_out_b'], p['w_router_t'])
    return x1, h2b, h2p, logits_t, k32, v32


def kernel(x_prompt, x_sample, cache_k, cache_v, c, c_ctx, w_mod, b_mod, g_pre1, w_in, conv_w, conv_b, g_q, g_k, w_conv_out, w_attn_out, w_out, g_post1, g_pre2, w_router, b_router, w_e_gate, w_e_up, w_e_down, w_s_gate, w_s_up, w_s_down, g_post2):
    batch, seq, d = x_prompt.shape
    dec_batch, dec_seq, _ = x_sample.shape
    depth = w_mod.shape[0]
    assert depth == 1
    past, n_kv, head_dim = cache_k.shape[2:]
    assert head_dim == LANES
    n_heads = w_attn_out.shape[1] // head_dim
    n_exp = w_router.shape[2]
    t_ctx, t_lat = batch * seq, dec_batch * dec_seq
    t_all = t_ctx + t_lat
    l = 0

    p = {
        'g_pre1': g_pre1[l][None], 'w_in': w_in[l], 'conv_w': conv_w[l], 'conv_b': conv_b[l][None],
        'g_q': g_q[l][None], 'g_k': g_k[l][None], 'w_conv_out': w_conv_out[l], 'w_attn_out': w_attn_out[l],
        'w_out_b': w_out[l].astype(BF16), 'g_post1': g_post1[l][None], 'g_pre2': g_pre2[l][None],
        'w_router_t': w_router[l].T,
    }
    cond = jnp.concatenate([c_ctx[None], c, jnp.zeros((SUBLANES - 1 - dec_batch, d), F32)], axis=0)
    mod = _modulation(cond, w_mod[l], b_mod[l][None])
    mod_ctx, mod_lat = mod[0:1], mod[1:1 + dec_batch]

    xc = x_prompt.reshape(t_ctx, d)
    xl = x_sample.reshape(t_lat, d)
    cache = (cache_k[:, l].reshape(dec_batch, past, n_kv * head_dim),
             cache_v[:, l].reshape(dec_batch, past, n_kv * head_dim))
    rope = _rope_tables(dec_seq)

    x1c, h2bc, h2pc, logc, k32, v32 = _sublayer1(xc, mod_ctx, p, seq, None, None, n_heads, n_kv, 256)
    x1l, h2bl, h2pl, logl, _, _ = _sublayer1(xl, mod_lat, p, dec_seq, rope, cache, n_heads, n_kv, 256)
    idx_all, w_all, rank_all, cnt = _router(jnp.concatenate([logc, logl], axis=1), b_router[l][:, None])

    bm = 256
    n_blocks = (t_all * TOP_K + n_exp * (bm - 1)) // bm + 1
    n_slots = n_blocks * bm
    blocks_e = jnp.floor((cnt[:, 0] + (bm - 1)) / bm)
    first_e = jnp.cumsum(blocks_e) - blocks_e
    pstart = (first_e * bm).astype(I32)

    tmf = 128
    w_tok = w_all.T
    dest = _dest_slots(pstart, idx_all, rank_all)
    dest3 = dest.reshape(TOP_K, t_all // tmf, tmf).transpose(1, 0, 2)
    used_end = jnp.sum(blocks_e, keepdims=True) * bm
    pad_lo = jnp.concatenate([first_e * bm + cnt[:, 0], used_end]).astype(I32)
    pad_hi = jnp.concatenate([(first_e + blocks_e) * bm, jnp.full((1,), float(n_slots), F32)]).astype(I32)
    slot_tab = _slot_table(pad_lo, pad_hi, dest3, n_slots)

    h2p = jnp.concatenate([h2pc, h2pl], axis=0)
    xs = _gather_rows(slot_tab, h2p, GATHER_BLOCK)
    yb = _experts(first_e.astype(I32), blocks_e.astype(I32), xs, w_e_gate[l], w_e_up[l], w_e_down[l], bm)

    ws = (w_s_gate[l].astype(BF16), w_s_up[l].astype(BF16), w_s_down[l].astype(BF16))

    def finish(lo, n, x1, h2b, mod_g):
        dest_g = dest3[lo // tmf:(lo + n) // tmf]
        return _final(dest_g, w_tok[lo:lo + n], yb, x1, h2b, mod_g, g_post2[l][None], *ws)

    y_ctx = finish(0, t_ctx, x1c, h2bc, mod_ctx)
    y_lat = finish(t_ctx, t_lat, x1l, h2bl, mod_lat)

    new_k = k32.reshape(batch, 1, seq, n_kv, head_dim)
    new_v = v32.reshape(batch, 1, seq, n_kv, head_dim)
    return (y_ctx.reshape(batch, seq, d), y_lat.reshape(dec_batch, dec_seq, d), new_k, new_v)
```

```python
import functools

import jax
import jax.numpy as jnp
from jax import lax
from jax.experimental import pallas as pl
from jax.experimental.pallas import tpu as pltpu

GRID_W = 64
ROPE_THETA = 10000.0
N_GROUPS = 8
TOPK_GROUPS = 4
TOP_K = 8
TOP_K_LOG2 = 3
ROUTED_SCALE = 2.5
EPS = 1e-6

LANES = 128
SUBLANES = 8
V7X_VMEM_BYTES = 64 * 1024 * 1024
MIB = 1024 * 1024

F32 = jnp.float32
BF16 = jnp.bfloat16
I32 = jnp.int32


def _params(n_grid, vmem_mib):
    assert vmem_mib * MIB < V7X_VMEM_BYTES
    return pltpu.CompilerParams(
        dimension_semantics=("arbitrary",) * n_grid, vmem_limit_bytes=vmem_mib * MIB)


def _silu(x):
    return x * jax.nn.sigmoid(x)


def _rms(x, g):
    return x * lax.rsqrt(jnp.mean(x * x, axis=-1, keepdims=True) + EPS) * g


def _pack_rows(val, out_ref, stage_ref, rows, row0=0, stage0=0):
    half = val.shape[1] // 2
    for s in range(half // LANES):
        base = stage0 + s * 2 * rows
        stage_ref[pl.ds(base, rows, stride=2), :] = val[:, s * LANES:(s + 1) * LANES]
        stage_ref[pl.ds(base + 1, rows, stride=2), :] = val[:, half + s * LANES:half + (s + 1) * LANES]
        pair = stage_ref[pl.ds(base, 2 * rows), :].astype(BF16)
        out_ref[pl.ds(row0 + s, rows, stride=SUBLANES), :] = pltpu.bitcast(pair, I32)


def _unpack_word(word, stage_ref, slot):
    rows = word.shape[0]
    base = slot * 2 * rows
    stage_ref[pl.ds(base, 2 * rows), :] = pltpu.bitcast(word, BF16).astype(F32)
    return stage_ref[pl.ds(base, rows, stride=2), :], stage_ref[pl.ds(base + 1, rows, stride=2), :]


def _mod_kernel(c_ref, w_ref, b_ref, o_ref):
    s = _silu(c_ref[...]).astype(BF16)
    o_ref[...] = jnp.dot(s, w_ref[...].astype(BF16), preferred_element_type=F32) + b_ref[...]


def _modulation(cond, w, b):
    rows, d = cond.shape
    n = w.shape[1]
    tn = 1024
    return pl.pallas_call(
        _mod_kernel,
        grid=(n // tn,),
        in_specs=[pl.BlockSpec((rows, d), lambda j: (0, 0)),
                  pl.BlockSpec((d, tn), lambda j: (0, j)),
                  pl.BlockSpec((1, tn), lambda j: (0, j))],
        out_specs=pl.BlockSpec((rows, tn), lambda j: (0, j)),
        out_shape=jax.ShapeDtypeStruct((rows, n), F32),
        compiler_params=_params(1, 40),
        name="modulation",
    )(cond, w, b)


def _prenorm_kernel(x_ref, shift_ref, scale_ref, g_ref, o_ref, *, tiles_per_row):
    r = pl.program_id(0) // tiles_per_row
    y = _rms(x_ref[...], g_ref[...])
    o_ref[...] = (y * (1.0 + scale_ref[pl.ds(r, 1), :]) + shift_ref[pl.ds(r, 1), :]).astype(o_ref.dtype)


def _prenorm(x, mod, g, shift_col, scale_col):
    t, d = x.shape
    nb = mod.shape[0]
    tm = 512
    return pl.pallas_call(
        functools.partial(_prenorm_kernel, tiles_per_row=t // nb // tm),
        grid=(t // tm,),
        in_specs=[pl.BlockSpec((tm, d), lambda i: (i, 0)),
                  pl.BlockSpec((nb, d), lambda i: (0, shift_col)),
                  pl.BlockSpec((nb, d), lambda i: (0, scale_col)),
                  pl.BlockSpec((1, d), lambda i: (0, 0))],
        out_specs=pl.BlockSpec((tm, d), lambda i: (i, 0)),
        out_shape=jax.ShapeDtypeStruct((t, d), BF16),
        compiler_params=_params(1, 32),
        name="prenorm",
    )(x, mod, mod, g)


def _mm_wcast_kernel(a_ref, w_ref, o_ref, wb_ref):
    @pl.when(pl.program_id(1) == 0)
    def _():
        wb_ref[...] = w_ref[...].astype(BF16)

    o_ref[...] = jnp.dot(a_ref[...], wb_ref[...], preferred_element_type=F32).astype(o_ref.dtype)


def _matmul_wcast(a, w, col_map, n_out, out_dtype):
    m, k = a.shape
    tm, tn = 1024, 1024
    return pl.pallas_call(
        _mm_wcast_kernel,
        grid=(n_out // tn, m // tm),
        in_specs=[pl.BlockSpec((tm, k), lambda j, i: (i, 0)),
                  pl.BlockSpec((k, tn), lambda j, i: (0, col_map(j)))],
        out_specs=pl.BlockSpec((tm, tn), lambda j, i: (i, j)),
        out_shape=jax.ShapeDtypeStruct((m, n_out), out_dtype),
        scratch_shapes=[pltpu.VMEM((k, tn), BF16)],
        compiler_params=_params(2, 48),
        name="in_proj",
    )(a, w)


def _z_col_map(j):
    return jnp.where(j < 2, j + 3, jnp.where(j < 6, j + 4, j - 6))


def _qk_kernel(*refs, use_rope, n_heads, n_kv, q_scale):
    if use_rope:
        q_ref, kv_ref, gq_ref, gk_ref, cos_ref, sin_ref, qn_ref, kn_ref, vb_ref, k32_ref, v32_ref = refs
        cos = cos_ref[...]
        sin = sin_ref[...]
        even = lax.broadcasted_iota(I32, cos.shape, 1) % 2 == 0
    else:
        q_ref, kv_ref, gq_ref, gk_ref, qn_ref, kn_ref, vb_ref, k32_ref, v32_ref = refs

    def norm_rope(xh, g):
        y = _rms(xh, g)
        if use_rope:
            sw = jnp.where(even, pltpu.roll(y, LANES - 1, 1), pltpu.roll(y, 1, 1))
            y = y * cos + sw * sin
        return y

    gq = gq_ref[...]
    gk = gk_ref[...]
    for h in range(n_heads):
        sl = slice(h * LANES, (h + 1) * LANES)
        qn_ref[:, sl] = (norm_rope(q_ref[:, sl].astype(F32), gq) * q_scale).astype(BF16)
    kw = n_kv * LANES
    for h in range(n_kv):
        sl = slice(h * LANES, (h + 1) * LANES)
        kh = norm_rope(kv_ref[:, sl], gk)
        k32_ref[:, sl] = kh
        kn_ref[:, sl] = kh.astype(BF16)
    v = kv_ref[:, kw:2 * kw]
    v32_ref[...] = v
    vb_ref[...] = v.astype(BF16)


def _qk_prep(z, kv, gq, gk, rope, seq_len, n_heads, n_kv):
    t = z.shape[0]
    dq = n_heads * LANES
    dk = n_kv * LANES
    tm = 256
    in_specs = [pl.BlockSpec((tm, dq), lambda i: (i, 0)),
                pl.BlockSpec((tm, 2 * dk), lambda i: (i, 0)),
                pl.BlockSpec((1, LANES), lambda i: (0, 0)),
                pl.BlockSpec((1, LANES), lambda i: (0, 0))]
    args = [z, kv, gq, gk]
    if rope is not None:
        per_seq = seq_len // tm
        in_specs += [pl.BlockSpec((tm, LANES), lambda i: (i % per_seq, 0))] * 2
        args += list(rope)
    out_specs = [pl.BlockSpec((tm, dq), lambda i: (i, 0))] + [pl.BlockSpec((tm, dk), lambda i: (i, 0))] * 4
    out_shape = [jax.ShapeDtypeStruct((t, dq), BF16), jax.ShapeDtypeStruct((t, dk), BF16),
                 jax.ShapeDtypeStruct((t, dk), BF16), jax.ShapeDtypeStruct((t, dk), F32),
                 jax.ShapeDtypeStruct((t, dk), F32)]
    return pl.pallas_call(
        functools.partial(_qk_kernel, use_rope=rope is not None, n_heads=n_heads, n_kv=n_kv,
                          q_scale=LANES ** -0.5),
        grid=(t // tm,),
        in_specs=in_specs, out_specs=out_specs, out_shape=out_shape,
        compiler_params=_params(1, 32),
        name="qk_prep",
    )(*args)


def _attn_kernel(*refs, has_cache, group, chunks):
    if has_cache:
        q_ref, k_ref, v_ref, ck_ref, cv_ref, o_ref = refs
    else:
        q_ref, k_ref, v_ref, o_ref = refs
    tq = q_ref.shape[0]
    nt = (((1,), (1,)), ((), ()))
    if has_cache:
        ck = ck_ref[...].astype(BF16)
        cv = cv_ref[...].astype(BF16)
    per = group // chunks
    for c in range(chunks):
        heads = range(c * per, (c + 1) * per)
        q = jnp.concatenate([q_ref[:, g * LANES:(g + 1) * LANES] for g in heads], axis=0)
        s_own = lax.dot_general(q, k_ref[...], nt, preferred_element_type=F32)
        m = jnp.max(s_own, axis=-1, keepdims=True)
        if has_cache:
            s_ctx = lax.dot_general(q, ck, nt, preferred_element_type=F32)
            m = jnp.maximum(m, jnp.max(s_ctx, axis=-1, keepdims=True))
        p = jnp.exp(s_own - m)
        denom = jnp.sum(p, axis=-1, keepdims=True)
        acc = jnp.dot(p.astype(BF16), v_ref[...], preferred_element_type=F32)
        if has_cache:
            pc = jnp.exp(s_ctx - m)
            denom = denom + jnp.sum(pc, axis=-1, keepdims=True)
            acc = acc + jnp.dot(pc.astype(BF16), cv, preferred_element_type=F32)
        o = acc / denom
        for n, g in enumerate(heads):
            o_ref[:, g * LANES:(g + 1) * LANES] = o[n * tq:(n + 1) * tq].astype(o_ref.dtype)


def _attention(qn, kn, vb, cache, batch, seq_len, n_heads, n_kv, tq):
    t = qn.shape[0]
    group = n_heads // n_kv
    nq = seq_len // tq
    in_specs = [pl.BlockSpec((tq, group * LANES), lambda b, h, i: (b * nq + i, h)),
                pl.BlockSpec((seq_len, LANES), lambda b, h, i: (b, h)),
                pl.BlockSpec((seq_len, LANES), lambda b, h, i: (b, h))]
    args = [qn, kn, vb]
    if cache is not None:
        past = cache[0].shape[1]
        in_specs += [pl.BlockSpec((None, past, LANES), lambda b, h, i: (b, 0, h))] * 2
        args += list(cache)
    return pl.pallas_call(
        functools.partial(_attn_kernel, has_cache=cache is not None, group=group,
                          chunks=group if cache is not None else group // 2),
        grid=(batch, n_kv, nq),
        in_specs=in_specs,
        out_specs=pl.BlockSpec((tq, group * LANES), lambda b, h, i: (b * nq + i, h)),
        out_shape=jax.ShapeDtypeStruct((t, n_heads * LANES), BF16),
        compiler_params=_params(3, 40),
        name="attention",
    )(*args)


CONV_HALO = 16


def _conv_kernel(u_ref, b_ref, c_ref, up_ref, cp_ref, un_ref, cn_ref, w_ref, bias_ref, o_ref, *,
                 tiles_per_seq):
    tm = u_ref.shape[0]
    pos = pl.program_id(0) % tiles_per_seq
    cu = c_ref[...].astype(F32) * u_ref[...].astype(F32)
    halo_prev = (cp_ref[...].astype(F32) * up_ref[...].astype(F32))[CONV_HALO - 1:CONV_HALO, :]
    halo_next = (cn_ref[...].astype(F32) * un_ref[...].astype(F32))[0:1, :]
    halo_prev = jnp.where(pos == 0, 0.0, halo_prev)
    halo_next = jnp.where(pos == tiles_per_seq - 1, 0.0, halo_next)
    row = lax.broadcasted_iota(I32, cu.shape, 0)
    prev = jnp.where(row == 0, halo_prev, pltpu.roll(cu, 1, 0))
    nxt = jnp.where(row == tm - 1, halo_next, pltpu.roll(cu, tm - 1, 0))
    w = w_ref[...]
    conv = prev * w[0:1, :] + cu * w[1:2, :] + nxt * w[2:3, :] + bias_ref[...]
    o_ref[...] = (b_ref[...].astype(F32) * conv).astype(o_ref.dtype)


def _gated_conv(z, conv_w, conv_b, seq_len, col0):
    t = z.shape[0]
    dc = conv_w.shape[1]
    tm = 256
    hb = tm // CONV_HALO
    last = t // CONV_HALO - 1
    prev_map = lambda c: (lambda i: (jnp.maximum(i * hb - 1, 0), c))
    next_map = lambda c: (lambda i: (jnp.minimum((i + 1) * hb, last), c))
    in_specs = [pl.BlockSpec((tm, dc), lambda i: (i, col0)),
                pl.BlockSpec((tm, dc), lambda i: (i, col0 + 1)),
                pl.BlockSpec((tm, dc), lambda i: (i, col0 + 2)),
                pl.BlockSpec((CONV_HALO, dc), prev_map(col0)),
                pl.BlockSpec((CONV_HALO, dc), prev_map(col0 + 2)),
                pl.BlockSpec((CONV_HALO, dc), next_map(col0)),
                pl.BlockSpec((CONV_HALO, dc), next_map(col0 + 2)),
                pl.BlockSpec(conv_w.shape, lambda i: (0, 0)),
                pl.BlockSpec((1, dc), lambda i: (0, 0))]
    return pl.pallas_call(
        functools.partial(_conv_kernel, tiles_per_seq=seq_len // tm),
        grid=(t // tm,),
        in_specs=in_specs,
        out_specs=pl.BlockSpec((tm, dc), lambda i: (i, 0)),
        out_shape=jax.ShapeDtypeStruct((t, dc), BF16),
        compiler_params=_params(1, 32),
        name="gated_conv",
    )(z, z, z, z, z, z, z, conv_w, conv_b)


def _merge_kernel(cv_ref, at_ref, gc_ref, ga_ref, wc_ref, wa_ref, o_ref, wcb_ref, wab_ref):
    @pl.when(pl.program_id(1) == 0)
    def _():
        wcb_ref[...] = wc_ref[...].astype(BF16)
        wab_ref[...] = wa_ref[...].astype(BF16)

    conv_out = jnp.dot(cv_ref[...], wcb_ref[...], preferred_element_type=F32)
    attn_out = jnp.dot(at_ref[...], wab_ref[...], preferred_element_type=F32)
    merged = (jax.nn.sigmoid(gc_ref[...].astype(F32)) * conv_out
              + jax.nn.sigmoid(ga_ref[...].astype(F32)) * attn_out)
    o_ref[...] = merged.astype(o_ref.dtype)


def _merge(cvg, attn, z, w_conv_out, w_attn_out, gc_col0, ga_col0):
    t, dc = cvg.shape
    dq = attn.shape[1]
    d = w_conv_out.shape[1]
    tm, tn = 512, 512
    return pl.pallas_call(
        _merge_kernel,
        grid=(d // tn, t // tm),
        in_specs=[pl.BlockSpec((tm, dc), lambda j, i: (i, 0)),
                  pl.BlockSpec((tm, dq), lambda j, i: (i, 0)),
                  pl.BlockSpec((tm, tn), lambda j, i: (i, gc_col0 + j)),
                  pl.BlockSpec((tm, tn), lambda j, i: (i, ga_col0 + j)),
                  pl.BlockSpec((dc, tn), lambda j, i: (0, j)),
                  pl.BlockSpec((dq, tn), lambda j, i: (0, j))],
        out_specs=pl.BlockSpec((tm, tn), lambda j, i: (i, j)),
        out_shape=jax.ShapeDtypeStruct((t, d), BF16),
        scratch_shapes=[pltpu.VMEM((dc, tn), BF16), pltpu.VMEM((dq, tn), BF16)],
        compiler_params=_params(2, 40),
        name="merge",
    )(cvg, attn, z, z, w_conv_out, w_attn_out)


def _route(logits_t, bias_col, carry):
    n_exp, tm = logits_t.shape
    per = n_exp // N_GROUPS
    assert per == SUBLANES
    neg = -jnp.inf
    scores = jax.nn.sigmoid(logits_t)
    biased = scores + bias_col
    sub = lax.broadcasted_iota(I32, (per, tm), 0).astype(F32)
    xs = [biased[g * per:(g + 1) * per, :] for g in range(N_GROUPS)]
    sc = [scores[g * per:(g + 1) * per, :] for g in range(N_GROUPS)]
    ids = [sub + float(g * per) for g in range(N_GROUPS)]

    def colmax(a):
        return jnp.max(a, axis=0, keepdims=True)

    def colmin(a):
        return jnp.min(a, axis=0, keepdims=True)

    rows = []
    for g in range(N_GROUPS):
        m1 = colmax(xs[g])
        j1 = colmin(jnp.where(xs[g] == m1, sub, float(per)))
        m2 = colmax(jnp.where(sub == j1, neg, xs[g]))
        rows.append(m1 + m2)
    gs = jnp.concatenate(rows, axis=0)
    gsel = jnp.zeros_like(gs)
    for _ in range(TOPK_GROUPS):
        m = colmax(gs)
        j = colmin(jnp.where(gs == m, sub, float(N_GROUPS)))
        hit = sub == j
        gsel = jnp.where(hit, 1.0, gsel)
        gs = jnp.where(hit, neg, gs)
    masked = [jnp.where(gsel[g:g + 1, :] > 0.0, xs[g], neg) for g in range(N_GROUPS)]
    idx_rows, w_rows = [], []
    member = [jnp.zeros((per, tm), F32) for _ in range(N_GROUPS)]
    for _ in range(TOP_K):
        mm = masked[0]
        for g in range(1, N_GROUPS):
            mm = jnp.maximum(mm, masked[g])
        m = colmax(mm)
        idx = colmin(jnp.where(masked[0] == m, ids[0], float(n_exp)))
        for g in range(1, N_GROUPS):
            idx = jnp.minimum(idx, colmin(jnp.where(masked[g] == m, ids[g], float(n_exp))))
        wk = jnp.zeros_like(idx)
        for g in range(N_GROUPS):
            hit = ids[g] == idx
            wk = wk + jnp.sum(jnp.where(hit, sc[g], 0.0), axis=0, keepdims=True)
            masked[g] = jnp.where(hit, neg, masked[g])
            member[g] = jnp.where(hit, 1.0, member[g])
        idx_rows.append(idx)
        w_rows.append(wk)
    w = jnp.concatenate(w_rows, axis=0)
    w = w / jnp.sum(w, axis=0, keepdims=True) * ROUTED_SCALE

    earlier = (lax.broadcasted_iota(I32, (tm, tm), 0) < lax.broadcasted_iota(I32, (tm, tm), 1)).astype(BF16)
    before = jnp.dot(jnp.concatenate(member, axis=0).astype(BF16), earlier, preferred_element_type=F32)
    rank_rows = []
    for k in range(TOP_K):
        rk = jnp.zeros_like(idx_rows[k])
        for g in range(N_GROUPS):
            pos = before[g * per:(g + 1) * per, :] + carry[g]
            rk = rk + jnp.sum(jnp.where(ids[g] == idx_rows[k], pos, 0.0), axis=0, keepdims=True)
        rank_rows.append(rk)
    new_carry = [carry[g] + jnp.sum(member[g], axis=1, keepdims=True) for g in range(N_GROUPS)]
    idx = jnp.concatenate(idx_rows, axis=0).astype(I32)
    rank = jnp.concatenate(rank_rows, axis=0).astype(I32)
    return idx, w, rank, new_carry


def _out_kernel(mg_ref, x_ref, gate_ref, shift_ref, scale_ref, gpost_ref, gpre_ref, wo_ref, wr_ref,
                x1_ref, h2b_ref, h2p_ref, logit_ref, stage_ref, *, tiles_per_row):
    tm = x_ref.shape[0]
    r = pl.program_id(0) // tiles_per_row
    gate = gate_ref[pl.ds(r, 1), :]
    scale = 1.0 + scale_ref[pl.ds(r, 1), :]
    shift = shift_ref[pl.ds(r, 1), :]
    rows = tm // OUT_CHUNKS
    for c in range(OUT_CHUNKS):
        sl = pl.ds(c * rows, rows)
        mix = jnp.dot(mg_ref[sl, :], wo_ref[...], preferred_element_type=F32)
        x1 = x_ref[sl, :] + gate * _rms(mix, gpost_ref[...])
        x1_ref[sl, :] = x1
        h2 = _rms(x1, gpre_ref[...]) * scale + shift
        h2b_ref[sl, :] = h2.astype(BF16)
        _pack_rows(h2, h2p_ref, stage_ref, rows, row0=c * rows * SUBLANES, stage0=c * rows * 2 * SUBLANES)
        logit_ref[:, sl] = lax.dot_general(wr_ref[...], h2, (((1,), (1,)), ((), ())),
                                           preferred_element_type=F32, precision=lax.Precision.HIGHEST)


OUT_CHUNKS = 4


def _out_proj(merged, x, mod, g_post, g_pre, w_out_b, w_router_t):
    t, d = x.shape
    nb = mod.shape[0]
    n_exp = w_router_t.shape[0]
    tm = 512
    row = lambda i: (i, 0)
    fixed = lambda i: (0, 0)
    once = pl.Buffered(1)
    in_specs = [pl.BlockSpec((tm, d), row), pl.BlockSpec((tm, d), row),
                pl.BlockSpec((nb, d), lambda i: (0, 2)),
                pl.BlockSpec((nb, d), lambda i: (0, 3)),
                pl.BlockSpec((nb, d), lambda i: (0, 4)),
                pl.BlockSpec((1, d), fixed), pl.BlockSpec((1, d), fixed),
                pl.BlockSpec((d, d), fixed, pipeline_mode=once),
                pl.BlockSpec((n_exp, d), fixed, pipeline_mode=once)]
    out_specs = [pl.BlockSpec((tm, d), row), pl.BlockSpec((tm, d), row),
                 pl.BlockSpec((tm * SUBLANES, LANES), row),
                 pl.BlockSpec((n_exp, tm), lambda i: (0, i))]
    out_shape = [jax.ShapeDtypeStruct((t, d), F32), jax.ShapeDtypeStruct((t, d), BF16),
                 jax.ShapeDtypeStruct((t * SUBLANES, LANES), I32),
                 jax.ShapeDtypeStruct((n_exp, t), F32)]
    return pl.pallas_call(
        functools.partial(_out_kernel, tiles_per_row=t // nb // tm),
        grid=(t // tm,),
        in_specs=in_specs, out_specs=out_specs, out_shape=out_shape,
        scratch_shapes=[pltpu.VMEM((SUBLANES * 2 * tm, LANES), F32)],
        compiler_params=_params(1, 56),
        name="out_proj",
    )(merged, x, mod, mod, mod, g_post, g_pre, w_out_b, w_router_t)


def _router_kernel(logit_ref, br_ref, idx_ref, wsel_ref, rank_ref, cnt_ref):
    per = SUBLANES

    @pl.when(pl.program_id(0) == 0)
    def _():
        cnt_ref[...] = jnp.zeros_like(cnt_ref)

    carry = [cnt_ref[g * per:(g + 1) * per, 0:1] for g in range(N_GROUPS)]
    idx, w, rank, carry = _route(logit_ref[...], br_ref[...], carry)
    idx_ref[...] = idx
    wsel_ref[...] = w
    rank_ref[...] = rank
    for g in range(N_GROUPS):
        cnt_ref[g * per:(g + 1) * per, :] = jnp.broadcast_to(carry[g], (per, LANES))


def _router(logits_t, b_router_col):
    n_exp, t = logits_t.shape
    tr = 1024
    tile = lambda i: (0, i)
    fixed = lambda i: (0, 0)
    return pl.pallas_call(
        _router_kernel,
        grid=(t // tr,),
        in_specs=[pl.BlockSpec((n_exp, tr), tile), pl.BlockSpec((n_exp, 1), fixed)],
        out_specs=[pl.BlockSpec((TOP_K, tr), tile), pl.BlockSpec((TOP_K, tr), tile),
                   pl.BlockSpec((TOP_K, tr), tile), pl.BlockSpec((n_exp, LANES), fixed)],
        out_shape=[jax.ShapeDtypeStruct((TOP_K, t), I32), jax.ShapeDtypeStruct((TOP_K, t), F32),
                   jax.ShapeDtypeStruct((TOP_K, t), I32), jax.ShapeDtypeStruct((n_exp, LANES), F32)],
        compiler_params=_params(1, 32),
        name="router",
    )(logits_t, b_router_col)


def _dest_kernel(pstart_ref, idx_ref, rank_ref, o_ref):
    idx = idx_ref[...]
    acc = rank_ref[...]
    for e in range(pstart_ref.shape[0]):
        acc = acc + jnp.where(idx == e, pstart_ref[e], 0)
    o_ref[...] = acc


def _dest_slots(pstart, idx, rank):
    return pl.pallas_call(
        _dest_kernel,
        in_specs=[pl.BlockSpec(memory_space=pltpu.SMEM),
                  pl.BlockSpec(memory_space=pltpu.VMEM), pl.BlockSpec(memory_space=pltpu.VMEM)],
        out_specs=pl.BlockSpec(memory_space=pltpu.VMEM),
        out_shape=jax.ShapeDtypeStruct(idx.shape, I32),
        name="dest_slots",
    )(pstart, idx, rank)


SLOT_FILL_UNROLL = 8


def _slot_table_kernel(dest_ref, tab_ref):
    i = pl.program_id(0)
    tm = dest_ref.shape[2]

    @pl.when(i == 0)
    def _():
        def clear(c, carry):
            for u in range(SLOT_FILL_UNROLL):
                tab_ref[c * SLOT_FILL_UNROLL + u] = 0
            return carry
        lax.fori_loop(0, tab_ref.shape[0] // SLOT_FILL_UNROLL, clear, 0)

    def fill(t, carry):
        for k in range(TOP_K):
            tab_ref[dest_ref[0, k, t]] = (i * tm + t) * TOP_K + k
        return carry

    lax.fori_loop(0, tm, fill, 0)


def _slot_table(dest3, n_slots):
    nt, _, tm = dest3.shape
    return pl.pallas_call(
        _slot_table_kernel,
        grid=(nt,),
        in_specs=[pl.BlockSpec((1, TOP_K, tm), lambda i: (i, 0, 0), memory_space=pltpu.SMEM)],
        out_specs=pl.BlockSpec(memory_space=pltpu.SMEM),
        out_shape=jax.ShapeDtypeStruct((n_slots,), I32),
        compiler_params=_params(1, 16),
        name="slot_table",
    )(dest3)


GATHER_UNROLL = 8
GATHER_BLOCK = 1024


def _gather_kernel(tab_ref, src_ref, o_ref):
    n = tab_ref.shape[2]

    def move(c, carry):
        for u in range(GATHER_UNROLL):
            r = c * GATHER_UNROLL + u
            tok = lax.shift_right_logical(tab_ref[0, 0, r], TOP_K_LOG2)
            src = pl.multiple_of(tok * SUBLANES, SUBLANES)
            dst = pl.multiple_of(r * SUBLANES, SUBLANES)
            o_ref[pl.ds(dst, SUBLANES), :] = src_ref[pl.ds(src, SUBLANES), :]
        return carry

    lax.fori_loop(0, n // GATHER_UNROLL, move, 0)


def _gather_rows(slot_tab, h2p, gb):
    n_slots = slot_tab.shape[0]
    nblk = n_slots // gb
    resident = h2p.size * h2p.dtype.itemsize
    return pl.pallas_call(
        _gather_kernel,
        grid=(nblk,),
        in_specs=[pl.BlockSpec((1, 1, gb), lambda b: (b, 0, 0), memory_space=pltpu.SMEM),
                  pl.BlockSpec(memory_space=pltpu.VMEM)],
        out_specs=pl.BlockSpec((gb * SUBLANES, LANES), lambda b: (b, 0)),
        out_shape=jax.ShapeDtypeStruct((n_slots * SUBLANES, LANES), I32),
        compiler_params=_params(1, resident // MIB + 8),
        name="dispatch_gather",
    )(slot_tab.reshape(nblk, 1, gb), h2p)


ROW_DMA_PRIORITY = 1
EXPERT_CHUNKS = 1
ROW_SLOTS = 4


def _expert_kernel(first_ref, nblk_ref, xs_hbm, wg_ref, wu_ref, wd_ref, yb_hbm,
                   xbuf_ref, obuf_ref, in_sem, out_sem, wgb_ref, wub_ref, wdb_ref, stage_in_ref, stage_out_ref,
                   *, bm):
    e = pl.program_id(0)
    nb = nblk_ref[e]
    b0 = first_ref[e]
    rows = bm * SUBLANES

    def x_copy(j, slot, base=None):
        blk = (b0 if base is None else base) + j
        return pltpu.make_async_copy(
            xs_hbm.at[pl.ds(pl.multiple_of(blk * rows, rows), rows), :],
            xbuf_ref.at[pl.ds(pl.multiple_of(slot * rows, rows), rows), :], in_sem.at[slot])

    def y_copy(j, slot):
        return pltpu.make_async_copy(
            obuf_ref.at[pl.ds(pl.multiple_of(slot * rows, rows), rows), :],
            yb_hbm.at[pl.ds(pl.multiple_of((b0 + j) * rows, rows), rows), :], out_sem.at[slot])

    def start_head(count, base=None):
        for j in range(ROW_SLOTS - 1):
            @pl.when(count > j)
            def _(j=j):
                x_copy(j, j, base).start(priority=ROW_DMA_PRIORITY)

    @pl.when(e == 0)
    def _():
        start_head(nb)

    @pl.when(nb > 0)
    def _():
        wgb_ref[...] = wg_ref[...].astype(BF16)
        wub_ref[...] = wu_ref[...].astype(BF16)
        wdb_ref[...] = wd_ref[...].astype(BF16)

    def block(j, carry):
        slot = lax.rem(j, ROW_SLOTS)
        x_copy(j, slot).wait()

        ahead = j + (ROW_SLOTS - 1)

        @pl.when(ahead < nb)
        def _():
            x_copy(ahead, lax.rem(ahead, ROW_SLOTS)).start(priority=ROW_DMA_PRIORITY)

        @pl.when(j >= ROW_SLOTS)
        def _():
            y_copy(j - ROW_SLOTS, slot).wait()

        xbase = slot * rows
        sub = bm // EXPERT_CHUNKS
        for c in range(EXPERT_CHUNKS):
            cbase = xbase + c * sub * SUBLANES
            los, his = [], []
            for s in range(SUBLANES):
                lo, hi = _unpack_word(xbuf_ref[pl.ds(cbase + s, sub, stride=SUBLANES), :], stage_in_ref,
                                      c * SUBLANES + s)
                los.append(lo.astype(BF16))
                his.append(hi.astype(BF16))
            x = jnp.concatenate(los + his, axis=1)
            g = jnp.dot(x, wgb_ref[...], preferred_element_type=F32)
            u = jnp.dot(x, wub_ref[...], preferred_element_type=F32)
            a = (_silu(g) * u).astype(BF16)
            y = jnp.dot(a, wdb_ref[...], preferred_element_type=F32)
            _pack_rows(y, obuf_ref, stage_out_ref, sub, row0=cbase, stage0=c * sub * 2 * SUBLANES)
        y_copy(j, slot).start()
        return carry

    lax.fori_loop(0, nb, block, 0)

    @pl.when(e + 1 < pl.num_programs(0))
    def _():
        nxt = jnp.minimum(e + 1, pl.num_programs(0) - 1)
        start_head(nblk_ref[nxt], first_ref[nxt])

    for back in range(ROW_SLOTS, 0, -1):
        @pl.when(nb >= back)
        def _(back=back):
            y_copy(nb - back, lax.rem(nb - back, ROW_SLOTS)).wait()

    @pl.when(e == pl.num_programs(0) - 1)
    def _():
        obuf_ref[pl.ds(0, rows), :] = pltpu.bitcast(jnp.zeros((2 * rows, LANES), BF16), I32)

        def clear(j, carry):
            cp = pltpu.make_async_copy(obuf_ref.at[pl.ds(0, rows), :],
                                       yb_hbm.at[pl.ds(pl.multiple_of(j * rows, rows), rows), :], out_sem.at[0])
            cp.start()
            cp.wait()
            return carry

        lax.fori_loop(b0 + nb, yb_hbm.shape[0] // rows, clear, 0)


def _experts(first_block, n_blocks_e, xs, w_gate, w_up, w_down, bm):
    n_exp, d, de = w_gate.shape
    rows = bm * SUBLANES
    grid_spec = pltpu.PrefetchScalarGridSpec(
        num_scalar_prefetch=2,
        grid=(n_exp,),
        in_specs=[pl.BlockSpec(memory_space=pl.ANY),
                  pl.BlockSpec((None, d, de), lambda e, fb, nb: (e, 0, 0)),
                  pl.BlockSpec((None, d, de), lambda e, fb, nb: (e, 0, 0)),
                  pl.BlockSpec((None, de, d), lambda e, fb, nb: (e, 0, 0))],
        out_specs=pl.BlockSpec(memory_space=pl.ANY),
        scratch_shapes=[pltpu.VMEM((ROW_SLOTS * rows, LANES), I32), pltpu.VMEM((ROW_SLOTS * rows, LANES), I32),
                        pltpu.SemaphoreType.DMA((ROW_SLOTS,)), pltpu.SemaphoreType.DMA((ROW_SLOTS,)),
                        pltpu.VMEM((d, de), BF16), pltpu.VMEM((d, de), BF16), pltpu.VMEM((de, d), BF16),
                        pltpu.VMEM((SUBLANES * 2 * bm, LANES), F32), pltpu.VMEM((SUBLANES * 2 * bm, LANES), F32)],
    )
    return pl.pallas_call(
        functools.partial(_expert_kernel, bm=bm),
        grid_spec=grid_spec,
        out_shape=jax.ShapeDtypeStruct(xs.shape, I32),
        compiler_params=_params(1, 56),
        name="routed_experts",
    )(first_block, n_blocks_e, xs, w_gate, w_up, w_down)


def _final_kernel(dest_ref, dnext_ref, w_ref, yb_hbm, x1_ref, h2_ref, gate_ref, gpost_ref, wsg_ref, wsu_ref,
                  wsd_ref, o_ref, buf_ref, sem, stage_ref, *, tiles_per_row):
    tm = x1_ref.shape[0]
    i = pl.program_id(0)
    last = pl.num_programs(0) - 1
    r = i // tiles_per_row
    tile_rows = TOP_K * tm * SUBLANES
    cur = (i % 2) * tile_rows
    nxt = tile_rows - cur

    def request(dref, base, slot, t, k):
        src = pl.multiple_of(dref[0, k, t] * SUBLANES, SUBLANES)
        dst = pl.multiple_of(base + (k * tm + t) * SUBLANES, SUBLANES)
        pltpu.make_async_copy(yb_hbm.at[pl.ds(src, SUBLANES), :], buf_ref.at[pl.ds(dst, SUBLANES), :],
                              sem.at[slot]).start(priority=k % 2)

    def wait_tile(base, slot):
        pltpu.make_async_copy(yb_hbm.at[pl.ds(0, tile_rows), :],
                              buf_ref.at[pl.ds(pl.multiple_of(base, tile_rows), tile_rows), :], sem.at[slot]).wait()

    @pl.when(i == 0)
    def _():
        def first(t, carry):
            for k in range(TOP_K):
                request(dest_ref, 0, 0, t, k)
            return carry
        lax.fori_loop(0, tm, first, 0)

    wait_tile(cur, i % 2)

    pending = [(t, k) for t in range(tm) for k in range(TOP_K)]
    per_piece = 2 * len(pending) // (SUBLANES * TOP_K)

    def request_piece(n):
        for t, k in pending[n * per_piece:(n + 1) * per_piece]:
            request(dnext_ref, nxt, (i + 1) % 2, t, k)

    h = h2_ref[...]
    g = jnp.dot(h, wsg_ref[...], preferred_element_type=F32)
    u = jnp.dot(h, wsu_ref[...], preferred_element_type=F32)
    shared = jnp.dot((_silu(g) * u).astype(BF16), wsd_ref[...], preferred_element_type=F32)

    w = w_ref[...]
    los, his = [], []
    for s in range(SUBLANES):
        lo_acc = jnp.zeros((tm, LANES), F32)
        hi_acc = jnp.zeros((tm, LANES), F32)
        for k in range(TOP_K):
            lo, hi = _unpack_word(buf_ref[pl.ds(cur + k * tm * SUBLANES + s, tm, stride=SUBLANES), :],
                                  stage_ref, s * TOP_K + k)
            wk = w[:, k:k + 1]
            lo_acc = lo_acc + wk * lo
            hi_acc = hi_acc + wk * hi
            request_piece(s * TOP_K + k)
        los.append(lo_acc)
        his.append(hi_acc)
    ffn = jnp.concatenate(los + his, axis=1) + shared
    o_ref[...] = x1_ref[...] + gate_ref[pl.ds(r, 1), :] * _rms(ffn, gpost_ref[...])

    @pl.when(i == last)
    def _():
        wait_tile(nxt, (i + 1) % 2)


def _final(dest, w_tok, yb, x1, h2b, mod, g_post, ws_gate, ws_up, ws_down):
    t, d = x1.shape
    nb = mod.shape[0]
    ds_ = ws_gate.shape[1]
    tm = dest.shape[2]
    n_tiles = t // tm
    row = lambda i: (i, 0)
    fixed = lambda i: (0, 0)
    return pl.pallas_call(
        functools.partial(_final_kernel, tiles_per_row=t // nb // tm),
        grid=(n_tiles,),
        in_specs=[pl.BlockSpec((1, TOP_K, tm), lambda i: (i, 0, 0), memory_space=pltpu.SMEM),
                  pl.BlockSpec((1, TOP_K, tm), lambda i: (jnp.minimum(i + 1, n_tiles - 1), 0, 0),
                               memory_space=pltpu.SMEM),
                  pl.BlockSpec((tm, TOP_K), row),
                  pl.BlockSpec(memory_space=pl.ANY),
                  pl.BlockSpec((tm, d), row), pl.BlockSpec((tm, d), row),
                  pl.BlockSpec((nb, d), lambda i: (0, 5)),
                  pl.BlockSpec((1, d), fixed),
                  pl.BlockSpec((d, ds_), fixed), pl.BlockSpec((d, ds_), fixed), pl.BlockSpec((ds_, d), fixed)],
        out_specs=pl.BlockSpec((tm, d), row),
        out_shape=jax.ShapeDtypeStruct((t, d), F32),
        scratch_shapes=[pltpu.VMEM((2 * TOP_K * tm * SUBLANES, LANES), I32), pltpu.SemaphoreType.DMA((2,)),
                        pltpu.VMEM((SUBLANES * TOP_K * 2 * tm, LANES), F32)],
        compiler_params=_params(1, 48),
        name="combine_final",
    )(dest, dest, w_tok, yb, x1, h2b, mod, g_post, ws_gate, ws_up, ws_down)


def _rope_tables(n_pos):
    half = LANES // 2
    pos = jnp.arange(n_pos)
    row = (pos // GRID_W).astype(F32)
    col = (pos % GRID_W).astype(F32)
    inv = ROPE_THETA ** (-jnp.arange(0, half, 2, dtype=F32) / half)
    ang = jnp.concatenate([row[:, None] * inv, col[:, None] * inv], axis=-1)
    cos = jnp.repeat(jnp.cos(ang), 2, axis=-1)
    sin = jnp.repeat(jnp.sin(ang), 2, axis=-1) * jnp.tile(jnp.array([-1.0, 1.0], F32), half)
    return cos, sin


def _sublayer1(x, mod, p, seq_len, rope, cache, n_heads, n_kv, tq):
    t = x.shape[0]
    h = _prenorm(x, mod, p['g_pre1'], 0, 1)
    z = _matmul_wcast(h, p['w_in'], _z_col_map, 9 * 1024, BF16)
    kv = _matmul_wcast(h, p['w_in'], lambda j: j + 5, 1024, F32)
    qn, kn, vb, k32, v32 = _qk_prep(z, kv, p['g_q'], p['g_k'], rope, seq_len, n_heads, n_kv)
    attn = _attention(qn, kn, vb, cache, t // seq_len, seq_len, n_heads, n_kv, tq)
    cvg = _gated_conv(z, p['conv_w'], p['conv_b'], seq_len, 6)
    merged = _merge(cvg, attn, z, p['w_conv_out'], p['w_attn_out'], 4, 8)
    x1, h2b, h2p, logits_t = _out_proj(merged, x, mod, p['g_post1'], p['g_pre2'], p['w_out_b'], p['w_router_t'])
    return x1, h2b, h2p, logits_t, k32, v32


def kernel(x_prompt, x_sample, cache_k, cache_v, c, c_ctx, w_mod, b_mod, g_pre1, w_in, conv_w, conv_b, g_q, g_k, w_conv_out, w_attn_out, w_out, g_post1, g_pre2, w_router, b_router, w_e_gate, w_e_up, w_e_down, w_s_gate, w_s_up, w_s_down, g_post2):
    batch, seq, d = x_prompt.shape
    dec_batch, dec_seq, _ = x_sample.shape
    depth = w_mod.shape[0]
    assert depth == 1
    past, n_kv, head_dim = cache_k.shape[2:]
    assert head_dim == LANES
    n_heads = w_attn_out.shape[1] // head_dim
    n_exp = w_router.shape[2]
    t_ctx, t_lat = batch * seq, dec_batch * dec_seq
    t_all = t_ctx + t_lat
    l = 0

    p = {
        'g_pre1': g_pre1[l][None], 'w_in': w_in[l], 'conv_w': conv_w[l], 'conv_b': conv_b[l][None],
        'g_q': g_q[l][None], 'g_k': g_k[l][None], 'w_conv_out': w_conv_out[l], 'w_attn_out': w_attn_out[l],
        'w_out_b': w_out[l].astype(BF16), 'g_post1': g_post1[l][None], 'g_pre2': g_pre2[l][None],
        'w_router_t': w_router[l].T,
    }
    cond = jnp.concatenate([c_ctx[None], c, jnp.zeros((SUBLANES - 1 - dec_batch, d), F32)], axis=0)
    mod = _modulation(cond, w_mod[l], b_mod[l][None])
    mod_ctx, mod_lat = mod[0:1], mod[1:1 + dec_batch]

    xc = x_prompt.reshape(t_ctx, d)
    xl = x_sample.reshape(t_lat, d)
    cache = (cache_k[:, l].reshape(dec_batch, past, n_kv * head_dim),
             cache_v[:, l].reshape(dec_batch, past, n_kv * head_dim))
    rope = _rope_tables(dec_seq)

    x1c, h2bc, h2pc, logc, k32, v32 = _sublayer1(xc, mod_ctx, p, seq, None, None, n_heads, n_kv, 256)
    x1l, h2bl, h2pl, logl, _, _ = _sublayer1(xl, mod_lat, p, dec_seq, rope, cache, n_heads, n_kv, 256)
    idx_all, w_all, rank_all, cnt = _router(jnp.concatenate([logc, logl], axis=1), b_router[l][:, None])

    bm = 256
    n_blocks = (t_all * TOP_K + n_exp * (bm - 1)) // bm + 1
    n_slots = n_blocks * bm
    blocks_e = jnp.floor((cnt[:, 0] + (bm - 1)) / bm)
    first_e = jnp.cumsum(blocks_e) - blocks_e
    pstart = (first_e * bm).astype(I32)

    tmf = 128
    w_tok = w_all.T
    dest = _dest_slots(pstart, idx_all, rank_all)
    dest3 = dest.reshape(TOP_K, t_all // tmf, tmf).transpose(1, 0, 2)
    slot_tab = _slot_table(dest3, n_slots)

    h2p = jnp.concatenate([h2pc, h2pl], axis=0)
    xs = _gather_rows(slot_tab, h2p, GATHER_BLOCK)
    yb = _experts(first_e.astype(I32), blocks_e.astype(I32), xs, w_e_gate[l], w_e_up[l], w_e_down[l], bm)

    ws = (w_s_gate[l].astype(BF16), w_s_up[l].astype(BF16), w_s_down[l].astype(BF16))

    def finish(lo, n, x1, h2b, mod_g):
        dest_g = dest3[lo // tmf:(lo + n) // tmf]
        return _final(dest_g, w_tok[lo:lo + n], yb, x1, h2b, mod_g, g_post2[l][None], *ws)

    y_ctx = finish(0, t_ctx, x1c, h2bc, mod_ctx)
    y_lat = finish(t_ctx, t_lat, x1l, h2bl, mod_lat)

    new_k = k32.reshape(batch, 1, seq, n_kv, head_dim)
    new_v = v32.reshape(batch, 1, seq, n_kv, head_dim)
    return (y_ctx.reshape(batch, seq, d), y_lat.reshape(dec_batch, dec_seq, d), new_k, new_v)
```

```python
import functools

import jax
import jax.numpy as jnp
from jax import lax
from jax.experimental import pallas as pl
from jax.experimental.pallas import tpu as pltpu

GRID_W = 64
ROPE_THETA = 10000.0
N_GROUPS = 8
TOPK_GROUPS = 4
TOP_K = 8
TOP_K_LOG2 = 3
ROUTED_SCALE = 2.5
EPS = 1e-6

LANES = 128
SUBLANES = 8
V7X_VMEM_BYTES = 64 * 1024 * 1024
MIB = 1024 * 1024

F32 = jnp.float32
BF16 = jnp.bfloat16
I32 = jnp.int32


def _params(n_grid, vmem_mib):
    assert vmem_mib * MIB < V7X_VMEM_BYTES
    return pltpu.CompilerParams(
        dimension_semantics=("arbitrary",) * n_grid, vmem_limit_bytes=vmem_mib * MIB)


def _silu(x):
    return x * jax.nn.sigmoid(x)


def _rms(x, g):
    return x * lax.rsqrt(jnp.mean(x * x, axis=-1, keepdims=True) + EPS) * g


def _pack_rows(val, out_ref, stage_ref, rows, row0=0, stage0=0):
    half = val.shape[1] // 2
    for s in range(half // LANES):
        base = stage0 + s * 2 * rows
        stage_ref[pl.ds(base, rows, stride=2), :] = val[:, s * LANES:(s + 1) * LANES]
        stage_ref[pl.ds(base + 1, rows, stride=2), :] = val[:, half + s * LANES:half + (s + 1) * LANES]
        pair = stage_ref[pl.ds(base, 2 * rows), :].astype(BF16)
        out_ref[pl.ds(row0 + s, rows, stride=SUBLANES), :] = pltpu.bitcast(pair, I32)


def _unpack_word(word, stage_ref, slot):
    rows = word.shape[0]
    base = slot * 2 * rows
    stage_ref[pl.ds(base, 2 * rows), :] = pltpu.bitcast(word, BF16).astype(F32)
    return stage_ref[pl.ds(base, rows, stride=2), :], stage_ref[pl.ds(base + 1, rows, stride=2), :]


def _mod_kernel(c_ref, w_ref, b_ref, o_ref):
    s = _silu(c_ref[...]).astype(BF16)
    o_ref[...] = jnp.dot(s, w_ref[...].astype(BF16), preferred_element_type=F32) + b_ref[...]


def _modulation(cond, w, b):
    rows, d = cond.shape
    n = w.shape[1]
    tn = 1024
    return pl.pallas_call(
        _mod_kernel,
        grid=(n // tn,),
        in_specs=[pl.BlockSpec((rows, d), lambda j: (0, 0)),
                  pl.BlockSpec((d, tn), lambda j: (0, j)),
                  pl.BlockSpec((1, tn), lambda j: (0, j))],
        out_specs=pl.BlockSpec((rows, tn), lambda j: (0, j)),
        out_shape=jax.ShapeDtypeStruct((rows, n), F32),
        compiler_params=_params(1, 40),
        name="modulation",
    )(cond, w, b)


def _prenorm_kernel(x_ref, shift_ref, scale_ref, g_ref, o_ref, *, tiles_per_row):
    r = pl.program_id(0) // tiles_per_row
    y = _rms(x_ref[...], g_ref[...])
    o_ref[...] = (y * (1.0 + scale_ref[pl.ds(r, 1), :]) + shift_ref[pl.ds(r, 1), :]).astype(o_ref.dtype)


def _prenorm(x, mod, g, shift_col, scale_col):
    t, d = x.shape
    nb = mod.shape[0]
    tm = 512
    return pl.pallas_call(
        functools.partial(_prenorm_kernel, tiles_per_row=t // nb // tm),
        grid=(t // tm,),
        in_specs=[pl.BlockSpec((tm, d), lambda i: (i, 0)),
                  pl.BlockSpec((nb, d), lambda i: (0, shift_col)),
                  pl.BlockSpec((nb, d), lambda i: (0, scale_col)),
                  pl.BlockSpec((1, d), lambda i: (0, 0))],
        out_specs=pl.BlockSpec((tm, d), lambda i: (i, 0)),
        out_shape=jax.ShapeDtypeStruct((t, d), BF16),
        compiler_params=_params(1, 32),
        name="prenorm",
    )(x, mod, mod, g)


def _mm_wcast_kernel(a_ref, w_ref, o_ref, wb_ref):
    @pl.when(pl.program_id(1) == 0)
    def _():
        wb_ref[...] = w_ref[...].astype(BF16)

    o_ref[...] = jnp.dot(a_ref[...], wb_ref[...], preferred_element_type=F32).astype(o_ref.dtype)


def _matmul_wcast(a, w, col_map, n_out, out_dtype):
    m, k = a.shape
    tn = 1024
    tm = 2048 if jnp.dtype(out_dtype).itemsize == 2 else 1024
    return pl.pallas_call(
        _mm_wcast_kernel,
        grid=(n_out // tn, m // tm),
        in_specs=[pl.BlockSpec((tm, k), lambda j, i: (i, 0)),
                  pl.BlockSpec((k, tn), lambda j, i: (0, col_map(j)))],
        out_specs=pl.BlockSpec((tm, tn), lambda j, i: (i, j)),
        out_shape=jax.ShapeDtypeStruct((m, n_out), out_dtype),
        scratch_shapes=[pltpu.VMEM((k, tn), BF16)],
        compiler_params=_params(2, 56),
        name="in_proj",
    )(a, w)


def _z_col_map(j):
    return jnp.where(j < 2, j + 3, jnp.where(j < 6, j + 4, j - 6))


def _qk_kernel(*refs, use_rope, n_heads, n_kv, q_scale):
    if use_rope:
        q_ref, kv_ref, gq_ref, gk_ref, cos_ref, sin_ref, qn_ref, kn_ref, vb_ref, k32_ref, v32_ref = refs
        cos = cos_ref[...]
        sin = sin_ref[...]
        even = lax.broadcasted_iota(I32, cos.shape, 1) % 2 == 0
    else:
        q_ref, kv_ref, gq_ref, gk_ref, qn_ref, kn_ref, vb_ref, k32_ref, v32_ref = refs

    def norm_rope(xh, g):
        y = _rms(xh, g)
        if use_rope:
            sw = jnp.where(even, pltpu.roll(y, LANES - 1, 1), pltpu.roll(y, 1, 1))
            y = y * cos + sw * sin
        return y

    gq = gq_ref[...]
    gk = gk_ref[...]
    for h in range(n_heads):
        sl = slice(h * LANES, (h + 1) * LANES)
        qn_ref[:, sl] = (norm_rope(q_ref[:, sl].astype(F32), gq) * q_scale).astype(BF16)
    kw = n_kv * LANES
    for h in range(n_kv):
        sl = slice(h * LANES, (h + 1) * LANES)
        kh = norm_rope(kv_ref[:, sl], gk)
        k32_ref[:, sl] = kh
        kn_ref[:, sl] = kh.astype(BF16)
    v = kv_ref[:, kw:2 * kw]
    v32_ref[...] = v
    vb_ref[...] = v.astype(BF16)


def _qk_prep(z, kv, gq, gk, rope, seq_len, n_heads, n_kv):
    t = z.shape[0]
    dq = n_heads * LANES
    dk = n_kv * LANES
    tm = 256
    in_specs = [pl.BlockSpec((tm, dq), lambda i: (i, 0)),
                pl.BlockSpec((tm, 2 * dk), lambda i: (i, 0)),
                pl.BlockSpec((1, LANES), lambda i: (0, 0)),
                pl.BlockSpec((1, LANES), lambda i: (0, 0))]
    args = [z, kv, gq, gk]
    if rope is not None:
        per_seq = seq_len // tm
        in_specs += [pl.BlockSpec((tm, LANES), lambda i: (i % per_seq, 0))] * 2
        args += list(rope)
    out_specs = [pl.BlockSpec((tm, dq), lambda i: (i, 0))] + [pl.BlockSpec((tm, dk), lambda i: (i, 0))] * 4
    out_shape = [jax.ShapeDtypeStruct((t, dq), BF16), jax.ShapeDtypeStruct((t, dk), BF16),
                 jax.ShapeDtypeStruct((t, dk), BF16), jax.ShapeDtypeStruct((t, dk), F32),
                 jax.ShapeDtypeStruct((t, dk), F32)]
    return pl.pallas_call(
        functools.partial(_qk_kernel, use_rope=rope is not None, n_heads=n_heads, n_kv=n_kv,
                          q_scale=LANES ** -0.5),
        grid=(t // tm,),
        in_specs=in_specs, out_specs=out_specs, out_shape=out_shape,
        compiler_params=_params(1, 32),
        name="qk_prep",
    )(*args)


def _attn_kernel(*refs, has_cache, group, chunks):
    if has_cache:
        q_ref, k_ref, v_ref, ck_ref, cv_ref, o_ref = refs
    else:
        q_ref, k_ref, v_ref, o_ref = refs
    tq = q_ref.shape[0]
    nt = (((1,), (1,)), ((), ()))
    if has_cache:
        ck = ck_ref[...].astype(BF16)
        cv = cv_ref[...].astype(BF16)
    per = group // chunks
    for c in range(chunks):
        heads = range(c * per, (c + 1) * per)
        q = jnp.concatenate([q_ref[:, g * LANES:(g + 1) * LANES] for g in heads], axis=0)
        s_own = lax.dot_general(q, k_ref[...], nt, preferred_element_type=F32)
        m = jnp.max(s_own, axis=-1, keepdims=True)
        if has_cache:
            s_ctx = lax.dot_general(q, ck, nt, preferred_element_type=F32)
            m = jnp.maximum(m, jnp.max(s_ctx, axis=-1, keepdims=True))
        p = jnp.exp(s_own - m)
        denom = jnp.sum(p, axis=-1, keepdims=True)
        acc = jnp.dot(p.astype(BF16), v_ref[...], preferred_element_type=F32)
        if has_cache:
            pc = jnp.exp(s_ctx - m)
            denom = denom + jnp.sum(pc, axis=-1, keepdims=True)
            acc = acc + jnp.dot(pc.astype(BF16), cv, preferred_element_type=F32)
        o = acc / denom
        for n, g in enumerate(heads):
            o_ref[:, g * LANES:(g + 1) * LANES] = o[n * tq:(n + 1) * tq].astype(o_ref.dtype)


def _attention(qn, kn, vb, cache, batch, seq_len, n_heads, n_kv, tq):
    t = qn.shape[0]
    group = n_heads // n_kv
    nq = seq_len // tq
    in_specs = [pl.BlockSpec((tq, group * LANES), lambda b, h, i: (b * nq + i, h)),
                pl.BlockSpec((seq_len, LANES), lambda b, h, i: (b, h)),
                pl.BlockSpec((seq_len, LANES), lambda b, h, i: (b, h))]
    args = [qn, kn, vb]
    if cache is not None:
        past = cache[0].shape[1]
        in_specs += [pl.BlockSpec((None, past, LANES), lambda b, h, i: (b, 0, h))] * 2
        args += list(cache)
    return pl.pallas_call(
        functools.partial(_attn_kernel, has_cache=cache is not None, group=group,
                          chunks=group if cache is not None else group // 2),
        grid=(batch, n_kv, nq),
        in_specs=in_specs,
        out_specs=pl.BlockSpec((tq, group * LANES), lambda b, h, i: (b * nq + i, h)),
        out_shape=jax.ShapeDtypeStruct((t, n_heads * LANES), BF16),
        compiler_params=_params(3, 40),
        name="attention",
    )(*args)


CONV_HALO = 16


def _conv_kernel(u_ref, b_ref, c_ref, up_ref, cp_ref, un_ref, cn_ref, w_ref, bias_ref, o_ref, *,
                 tiles_per_seq):
    tm = u_ref.shape[0]
    pos = pl.program_id(0) % tiles_per_seq
    cu = c_ref[...].astype(F32) * u_ref[...].astype(F32)
    halo_prev = (cp_ref[...].astype(F32) * up_ref[...].astype(F32))[CONV_HALO - 1:CONV_HALO, :]
    halo_next = (cn_ref[...].astype(F32) * un_ref[...].astype(F32))[0:1, :]
    halo_prev = jnp.where(pos == 0, 0.0, halo_prev)
    halo_next = jnp.where(pos == tiles_per_seq - 1, 0.0, halo_next)
    row = lax.broadcasted_iota(I32, cu.shape, 0)
    prev = jnp.where(row == 0, halo_prev, pltpu.roll(cu, 1, 0))
    nxt = jnp.where(row == tm - 1, halo_next, pltpu.roll(cu, tm - 1, 0))
    w = w_ref[...]
    conv = prev * w[0:1, :] + cu * w[1:2, :] + nxt * w[2:3, :] + bias_ref[...]
    o_ref[...] = (b_ref[...].astype(F32) * conv).astype(o_ref.dtype)


def _gated_conv(z, conv_w, conv_b, seq_len, col0):
    t = z.shape[0]
    dc = conv_w.shape[1]
    tm = 256
    hb = tm // CONV_HALO
    last = t // CONV_HALO - 1
    prev_map = lambda c: (lambda i: (jnp.maximum(i * hb - 1, 0), c))
    next_map = lambda c: (lambda i: (jnp.minimum((i + 1) * hb, last), c))
    in_specs = [pl.BlockSpec((tm, dc), lambda i: (i, col0)),
                pl.BlockSpec((tm, dc), lambda i: (i, col0 + 1)),
                pl.BlockSpec((tm, dc), lambda i: (i, col0 + 2)),
                pl.BlockSpec((CONV_HALO, dc), prev_map(col0)),
                pl.BlockSpec((CONV_HALO, dc), prev_map(col0 + 2)),
                pl.BlockSpec((CONV_HALO, dc), next_map(col0)),
                pl.BlockSpec((CONV_HALO, dc), next_map(col0 + 2)),
                pl.BlockSpec(conv_w.shape, lambda i: (0, 0)),
                pl.BlockSpec((1, dc), lambda i: (0, 0))]
    return pl.pallas_call(
        functools.partial(_conv_kernel, tiles_per_seq=seq_len // tm),
        grid=(t // tm,),
        in_specs=in_specs,
        out_specs=pl.BlockSpec((tm, dc), lambda i: (i, 0)),
        out_shape=jax.ShapeDtypeStruct((t, dc), BF16),
        compiler_params=_params(1, 32),
        name="gated_conv",
    )(z, z, z, z, z, z, z, conv_w, conv_b)


def _merge_kernel(cv_ref, at_ref, gc_ref, ga_ref, wc_ref, wa_ref, o_ref, wcb_ref, wab_ref):
    @pl.when(pl.program_id(1) == 0)
    def _():
        wcb_ref[...] = wc_ref[...].astype(BF16)
        wab_ref[...] = wa_ref[...].astype(BF16)

    conv_out = jnp.dot(cv_ref[...], wcb_ref[...], preferred_element_type=F32)
    attn_out = jnp.dot(at_ref[...], wab_ref[...], preferred_element_type=F32)
    merged = (jax.nn.sigmoid(gc_ref[...].astype(F32)) * conv_out
              + jax.nn.sigmoid(ga_ref[...].astype(F32)) * attn_out)
    o_ref[...] = merged.astype(o_ref.dtype)


def _merge(cvg, attn, z, w_conv_out, w_attn_out, gc_col0, ga_col0):
    t, dc = cvg.shape
    dq = attn.shape[1]
    d = w_conv_out.shape[1]
    tm, tn = 512, 512
    return pl.pallas_call(
        _merge_kernel,
        grid=(d // tn, t // tm),
        in_specs=[pl.BlockSpec((tm, dc), lambda j, i: (i, 0)),
                  pl.BlockSpec((tm, dq), lambda j, i: (i, 0)),
                  pl.BlockSpec((tm, tn), lambda j, i: (i, gc_col0 + j)),
                  pl.BlockSpec((tm, tn), lambda j, i: (i, ga_col0 + j)),
                  pl.BlockSpec((dc, tn), lambda j, i: (0, j)),
                  pl.BlockSpec((dq, tn), lambda j, i: (0, j))],
        out_specs=pl.BlockSpec((tm, tn), lambda j, i: (i, j)),
        out_shape=jax.ShapeDtypeStruct((t, d), BF16),
        scratch_shapes=[pltpu.VMEM((dc, tn), BF16), pltpu.VMEM((dq, tn), BF16)],
        compiler_params=_params(2, 40),
        name="merge",
    )(cvg, attn, z, z, w_conv_out, w_attn_out)


def _route(logits_t, bias_col, carry):
    n_exp, tm = logits_t.shape
    per = n_exp // N_GROUPS
    assert per == SUBLANES
    neg = -jnp.inf
    scores = jax.nn.sigmoid(logits_t)
    biased = scores + bias_col
    sub = lax.broadcasted_iota(I32, (per, tm), 0).astype(F32)
    xs = [biased[g * per:(g + 1) * per, :] for g in range(N_GROUPS)]
    sc = [scores[g * per:(g + 1) * per, :] for g in range(N_GROUPS)]
    ids = [sub + float(g * per) for g in range(N_GROUPS)]

    def colmax(a):
        return jnp.max(a, axis=0, keepdims=True)

    def colmin(a):
        return jnp.min(a, axis=0, keepdims=True)

    rows = []
    for g in range(N_GROUPS):
        m1 = colmax(xs[g])
        j1 = colmin(jnp.where(xs[g] == m1, sub, float(per)))
        m2 = colmax(jnp.where(sub == j1, neg, xs[g]))
        rows.append(m1 + m2)
    gs = jnp.concatenate(rows, axis=0)
    gsel = jnp.zeros_like(gs)
    for _ in range(TOPK_GROUPS):
        m = colmax(gs)
        j = colmin(jnp.where(gs == m, sub, float(N_GROUPS)))
        hit = sub == j
        gsel = jnp.where(hit, 1.0, gsel)
        gs = jnp.where(hit, neg, gs)
    masked = [jnp.where(gsel[g:g + 1, :] > 0.0, xs[g], neg) for g in range(N_GROUPS)]
    idx_rows, w_rows = [], []
    member = [jnp.zeros((per, tm), F32) for _ in range(N_GROUPS)]
    for _ in range(TOP_K):
        mm = masked[0]
        for g in range(1, N_GROUPS):
            mm = jnp.maximum(mm, masked[g])
        m = colmax(mm)
        idx = colmin(jnp.where(masked[0] == m, ids[0], float(n_exp)))
        for g in range(1, N_GROUPS):
            idx = jnp.minimum(idx, colmin(jnp.where(masked[g] == m, ids[g], float(n_exp))))
        wk = jnp.zeros_like(idx)
        for g in range(N_GROUPS):
            hit = ids[g] == idx
            wk = wk + jnp.sum(jnp.where(hit, sc[g], 0.0), axis=0, keepdims=True)
            masked[g] = jnp.where(hit, neg, masked[g])
            member[g] = jnp.where(hit, 1.0, member[g])
        idx_rows.append(idx)
        w_rows.append(wk)
    w = jnp.concatenate(w_rows, axis=0)
    w = w / jnp.sum(w, axis=0, keepdims=True) * ROUTED_SCALE

    earlier = (lax.broadcasted_iota(I32, (tm, tm), 0) < lax.broadcasted_iota(I32, (tm, tm), 1)).astype(BF16)
    before = jnp.dot(jnp.concatenate(member, axis=0).astype(BF16), earlier, preferred_element_type=F32)
    rank_rows = []
    for k in range(TOP_K):
        rk = jnp.zeros_like(idx_rows[k])
        for g in range(N_GROUPS):
            pos = before[g * per:(g + 1) * per, :] + carry[g]
            rk = rk + jnp.sum(jnp.where(ids[g] == idx_rows[k], pos, 0.0), axis=0, keepdims=True)
        rank_rows.append(rk)
    new_carry = [carry[g] + jnp.sum(member[g], axis=1, keepdims=True) for g in range(N_GROUPS)]
    idx = jnp.concatenate(idx_rows, axis=0).astype(I32)
    rank = jnp.concatenate(rank_rows, axis=0).astype(I32)
    return idx, w, rank, new_carry


def _out_kernel(mg_ref, x_ref, gate_ref, shift_ref, scale_ref, gpost_ref, gpre_ref, wo_ref, wr_ref,
                x1_ref, h2b_ref, h2p_ref, logit_ref, stage_ref, *, tiles_per_row):
    tm = x_ref.shape[0]
    r = pl.program_id(0) // tiles_per_row
    gate = gate_ref[pl.ds(r, 1), :]
    scale = 1.0 + scale_ref[pl.ds(r, 1), :]
    shift = shift_ref[pl.ds(r, 1), :]
    rows = tm // OUT_CHUNKS
    for c in range(OUT_CHUNKS):
        sl = pl.ds(c * rows, rows)
        mix = jnp.dot(mg_ref[sl, :], wo_ref[...], preferred_element_type=F32)
        x1 = x_ref[sl, :] + gate * _rms(mix, gpost_ref[...])
        x1_ref[sl, :] = x1
        h2 = _rms(x1, gpre_ref[...]) * scale + shift
        h2b_ref[sl, :] = h2.astype(BF16)
        _pack_rows(h2, h2p_ref, stage_ref, rows, row0=c * rows * SUBLANES, stage0=c * rows * 2 * SUBLANES)
        logit_ref[:, sl] = lax.dot_general(wr_ref[...], h2, (((1,), (1,)), ((), ())),
                                           preferred_element_type=F32, precision=lax.Precision.HIGHEST)


OUT_CHUNKS = 4


def _out_proj(merged, x, mod, g_post, g_pre, w_out_b, w_router_t):
    t, d = x.shape
    nb = mod.shape[0]
    n_exp = w_router_t.shape[0]
    tm = 512
    row = lambda i: (i, 0)
    fixed = lambda i: (0, 0)
    once = pl.Buffered(1)
    in_specs = [pl.BlockSpec((tm, d), row), pl.BlockSpec((tm, d), row),
                pl.BlockSpec((nb, d), lambda i: (0, 2)),
                pl.BlockSpec((nb, d), lambda i: (0, 3)),
                pl.BlockSpec((nb, d), lambda i: (0, 4)),
                pl.BlockSpec((1, d), fixed), pl.BlockSpec((1, d), fixed),
                pl.BlockSpec((d, d), fixed, pipeline_mode=once),
                pl.BlockSpec((n_exp, d), fixed, pipeline_mode=once)]
    out_specs = [pl.BlockSpec((tm, d), row), pl.BlockSpec((tm, d), row),
                 pl.BlockSpec((tm * SUBLANES, LANES), row),
                 pl.BlockSpec((n_exp, tm), lambda i: (0, i))]
    out_shape = [jax.ShapeDtypeStruct((t, d), F32), jax.ShapeDtypeStruct((t, d), BF16),
                 jax.ShapeDtypeStruct((t * SUBLANES, LANES), I32),
                 jax.ShapeDtypeStruct((n_exp, t), F32)]
    return pl.pallas_call(
        functools.partial(_out_kernel, tiles_per_row=t // nb // tm),
        grid=(t // tm,),
        in_specs=in_specs, out_specs=out_specs, out_shape=out_shape,
        scratch_shapes=[pltpu.VMEM((SUBLANES * 2 * tm, LANES), F32)],
        compiler_params=_params(1, 56),
        name="out_proj",
    )(merged, x, mod, mod, mod, g_post, g_pre, w_out_b, w_router_t)


def _router_kernel(logit_ref, br_ref, idx_ref, wsel_ref, rank_ref, cnt_ref):
    per = SUBLANES

    @pl.when(pl.program_id(0) == 0)
    def _():
        cnt_ref[...] = jnp.zeros_like(cnt_ref)

    carry = [cnt_ref[g * per:(g + 1) * per, 0:1] for g in range(N_GROUPS)]
    idx, w, rank, carry = _route(logit_ref[...], br_ref[...], carry)
    idx_ref[...] = idx
    wsel_ref[...] = w
    rank_ref[...] = rank
    for g in range(N_GROUPS):
        cnt_ref[g * per:(g + 1) * per, :] = jnp.broadcast_to(carry[g], (per, LANES))


def _router(logits_t, b_router_col):
    n_exp, t = logits_t.shape
    tr = 1024
    tile = lambda i: (0, i)
    fixed = lambda i: (0, 0)
    return pl.pallas_call(
        _router_kernel,
        grid=(t // tr,),
        in_specs=[pl.BlockSpec((n_exp, tr), tile), pl.BlockSpec((n_exp, 1), fixed)],
        out_specs=[pl.BlockSpec((TOP_K, tr), tile), pl.BlockSpec((TOP_K, tr), tile),
                   pl.BlockSpec((TOP_K, tr), tile), pl.BlockSpec((n_exp, LANES), fixed)],
        out_shape=[jax.ShapeDtypeStruct((TOP_K, t), I32), jax.ShapeDtypeStruct((TOP_K, t), F32),
                   jax.ShapeDtypeStruct((TOP_K, t), I32), jax.ShapeDtypeStruct((n_exp, LANES), F32)],
        compiler_params=_params(1, 32),
        name="router",
    )(logits_t, b_router_col)


def _dest_kernel(pstart_ref, idx_ref, rank_ref, o_ref):
    idx = idx_ref[...]
    acc = rank_ref[...]
    for e in range(pstart_ref.shape[0]):
        acc = acc + jnp.where(idx == e, pstart_ref[e], 0)
    o_ref[...] = acc


def _dest_slots(pstart, idx, rank):
    return pl.pallas_call(
        _dest_kernel,
        in_specs=[pl.BlockSpec(memory_space=pltpu.SMEM),
                  pl.BlockSpec(memory_space=pltpu.VMEM), pl.BlockSpec(memory_space=pltpu.VMEM)],
        out_specs=pl.BlockSpec(memory_space=pltpu.VMEM),
        out_shape=jax.ShapeDtypeStruct(idx.shape, I32),
        name="dest_slots",
    )(pstart, idx, rank)


def _slot_table_kernel(pad_lo_ref, pad_hi_ref, dest_ref, tab_ref):
    i = pl.program_id(0)
    tm = dest_ref.shape[2]

    @pl.when(i == 0)
    def _():
        def clear_range(e, carry):
            def clear(s, c):
                tab_ref[s] = 0
                return c
            return lax.fori_loop(pad_lo_ref[e], pad_hi_ref[e], clear, carry)
        lax.fori_loop(0, pad_lo_ref.shape[0], clear_range, 0)

    def fill(t, carry):
        for k in range(TOP_K):
            tab_ref[dest_ref[0, k, t]] = (i * tm + t) * TOP_K + k
        return carry

    lax.fori_loop(0, tm, fill, 0)


def _slot_table(pad_lo, pad_hi, dest3, n_slots):
    nt, _, tm = dest3.shape
    grid_spec = pltpu.PrefetchScalarGridSpec(
        num_scalar_prefetch=2,
        grid=(nt,),
        in_specs=[pl.BlockSpec((1, TOP_K, tm), lambda i, lo, hi: (i, 0, 0), memory_space=pltpu.SMEM)],
        out_specs=pl.BlockSpec(memory_space=pltpu.SMEM),
    )
    return pl.pallas_call(
        _slot_table_kernel,
        grid_spec=grid_spec,
        out_shape=jax.ShapeDtypeStruct((n_slots,), I32),
        compiler_params=_params(1, 16),
        name="slot_table",
    )(pad_lo, pad_hi, dest3)


GATHER_UNROLL = 8
GATHER_BLOCK = 1024


def _gather_kernel(tab_ref, src_ref, o_ref):
    n = tab_ref.shape[2]

    def move(c, carry):
        for u in range(GATHER_UNROLL):
            r = c * GATHER_UNROLL + u
            tok = lax.shift_right_logical(tab_ref[0, 0, r], TOP_K_LOG2)
            src = pl.multiple_of(tok * SUBLANES, SUBLANES)
            dst = pl.multiple_of(r * SUBLANES, SUBLANES)
            o_ref[pl.ds(dst, SUBLANES), :] = src_ref[pl.ds(src, SUBLANES), :]
        return carry

    lax.fori_loop(0, n // GATHER_UNROLL, move, 0)


def _gather_rows(slot_tab, h2p, gb):
    n_slots = slot_tab.shape[0]
    nblk = n_slots // gb
    resident = h2p.size * h2p.dtype.itemsize
    return pl.pallas_call(
        _gather_kernel,
        grid=(nblk,),
        in_specs=[pl.BlockSpec((1, 1, gb), lambda b: (b, 0, 0), memory_space=pltpu.SMEM),
                  pl.BlockSpec(memory_space=pltpu.VMEM)],
        out_specs=pl.BlockSpec((gb * SUBLANES, LANES), lambda b: (b, 0)),
        out_shape=jax.ShapeDtypeStruct((n_slots * SUBLANES, LANES), I32),
        compiler_params=_params(1, resident // MIB + 8),
        name="dispatch_gather",
    )(slot_tab.reshape(nblk, 1, gb), h2p)


ROW_DMA_PRIORITY = 1
EXPERT_CHUNKS = 1
X_SLOTS = 6
Y_SLOTS = 3


def _expert_kernel(first_ref, nblk_ref, xs_hbm, wg_ref, wu_ref, wd_ref, yb_hbm,
                   xbuf_ref, obuf_ref, in_sem, out_sem, wgb_ref, wub_ref, wdb_ref, stage_in_ref, stage_out_ref,
                   *, bm):
    e = pl.program_id(0)
    nb = nblk_ref[e]
    b0 = first_ref[e]
    rows = bm * SUBLANES

    def x_copy(j, slot, base=None):
        blk = (b0 if base is None else base) + j
        return pltpu.make_async_copy(
            xs_hbm.at[pl.ds(pl.multiple_of(blk * rows, rows), rows), :],
            xbuf_ref.at[pl.ds(pl.multiple_of(slot * rows, rows), rows), :], in_sem.at[slot])

    def y_copy(j, slot):
        return pltpu.make_async_copy(
            obuf_ref.at[pl.ds(pl.multiple_of(slot * rows, rows), rows), :],
            yb_hbm.at[pl.ds(pl.multiple_of((b0 + j) * rows, rows), rows), :], out_sem.at[slot])

    def start_head(count, base=None):
        for j in range(X_SLOTS - 1):
            @pl.when(count > j)
            def _(j=j):
                x_copy(j, j, base).start(priority=ROW_DMA_PRIORITY)

    @pl.when(e == 0)
    def _():
        start_head(nb)

    @pl.when(nb > 0)
    def _():
        wgb_ref[...] = wg_ref[...].astype(BF16)
        wub_ref[...] = wu_ref[...].astype(BF16)
        wdb_ref[...] = wd_ref[...].astype(BF16)

    def block(j, carry):
        xslot = lax.rem(j, X_SLOTS)
        yslot = lax.rem(j, Y_SLOTS)
        x_copy(j, xslot).wait()

        ahead = j + (X_SLOTS - 1)

        @pl.when(ahead < nb)
        def _():
            x_copy(ahead, lax.rem(ahead, X_SLOTS)).start(priority=ROW_DMA_PRIORITY)

        @pl.when(j >= Y_SLOTS)
        def _():
            y_copy(j - Y_SLOTS, yslot).wait()

        xbase = xslot * rows
        ybase = yslot * rows
        sub = bm // EXPERT_CHUNKS
        for c in range(EXPERT_CHUNKS):
            cbase = xbase + c * sub * SUBLANES
            obase = ybase + c * sub * SUBLANES
            los, his = [], []
            for s in range(SUBLANES):
                lo, hi = _unpack_word(xbuf_ref[pl.ds(cbase + s, sub, stride=SUBLANES), :], stage_in_ref,
                                      c * SUBLANES + s)
                los.append(lo.astype(BF16))
                his.append(hi.astype(BF16))
            x = jnp.concatenate(los + his, axis=1)
            g = jnp.dot(x, wgb_ref[...], preferred_element_type=F32)
            u = jnp.dot(x, wub_ref[...], preferred_element_type=F32)
            a = (_silu(g) * u).astype(BF16)
            y = jnp.dot(a, wdb_ref[...], preferred_element_type=F32)
            _pack_rows(y, obuf_ref, stage_out_ref, sub, row0=obase, stage0=c * sub * 2 * SUBLANES)
        y_copy(j, yslot).start()
        return carry

    lax.fori_loop(0, nb, block, 0)

    @pl.when(e + 1 < pl.num_programs(0))
    def _():
        nxt = jnp.minimum(e + 1, pl.num_programs(0) - 1)
        start_head(nblk_ref[nxt], first_ref[nxt])

    for back in range(Y_SLOTS, 0, -1):
        @pl.when(nb >= back)
        def _(back=back):
            y_copy(nb - back, lax.rem(nb - back, Y_SLOTS)).wait()

    @pl.when(e == pl.num_programs(0) - 1)
    def _():
        obuf_ref[pl.ds(0, rows), :] = pltpu.bitcast(jnp.zeros((2 * rows, LANES), BF16), I32)

        def clear(j, carry):
            cp = pltpu.make_async_copy(obuf_ref.at[pl.ds(0, rows), :],
                                       yb_hbm.at[pl.ds(pl.multiple_of(j * rows, rows), rows), :], out_sem.at[0])
            cp.start()
            cp.wait()
            return carry

        lax.fori_loop(b0 + nb, yb_hbm.shape[0] // rows, clear, 0)


def _experts(first_block, n_blocks_e, xs, w_gate, w_up, w_down, bm):
    n_exp, d, de = w_gate.shape
    rows = bm * SUBLANES
    grid_spec = pltpu.PrefetchScalarGridSpec(
        num_scalar_prefetch=2,
        grid=(n_exp,),
        in_specs=[pl.BlockSpec(memory_space=pl.ANY),
                  pl.BlockSpec((None, d, de), lambda e, fb, nb: (e, 0, 0)),
                  pl.BlockSpec((None, d, de), lambda e, fb, nb: (e, 0, 0)),
                  pl.BlockSpec((None, de, d), lambda e, fb, nb: (e, 0, 0))],
        out_specs=pl.BlockSpec(memory_space=pl.ANY),
        scratch_shapes=[pltpu.VMEM((X_SLOTS * rows, LANES), I32), pltpu.VMEM((Y_SLOTS * rows, LANES), I32),
                        pltpu.SemaphoreType.DMA((X_SLOTS,)), pltpu.SemaphoreType.DMA((Y_SLOTS,)),
                        pltpu.VMEM((d, de), BF16), pltpu.VMEM((d, de), BF16), pltpu.VMEM((de, d), BF16),
                        pltpu.VMEM((SUBLANES * 2 * bm, LANES), F32), pltpu.VMEM((SUBLANES * 2 * bm, LANES), F32)],
    )
    return pl.pallas_call(
        functools.partial(_expert_kernel, bm=bm),
        grid_spec=grid_spec,
        out_shape=jax.ShapeDtypeStruct(xs.shape, I32),
        compiler_params=_params(1, 56),
        name="routed_experts",
    )(first_block, n_blocks_e, xs, w_gate, w_up, w_down)


def _final_kernel(dest_ref, dnext_ref, w_ref, yb_hbm, x1_ref, h2_ref, gate_ref, gpost_ref, wsg_ref, wsu_ref,
                  wsd_ref, o_ref, buf_ref, sem, stage_ref, *, tiles_per_row):
    tm = x1_ref.shape[0]
    i = pl.program_id(0)
    last = pl.num_programs(0) - 1
    r = i // tiles_per_row
    tile_rows = TOP_K * tm * SUBLANES
    cur = (i % 2) * tile_rows
    nxt = tile_rows - cur

    def request(dref, base, slot, t, k):
        src = pl.multiple_of(dref[0, k, t] * SUBLANES, SUBLANES)
        dst = pl.multiple_of(base + (k * tm + t) * SUBLANES, SUBLANES)
        pltpu.make_async_copy(yb_hbm.at[pl.ds(src, SUBLANES), :], buf_ref.at[pl.ds(dst, SUBLANES), :],
                              sem.at[slot]).start(priority=k % 2)

    def wait_tile(base, slot):
        pltpu.make_async_copy(yb_hbm.at[pl.ds(0, tile_rows), :],
                              buf_ref.at[pl.ds(pl.multiple_of(base, tile_rows), tile_rows), :], sem.at[slot]).wait()

    @pl.when(i == 0)
    def _():
        def first(t, carry):
            for k in range(TOP_K):
                request(dest_ref, 0, 0, t, k)
            return carry
        lax.fori_loop(0, tm, first, 0)

    wait_tile(cur, i % 2)

    pending = [(t, k) for t in range(tm) for k in range(TOP_K)]
    per_piece = 2 * len(pending) // (SUBLANES * TOP_K)

    def request_piece(n):
        for t, k in pending[n * per_piece:(n + 1) * per_piece]:
            request(dnext_ref, nxt, (i + 1) % 2, t, k)

    h = h2_ref[...]
    g = jnp.dot(h, wsg_ref[...], preferred_element_type=F32)
    u = jnp.dot(h, wsu_ref[...], preferred_element_type=F32)
    shared = jnp.dot((_silu(g) * u).astype(BF16), wsd_ref[...], preferred_element_type=F32)

    w = w_ref[...]
    los, his = [], []
    for s in range(SUBLANES):
        lo_acc = jnp.zeros((tm, LANES), F32)
        hi_acc = jnp.zeros((tm, LANES), F32)
        for k in range(TOP_K):
            lo, hi = _unpack_word(buf_ref[pl.ds(cur + k * tm * SUBLANES + s, tm, stride=SUBLANES), :],
                                  stage_ref, s * TOP_K + k)
            wk = w[:, k:k + 1]
            lo_acc = lo_acc + wk * lo
            hi_acc = hi_acc + wk * hi
            request_piece(s * TOP_K + k)
        los.append(lo_acc)
        his.append(hi_acc)
    ffn = jnp.concatenate(los + his, axis=1) + shared
    o_ref[...] = x1_ref[...] + gate_ref[pl.ds(r, 1), :] * _rms(ffn, gpost_ref[...])

    @pl.when(i == last)
    def _():
        wait_tile(nxt, (i + 1) % 2)


def _final(dest, w_tok, yb, x1, h2b, mod, g_post, ws_gate, ws_up, ws_down):
    t, d = x1.shape
    nb = mod.shape[0]
    ds_ = ws_gate.shape[1]
    tm = dest.shape[2]
    n_tiles = t // tm
    row = lambda i: (i, 0)
    fixed = lambda i: (0, 0)
    return pl.pallas_call(
        functools.partial(_final_kernel, tiles_per_row=t // nb // tm),
        grid=(n_tiles,),
        in_specs=[pl.BlockSpec((1, TOP_K, tm), lambda i: (i, 0, 0), memory_space=pltpu.SMEM),
                  pl.BlockSpec((1, TOP_K, tm), lambda i: (jnp.minimum(i + 1, n_tiles - 1), 0, 0),
                               memory_space=pltpu.SMEM),
                  pl.BlockSpec((tm, TOP_K), row),
                  pl.BlockSpec(memory_space=pl.ANY),
                  pl.BlockSpec((tm, d), row), pl.BlockSpec((tm, d), row),
                  pl.BlockSpec((nb, d), lambda i: (0, 5)),
                  pl.BlockSpec((1, d), fixed),
                  pl.BlockSpec((d, ds_), fixed), pl.BlockSpec((d, ds_), fixed), pl.BlockSpec((ds_, d), fixed)],
        out_specs=pl.BlockSpec((tm, d), row),
        out_shape=jax.ShapeDtypeStruct((t, d), F32),
        scratch_shapes=[pltpu.VMEM((2 * TOP_K * tm * SUBLANES, LANES), I32), pltpu.SemaphoreType.DMA((2,)),
                        pltpu.VMEM((SUBLANES * TOP_K * 2 * tm, LANES), F32)],
        compiler_params=_params(1, 48),
        name="combine_final",
    )(dest, dest, w_tok, yb, x1, h2b, mod, g_post, ws_gate, ws_up, ws_down)


def _rope_tables(n_pos):
    half = LANES // 2
    pos = jnp.arange(n_pos)
    row = (pos // GRID_W).astype(F32)
    col = (pos % GRID_W).astype(F32)
    inv = ROPE_THETA ** (-jnp.arange(0, half, 2, dtype=F32) / half)
    ang = jnp.concatenate([row[:, None] * inv, col[:, None] * inv], axis=-1)
    cos = jnp.repeat(jnp.cos(ang), 2, axis=-1)
    sin = jnp.repeat(jnp.sin(ang), 2, axis=-1) * jnp.tile(jnp.array([-1.0, 1.0], F32), half)
    return cos, sin


def _sublayer1(x, mod, p, seq_len, rope, cache, n_heads, n_kv, tq):
    t = x.shape[0]
    h = _prenorm(x, mod, p['g_pre1'], 0, 1)
    z = _matmul_wcast(h, p['w_in'], _z_col_map, 9 * 1024, BF16)
    kv = _matmul_wcast(h, p['w_in'], lambda j: j + 5, 1024, F32)
    qn, kn, vb, k32, v32 = _qk_prep(z, kv, p['g_q'], p['g_k'], rope, seq_len, n_heads, n_kv)
    attn = _attention(qn, kn, vb, cache, t // seq_len, seq_len, n_heads, n_kv, tq)
    cvg = _gated_conv(z, p['conv_w'], p['conv_b'], seq_len, 6)
    merged = _merge(cvg, attn, z, p['w_conv_out'], p['w_attn_out'], 4, 8)
    x1, h2b, h2p, logits_t = _out_proj(merged, x, mod, p['g_post1'], p['g_pre2'], p['w_out_b'], p['w_router_t'])
    return x1, h2b, h2p, logits_t, k32, v32


def kernel(x_prompt, x_sample, cache_k, cache_v, c, c_ctx, w_mod, b_mod, g_pre1, w_in, conv_w, conv_b, g_q, g_k, w_conv_out, w_attn_out, w_out, g_post1, g_pre2, w_router, b_router, w_e_gate, w_e_up, w_e_down, w_s_gate, w_s_up, w_s_down, g_post2):
    batch, seq, d = x_prompt.shape
    dec_batch, dec_seq, _ = x_sample.shape
    depth = w_mod.shape[0]
    assert depth == 1
    past, n_kv, head_dim = cache_k.shape[2:]
    assert head_dim == LANES
    n_heads = w_attn_out.shape[1] // head_dim
    n_exp = w_router.shape[2]
    t_ctx, t_lat = batch * seq, dec_batch * dec_seq
    t_all = t_ctx + t_lat
    l = 0

    p = {
        'g_pre1': g_pre1[l][None], 'w_in': w_in[l], 'conv_w': conv_w[l], 'conv_b': conv_b[l][None],
        'g_q': g_q[l][None], 'g_k': g_k[l][None], 'w_conv_out': w_conv_out[l], 'w_attn_out': w_attn_out[l],
        'w_out_b': w_out[l].astype(BF16), 'g_post1': g_post1[l][None], 'g_pre2': g_pre2[l][None],
        'w_router_t': w_router[l].T,
    }
    cond = jnp.concatenate([c_ctx[None], c, jnp.zeros((SUBLANES - 1 - dec_batch, d), F32)], axis=0)
    mod = _modulation(cond, w_mod[l], b_mod[l][None])
    mod_ctx, mod_lat = mod[0:1], mod[1:1 + dec_batch]

    xc = x_prompt.reshape(t_ctx, d)
    xl = x_sample.reshape(t_lat, d)
    cache = (cache_k[:, l].reshape(dec_batch, past, n_kv * head_dim),
             cache_v[:, l].reshape(dec_batch, past, n_kv * head_dim))
    rope = _rope_tables(dec_seq)

    x1c, h2bc, h2pc, logc, k32, v32 = _sublayer1(xc, mod_ctx, p, seq, None, None, n_heads, n_kv, 256)
    x1l, h2bl, h2pl, logl, _, _ = _sublayer1(xl, mod_lat, p, dec_seq, rope, cache, n_heads, n_kv, 256)
    idx_all, w_all, rank_all, cnt = _router(jnp.concatenate([logc, logl], axis=1), b_router[l][:, None])

    bm = 256
    n_blocks = (t_all * TOP_K + n_exp * (bm - 1)) // bm + 1
    n_slots = n_blocks * bm
    blocks_e = jnp.floor((cnt[:, 0] + (bm - 1)) / bm)
    first_e = jnp.cumsum(blocks_e) - blocks_e
    pstart = (first_e * bm).astype(I32)

    tmf = 128
    w_tok = w_all.T
    dest = _dest_slots(pstart, idx_all, rank_all)
    dest3 = dest.reshape(TOP_K, t_all // tmf, tmf).transpose(1, 0, 2)
    used_end = jnp.sum(blocks_e, keepdims=True) * bm
    pad_lo = jnp.concatenate([first_e * bm + cnt[:, 0], used_end]).astype(I32)
    pad_hi = jnp.concatenate([(first_e + blocks_e) * bm, jnp.full((1,), float(n_slots), F32)]).astype(I32)
    slot_tab = _slot_table(pad_lo, pad_hi, dest3, n_slots)

    h2p = jnp.concatenate([h2pc, h2pl], axis=0)
    xs = _gather_rows(slot_tab, h2p, GATHER_BLOCK)
    yb = _experts(first_e.astype(I32), blocks_e.astype(I32), xs, w_e_gate[l], w_e_up[l], w_e_down[l], bm)

    ws = (w_s_gate[l].astype(BF16), w_s_up[l].astype(BF16), w_s_down[l].astype(BF16))

    def finish(lo, n, x1, h2b, mod_g):
        dest_g = dest3[lo // tmf:(lo + n) // tmf]
        return _final(dest_g, w_tok[lo:lo + n], yb, x1, h2b, mod_g, g_post2[l][None], *ws)

    y_ctx = finish(0, t_ctx, x1c, h2bc, mod_ctx)
    y_lat = finish(t_ctx, t_lat, x1l, h2bl, mod_lat)

    new_k = k32.reshape(batch, 1, seq, n_kv, head_dim)
    new_v = v32.reshape(batch, 1, seq, n_kv, head_dim)
    return (y_ctx.reshape(batch, seq, d), y_lat.reshape(dec_batch, dec_seq, d), new_k, new_v)
```

```python
import functools

import jax
import jax.numpy as jnp
from jax import lax
from jax.experimental import pallas as pl
from jax.experimental.pallas import tpu as pltpu

GRID_W = 64
ROPE_THETA = 10000.0
N_GROUPS = 8
TOPK_GROUPS = 4
TOP_K = 8
TOP_K_LOG2 = 3
ROUTED_SCALE = 2.5
EPS = 1e-6

LANES = 128
SUBLANES = 8
V7X_VMEM_BYTES = 64 * 1024 * 1024
MIB = 1024 * 1024

F32 = jnp.float32
BF16 = jnp.bfloat16
I32 = jnp.int32


def _params(n_grid, vmem_mib):
    assert vmem_mib * MIB < V7X_VMEM_BYTES
    return pltpu.CompilerParams(
        dimension_semantics=("arbitrary",) * n_grid, vmem_limit_bytes=vmem_mib * MIB)


def _silu(x):
    return x * jax.nn.sigmoid(x)


def _rms(x, g):
    return x * lax.rsqrt(jnp.mean(x * x, axis=-1, keepdims=True) + EPS) * g


def _pack_rows(val, out_ref, stage_ref, rows, row0=0, stage0=0):
    half = val.shape[1] // 2
    for s in range(half // LANES):
        base = stage0 + s * 2 * rows
        stage_ref[pl.ds(base, rows, stride=2), :] = val[:, s * LANES:(s + 1) * LANES]
        stage_ref[pl.ds(base + 1, rows, stride=2), :] = val[:, half + s * LANES:half + (s + 1) * LANES]
        pair = stage_ref[pl.ds(base, 2 * rows), :].astype(BF16)
        out_ref[pl.ds(row0 + s, rows, stride=SUBLANES), :] = pltpu.bitcast(pair, I32)


def _unpack_word(word, stage_ref, slot):
    rows = word.shape[0]
    base = slot * 2 * rows
    stage_ref[pl.ds(base, 2 * rows), :] = pltpu.bitcast(word, BF16).astype(F32)
    return stage_ref[pl.ds(base, rows, stride=2), :], stage_ref[pl.ds(base + 1, rows, stride=2), :]


def _mod_kernel(c_ref, w_ref, b_ref, o_ref):
    s = _silu(c_ref[...]).astype(BF16)
    o_ref[...] = jnp.dot(s, w_ref[...].astype(BF16), preferred_element_type=F32) + b_ref[...]


def _modulation(cond, w, b):
    rows, d = cond.shape
    n = w.shape[1]
    tn = 1024
    return pl.pallas_call(
        _mod_kernel,
        grid=(n // tn,),
        in_specs=[pl.BlockSpec((rows, d), lambda j: (0, 0)),
                  pl.BlockSpec((d, tn), lambda j: (0, j)),
                  pl.BlockSpec((1, tn), lambda j: (0, j))],
        out_specs=pl.BlockSpec((rows, tn), lambda j: (0, j)),
        out_shape=jax.ShapeDtypeStruct((rows, n), F32),
        compiler_params=_params(1, 40),
        name="modulation",
    )(cond, w, b)


def _prenorm_kernel(x_ref, shift_ref, scale_ref, g_ref, o_ref, *, tiles_per_row):
    r = pl.program_id(0) // tiles_per_row
    y = _rms(x_ref[...], g_ref[...])
    o_ref[...] = (y * (1.0 + scale_ref[pl.ds(r, 1), :]) + shift_ref[pl.ds(r, 1), :]).astype(o_ref.dtype)


def _prenorm(x, mod, g, shift_col, scale_col):
    t, d = x.shape
    nb = mod.shape[0]
    tm = 512
    return pl.pallas_call(
        functools.partial(_prenorm_kernel, tiles_per_row=t // nb // tm),
        grid=(t // tm,),
        in_specs=[pl.BlockSpec((tm, d), lambda i: (i, 0)),
                  pl.BlockSpec((nb, d), lambda i: (0, shift_col)),
                  pl.BlockSpec((nb, d), lambda i: (0, scale_col)),
                  pl.BlockSpec((1, d), lambda i: (0, 0))],
        out_specs=pl.BlockSpec((tm, d), lambda i: (i, 0)),
        out_shape=jax.ShapeDtypeStruct((t, d), BF16),
        compiler_params=_params(1, 32),
        name="prenorm",
    )(x, mod, mod, g)


def _mm_wcast_kernel(a_ref, w_ref, o_ref, wb_ref):
    @pl.when(pl.program_id(1) == 0)
    def _():
        wb_ref[...] = w_ref[...].astype(BF16)

    o_ref[...] = jnp.dot(a_ref[...], wb_ref[...], preferred_element_type=F32).astype(o_ref.dtype)


def _matmul_wcast(a, w, col_map, n_out, out_dtype):
    m, k = a.shape
    tn = 1024
    tm = 2048 if jnp.dtype(out_dtype).itemsize == 2 else 1024
    return pl.pallas_call(
        _mm_wcast_kernel,
        grid=(n_out // tn, m // tm),
        in_specs=[pl.BlockSpec((tm, k), lambda j, i: (i, 0)),
                  pl.BlockSpec((k, tn), lambda j, i: (0, col_map(j)))],
        out_specs=pl.BlockSpec((tm, tn), lambda j, i: (i, j)),
        out_shape=jax.ShapeDtypeStruct((m, n_out), out_dtype),
        scratch_shapes=[pltpu.VMEM((k, tn), BF16)],
        compiler_params=_params(2, 56),
        name="in_proj",
    )(a, w)


def _z_col_map(j):
    return jnp.where(j < 2, j + 3, jnp.where(j < 6, j + 4, j - 6))


def _qk_kernel(*refs, use_rope, n_heads, n_kv, q_scale):
    if use_rope:
        q_ref, kv_ref, gq_ref, gk_ref, cos_ref, sin_ref, qn_ref, kn_ref, vb_ref, k32_ref, v32_ref = refs
        cos = cos_ref[...]
        sin = sin_ref[...]
        even = lax.broadcasted_iota(I32, cos.shape, 1) % 2 == 0
    else:
        q_ref, kv_ref, gq_ref, gk_ref, qn_ref, kn_ref, vb_ref, k32_ref, v32_ref = refs

    def norm_rope(xh, g):
        y = _rms(xh, g)
        if use_rope:
            sw = jnp.where(even, pltpu.roll(y, LANES - 1, 1), pltpu.roll(y, 1, 1))
            y = y * cos + sw * sin
        return y

    gq = gq_ref[...]
    gk = gk_ref[...]
    for h in range(n_heads):
        sl = slice(h * LANES, (h + 1) * LANES)
        qn_ref[:, sl] = (norm_rope(q_ref[:, sl].astype(F32), gq) * q_scale).astype(BF16)
    kw = n_kv * LANES
    for h in range(n_kv):
        sl = slice(h * LANES, (h + 1) * LANES)
        kh = norm_rope(kv_ref[:, sl], gk)
        k32_ref[:, sl] = kh
        kn_ref[:, sl] = kh.astype(BF16)
    v = kv_ref[:, kw:2 * kw]
    v32_ref[...] = v
    vb_ref[...] = v.astype(BF16)


def _qk_prep(z, kv, gq, gk, rope, seq_len, n_heads, n_kv):
    t = z.shape[0]
    dq = n_heads * LANES
    dk = n_kv * LANES
    tm = 256
    in_specs = [pl.BlockSpec((tm, dq), lambda i: (i, 0)),
                pl.BlockSpec((tm, 2 * dk), lambda i: (i, 0)),
                pl.BlockSpec((1, LANES), lambda i: (0, 0)),
                pl.BlockSpec((1, LANES), lambda i: (0, 0))]
    args = [z, kv, gq, gk]
    if rope is not None:
        per_seq = seq_len // tm
        in_specs += [pl.BlockSpec((tm, LANES), lambda i: (i % per_seq, 0))] * 2
        args += list(rope)
    out_specs = [pl.BlockSpec((tm, dq), lambda i: (i, 0))] + [pl.BlockSpec((tm, dk), lambda i: (i, 0))] * 4
    out_shape = [jax.ShapeDtypeStruct((t, dq), BF16), jax.ShapeDtypeStruct((t, dk), BF16),
                 jax.ShapeDtypeStruct((t, dk), BF16), jax.ShapeDtypeStruct((t, dk), F32),
                 jax.ShapeDtypeStruct((t, dk), F32)]
    return pl.pallas_call(
        functools.partial(_qk_kernel, use_rope=rope is not None, n_heads=n_heads, n_kv=n_kv,
                          q_scale=LANES ** -0.5),
        grid=(t // tm,),
        in_specs=in_specs, out_specs=out_specs, out_shape=out_shape,
        compiler_params=_params(1, 32),
        name="qk_prep",
    )(*args)


def _attn_kernel(*refs, has_cache, group, chunks):
    if has_cache:
        q_ref, k_ref, v_ref, ck_ref, cv_ref, o_ref = refs
    else:
        q_ref, k_ref, v_ref, o_ref = refs
    tq = q_ref.shape[0]
    nt = (((1,), (1,)), ((), ()))
    if has_cache:
        ck = ck_ref[...].astype(BF16)
        cv = cv_ref[...].astype(BF16)
    per = group // chunks
    for c in range(chunks):
        heads = range(c * per, (c + 1) * per)
        q = jnp.concatenate([q_ref[:, g * LANES:(g + 1) * LANES] for g in heads], axis=0)
        s_own = lax.dot_general(q, k_ref[...], nt, preferred_element_type=F32)
        m = jnp.max(s_own, axis=-1, keepdims=True)
        if has_cache:
            s_ctx = lax.dot_general(q, ck, nt, preferred_element_type=F32)
            m = jnp.maximum(m, jnp.max(s_ctx, axis=-1, keepdims=True))
        p = jnp.exp(s_own - m)
        denom = jnp.sum(p, axis=-1, keepdims=True)
        acc = jnp.dot(p.astype(BF16), v_ref[...], preferred_element_type=F32)
        if has_cache:
            pc = jnp.exp(s_ctx - m)
            denom = denom + jnp.sum(pc, axis=-1, keepdims=True)
            acc = acc + jnp.dot(pc.astype(BF16), cv, preferred_element_type=F32)
        o = acc / denom
        for n, g in enumerate(heads):
            o_ref[:, g * LANES:(g + 1) * LANES] = o[n * tq:(n + 1) * tq].astype(o_ref.dtype)


def _attention(qn, kn, vb, cache, batch, seq_len, n_heads, n_kv, tq):
    t = qn.shape[0]
    group = n_heads // n_kv
    nq = seq_len // tq
    in_specs = [pl.BlockSpec((tq, group * LANES), lambda b, h, i: (b * nq + i, h)),
                pl.BlockSpec((seq_len, LANES), lambda b, h, i: (b, h)),
                pl.BlockSpec((seq_len, LANES), lambda b, h, i: (b, h))]
    args = [qn, kn, vb]
    if cache is not None:
        past = cache[0].shape[1]
        in_specs += [pl.BlockSpec((None, past, LANES), lambda b, h, i: (b, 0, h))] * 2
        args += list(cache)
    return pl.pallas_call(
        functools.partial(_attn_kernel, has_cache=cache is not None, group=group,
                          chunks=group if cache is not None else group // 2),
        grid=(batch, n_kv, nq),
        in_specs=in_specs,
        out_specs=pl.BlockSpec((tq, group * LANES), lambda b, h, i: (b * nq + i, h)),
        out_shape=jax.ShapeDtypeStruct((t, n_heads * LANES), BF16),
        compiler_params=_params(3, 40),
        name="attention",
    )(*args)


CONV_HALO = 16


def _conv_kernel(u_ref, b_ref, c_ref, up_ref, cp_ref, un_ref, cn_ref, w_ref, bias_ref, o_ref, *,
                 tiles_per_seq):
    tm = u_ref.shape[0]
    pos = pl.program_id(0) % tiles_per_seq
    cu = c_ref[...].astype(F32) * u_ref[...].astype(F32)
    halo_prev = (cp_ref[...].astype(F32) * up_ref[...].astype(F32))[CONV_HALO - 1:CONV_HALO, :]
    halo_next = (cn_ref[...].astype(F32) * un_ref[...].astype(F32))[0:1, :]
    halo_prev = jnp.where(pos == 0, 0.0, halo_prev)
    halo_next = jnp.where(pos == tiles_per_seq - 1, 0.0, halo_next)
    row = lax.broadcasted_iota(I32, cu.shape, 0)
    prev = jnp.where(row == 0, halo_prev, pltpu.roll(cu, 1, 0))
    nxt = jnp.where(row == tm - 1, halo_next, pltpu.roll(cu, tm - 1, 0))
    w = w_ref[...]
    conv = prev * w[0:1, :] + cu * w[1:2, :] + nxt * w[2:3, :] + bias_ref[...]
    o_ref[...] = (b_ref[...].astype(F32) * conv).astype(o_ref.dtype)


def _gated_conv(z, conv_w, conv_b, seq_len, col0):
    t = z.shape[0]
    dc = conv_w.shape[1]
    tm = 256
    hb = tm // CONV_HALO
    last = t // CONV_HALO - 1
    prev_map = lambda c: (lambda i: (jnp.maximum(i * hb - 1, 0), c))
    next_map = lambda c: (lambda i: (jnp.minimum((i + 1) * hb, last), c))
    in_specs = [pl.BlockSpec((tm, dc), lambda i: (i, col0)),
                pl.BlockSpec((tm, dc), lambda i: (i, col0 + 1)),
                pl.BlockSpec((tm, dc), lambda i: (i, col0 + 2)),
                pl.BlockSpec((CONV_HALO, dc), prev_map(col0)),
                pl.BlockSpec((CONV_HALO, dc), prev_map(col0 + 2)),
                pl.BlockSpec((CONV_HALO, dc), next_map(col0)),
                pl.BlockSpec((CONV_HALO, dc), next_map(col0 + 2)),
                pl.BlockSpec(conv_w.shape, lambda i: (0, 0)),
                pl.BlockSpec((1, dc), lambda i: (0, 0))]
    return pl.pallas_call(
        functools.partial(_conv_kernel, tiles_per_seq=seq_len // tm),
        grid=(t // tm,),
        in_specs=in_specs,
        out_specs=pl.BlockSpec((tm, dc), lambda i: (i, 0)),
        out_shape=jax.ShapeDtypeStruct((t, dc), BF16),
        compiler_params=_params(1, 32),
        name="gated_conv",
    )(z, z, z, z, z, z, z, conv_w, conv_b)


def _merge_kernel(cv_ref, at_ref, gc_ref, ga_ref, wc_ref, wa_ref, o_ref, wcb_ref, wab_ref):
    @pl.when(pl.program_id(1) == 0)
    def _():
        wcb_ref[...] = wc_ref[...].astype(BF16)
        wab_ref[...] = wa_ref[...].astype(BF16)

    conv_out = jnp.dot(cv_ref[...], wcb_ref[...], preferred_element_type=F32)
    attn_out = jnp.dot(at_ref[...], wab_ref[...], preferred_element_type=F32)
    merged = (jax.nn.sigmoid(gc_ref[...].astype(F32)) * conv_out
              + jax.nn.sigmoid(ga_ref[...].astype(F32)) * attn_out)
    o_ref[...] = merged.astype(o_ref.dtype)


def _merge(cvg, attn, z, w_conv_out, w_attn_out, gc_col0, ga_col0):
    t, dc = cvg.shape
    dq = attn.shape[1]
    d = w_conv_out.shape[1]
    tm, tn = 512, 512
    return pl.pallas_call(
        _merge_kernel,
        grid=(d // tn, t // tm),
        in_specs=[pl.BlockSpec((tm, dc), lambda j, i: (i, 0)),
                  pl.BlockSpec((tm, dq), lambda j, i: (i, 0)),
                  pl.BlockSpec((tm, tn), lambda j, i: (i, gc_col0 + j)),
                  pl.BlockSpec((tm, tn), lambda j, i: (i, ga_col0 + j)),
                  pl.BlockSpec((dc, tn), lambda j, i: (0, j)),
                  pl.BlockSpec((dq, tn), lambda j, i: (0, j))],
        out_specs=pl.BlockSpec((tm, tn), lambda j, i: (i, j)),
        out_shape=jax.ShapeDtypeStruct((t, d), BF16),
        scratch_shapes=[pltpu.VMEM((dc, tn), BF16), pltpu.VMEM((dq, tn), BF16)],
        compiler_params=_params(2, 40),
        name="merge",
    )(cvg, attn, z, z, w_conv_out, w_attn_out)


def _route(logits_t, bias_col, carry):
    n_exp, tm = logits_t.shape
    per = n_exp // N_GROUPS
    assert per == SUBLANES
    neg = -jnp.inf
    scores = jax.nn.sigmoid(logits_t)
    biased = scores + bias_col
    sub = lax.broadcasted_iota(I32, (per, tm), 0).astype(F32)
    xs = [biased[g * per:(g + 1) * per, :] for g in range(N_GROUPS)]
    sc = [scores[g * per:(g + 1) * per, :] for g in range(N_GROUPS)]
    ids = [sub + float(g * per) for g in range(N_GROUPS)]

    def colmax(a):
        return jnp.max(a, axis=0, keepdims=True)

    def colmin(a):
        return jnp.min(a, axis=0, keepdims=True)

    rows = []
    for g in range(N_GROUPS):
        m1 = colmax(xs[g])
        j1 = colmin(jnp.where(xs[g] == m1, sub, float(per)))
        m2 = colmax(jnp.where(sub == j1, neg, xs[g]))
        rows.append(m1 + m2)
    gs = jnp.concatenate(rows, axis=0)
    gsel = jnp.zeros_like(gs)
    for _ in range(TOPK_GROUPS):
        m = colmax(gs)
        j = colmin(jnp.where(gs == m, sub, float(N_GROUPS)))
        hit = sub == j
        gsel = jnp.where(hit, 1.0, gsel)
        gs = jnp.where(hit, neg, gs)
    masked = [jnp.where(gsel[g:g + 1, :] > 0.0, xs[g], neg) for g in range(N_GROUPS)]
    idx_rows, w_rows = [], []
    member = [jnp.zeros((per, tm), F32) for _ in range(N_GROUPS)]
    for _ in range(TOP_K):
        mm = masked[0]
        for g in range(1, N_GROUPS):
            mm = jnp.maximum(mm, masked[g])
        m = colmax(mm)
        idx = colmin(jnp.where(masked[0] == m, ids[0], float(n_exp)))
        for g in range(1, N_GROUPS):
            idx = jnp.minimum(idx, colmin(jnp.where(masked[g] == m, ids[g], float(n_exp))))
        wk = jnp.zeros_like(idx)
        for g in range(N_GROUPS):
            hit = ids[g] == idx
            wk = wk + jnp.sum(jnp.where(hit, sc[g], 0.0), axis=0, keepdims=True)
            masked[g] = jnp.where(hit, neg, masked[g])
            member[g] = jnp.where(hit, 1.0, member[g])
        idx_rows.append(idx)
        w_rows.append(wk)
    w = jnp.concatenate(w_rows, axis=0)
    w = w / jnp.sum(w, axis=0, keepdims=True) * ROUTED_SCALE

    earlier = (lax.broadcasted_iota(I32, (tm, tm), 0) < lax.broadcasted_iota(I32, (tm, tm), 1)).astype(BF16)
    before = jnp.dot(jnp.concatenate(member, axis=0).astype(BF16), earlier, preferred_element_type=F32)
    rank_rows = []
    for k in range(TOP_K):
        rk = jnp.zeros_like(idx_rows[k])
        for g in range(N_GROUPS):
            pos = before[g * per:(g + 1) * per, :] + carry[g]
            rk = rk + jnp.sum(jnp.where(ids[g] == idx_rows[k], pos, 0.0), axis=0, keepdims=True)
        rank_rows.append(rk)
    new_carry = [carry[g] + jnp.sum(member[g], axis=1, keepdims=True) for g in range(N_GROUPS)]
    idx = jnp.concatenate(idx_rows, axis=0).astype(I32)
    rank = jnp.concatenate(rank_rows, axis=0).astype(I32)
    return idx, w, rank, new_carry


def _out_kernel(mg_ref, x_ref, gate_ref, shift_ref, scale_ref, gpost_ref, gpre_ref, wo_ref, wr_ref,
                x1_ref, h2b_ref, h2p_ref, logit_ref, stage_ref, *, tiles_per_row):
    tm = x_ref.shape[0]
    r = pl.program_id(0) // tiles_per_row
    gate = gate_ref[pl.ds(r, 1), :]
    scale = 1.0 + scale_ref[pl.ds(r, 1), :]
    shift = shift_ref[pl.ds(r, 1), :]
    rows = tm // OUT_CHUNKS
    for c in range(OUT_CHUNKS):
        sl = pl.ds(c * rows, rows)
        mix = jnp.dot(mg_ref[sl, :], wo_ref[...], preferred_element_type=F32)
        x1 = x_ref[sl, :] + gate * _rms(mix, gpost_ref[...])
        x1_ref[sl, :] = x1
        h2 = _rms(x1, gpre_ref[...]) * scale + shift
        h2b_ref[sl, :] = h2.astype(BF16)
        _pack_rows(h2, h2p_ref, stage_ref, rows, row0=c * rows * SUBLANES, stage0=c * rows * 2 * SUBLANES)
        logit_ref[:, sl] = lax.dot_general(wr_ref[...], h2, (((1,), (1,)), ((), ())),
                                           preferred_element_type=F32, precision=lax.Precision.HIGHEST)


OUT_CHUNKS = 4


def _out_proj(merged, x, mod, g_post, g_pre, w_out_b, w_router_t):
    t, d = x.shape
    nb = mod.shape[0]
    n_exp = w_router_t.shape[0]
    tm = 512
    row = lambda i: (i, 0)
    fixed = lambda i: (0, 0)
    once = pl.Buffered(1)
    in_specs = [pl.BlockSpec((tm, d), row), pl.BlockSpec((tm, d), row),
                pl.BlockSpec((nb, d), lambda i: (0, 2)),
                pl.BlockSpec((nb, d), lambda i: (0, 3)),
                pl.BlockSpec((nb, d), lambda i: (0, 4)),
                pl.BlockSpec((1, d), fixed), pl.BlockSpec((1, d), fixed),
                pl.BlockSpec((d, d), fixed, pipeline_mode=once),
                pl.BlockSpec((n_exp, d), fixed, pipeline_mode=once)]
    out_specs = [pl.BlockSpec((tm, d), row), pl.BlockSpec((tm, d), row),
                 pl.BlockSpec((tm * SUBLANES, LANES), row),
                 pl.BlockSpec((n_exp, tm), lambda i: (0, i))]
    out_shape = [jax.ShapeDtypeStruct((t, d), F32), jax.ShapeDtypeStruct((t, d), BF16),
                 jax.ShapeDtypeStruct((t * SUBLANES, LANES), I32),
                 jax.ShapeDtypeStruct((n_exp, t), F32)]
    return pl.pallas_call(
        functools.partial(_out_kernel, tiles_per_row=t // nb // tm),
        grid=(t // tm,),
        in_specs=in_specs, out_specs=out_specs, out_shape=out_shape,
        scratch_shapes=[pltpu.VMEM((SUBLANES * 2 * tm, LANES), F32)],
        compiler_params=_params(1, 56),
        name="out_proj",
    )(merged, x, mod, mod, mod, g_post, g_pre, w_out_b, w_router_t)


def _router_kernel(logit_ref, br_ref, idx_ref, wsel_ref, rank_ref, cnt_ref):
    per = SUBLANES

    @pl.when(pl.program_id(0) == 0)
    def _():
        cnt_ref[...] = jnp.zeros_like(cnt_ref)

    carry = [cnt_ref[g * per:(g + 1) * per, 0:1] for g in range(N_GROUPS)]
    idx, w, rank, carry = _route(logit_ref[...], br_ref[...], carry)
    idx_ref[...] = idx
    wsel_ref[...] = w
    rank_ref[...] = rank
    for g in range(N_GROUPS):
        cnt_ref[g * per:(g + 1) * per, :] = jnp.broadcast_to(carry[g], (per, LANES))


def _router(logits_t, b_router_col):
    n_exp, t = logits_t.shape
    tr = 1024
    tile = lambda i: (0, i)
    fixed = lambda i: (0, 0)
    return pl.pallas_call(
        _router_kernel,
        grid=(t // tr,),
        in_specs=[pl.BlockSpec((n_exp, tr), tile), pl.BlockSpec((n_exp, 1), fixed)],
        out_specs=[pl.BlockSpec((TOP_K, tr), tile), pl.BlockSpec((TOP_K, tr), tile),
                   pl.BlockSpec((TOP_K, tr), tile), pl.BlockSpec((n_exp, LANES), fixed)],
        out_shape=[jax.ShapeDtypeStruct((TOP_K, t), I32), jax.ShapeDtypeStruct((TOP_K, t), F32),
                   jax.ShapeDtypeStruct((TOP_K, t), I32), jax.ShapeDtypeStruct((n_exp, LANES), F32)],
        compiler_params=_params(1, 32),
        name="router",
    )(logits_t, b_router_col)


def _dest_kernel(pstart_ref, idx_ref, rank_ref, o_ref):
    idx = idx_ref[...]
    acc = rank_ref[...]
    for e in range(pstart_ref.shape[0]):
        acc = acc + jnp.where(idx == e, pstart_ref[e], 0)
    o_ref[...] = acc


def _dest_slots(pstart, idx, rank):
    return pl.pallas_call(
        _dest_kernel,
        in_specs=[pl.BlockSpec(memory_space=pltpu.SMEM),
                  pl.BlockSpec(memory_space=pltpu.VMEM), pl.BlockSpec(memory_space=pltpu.VMEM)],
        out_specs=pl.BlockSpec(memory_space=pltpu.VMEM),
        out_shape=jax.ShapeDtypeStruct(idx.shape, I32),
        name="dest_slots",
    )(pstart, idx, rank)


SLOT_TILES_PER_STEP = 8


def _slot_table_kernel(pad_lo_ref, pad_hi_ref, dest_ref, tab_ref):
    i = pl.program_id(0)
    tm = dest_ref.shape[2]

    @pl.when(i == 0)
    def _():
        def clear_range(e, carry):
            def clear(s, c):
                tab_ref[s] = 0
                return c
            return lax.fori_loop(pad_lo_ref[e], pad_hi_ref[e], clear, carry)
        lax.fori_loop(0, pad_lo_ref.shape[0], clear_range, 0)

    tiles = dest_ref.shape[0]

    def fill_tile(sub, carry):
        tok0 = (i * tiles + sub) * tm

        def fill(t, c):
            for k in range(TOP_K):
                tab_ref[dest_ref[sub, k, t]] = (tok0 + t) * TOP_K + k
            return c

        return lax.fori_loop(0, tm, fill, carry)

    lax.fori_loop(0, tiles, fill_tile, 0)


def _slot_table(pad_lo, pad_hi, dest3, n_slots):
    nt, _, tm = dest3.shape
    grid_spec = pltpu.PrefetchScalarGridSpec(
        num_scalar_prefetch=2,
        grid=(nt // SLOT_TILES_PER_STEP,),
        in_specs=[pl.BlockSpec((SLOT_TILES_PER_STEP, TOP_K, tm), lambda i, lo, hi: (i, 0, 0),
                               memory_space=pltpu.SMEM)],
        out_specs=pl.BlockSpec(memory_space=pltpu.SMEM),
    )
    return pl.pallas_call(
        _slot_table_kernel,
        grid_spec=grid_spec,
        out_shape=jax.ShapeDtypeStruct((n_slots,), I32),
        compiler_params=_params(1, 16),
        name="slot_table",
    )(pad_lo, pad_hi, dest3)


GATHER_UNROLL = 8
GATHER_BLOCK = 1024


def _gather_kernel(tab_ref, src_ref, o_ref):
    n = tab_ref.shape[2]

    def move(c, carry):
        for u in range(GATHER_UNROLL):
            r = c * GATHER_UNROLL + u
            tok = lax.shift_right_logical(tab_ref[0, 0, r], TOP_K_LOG2)
            src = pl.multiple_of(tok * SUBLANES, SUBLANES)
            dst = pl.multiple_of(r * SUBLANES, SUBLANES)
            o_ref[pl.ds(dst, SUBLANES), :] = src_ref[pl.ds(src, SUBLANES), :]
        return carry

    lax.fori_loop(0, n // GATHER_UNROLL, move, 0)


def _gather_rows(slot_tab, h2p, gb):
    n_slots = slot_tab.shape[0]
    nblk = n_slots // gb
    resident = h2p.size * h2p.dtype.itemsize
    return pl.pallas_call(
        _gather_kernel,
        grid=(nblk,),
        in_specs=[pl.BlockSpec((1, 1, gb), lambda b: (b, 0, 0), memory_space=pltpu.SMEM),
                  pl.BlockSpec(memory_space=pltpu.VMEM)],
        out_specs=pl.BlockSpec((gb * SUBLANES, LANES), lambda b: (b, 0)),
        out_shape=jax.ShapeDtypeStruct((n_slots * SUBLANES, LANES), I32),
        compiler_params=_params(1, resident // MIB + 8),
        name="dispatch_gather",
    )(slot_tab.reshape(nblk, 1, gb), h2p)


ROW_DMA_PRIORITY = 1
EXPERT_CHUNKS = 1
X_SLOTS = 6
Y_SLOTS = 3


def _expert_kernel(first_ref, nblk_ref, xs_hbm, wg_ref, wu_ref, wd_ref, yb_hbm,
                   xbuf_ref, obuf_ref, in_sem, out_sem, wgb_ref, wub_ref, wdb_ref, stage_in_ref, stage_out_ref,
                   *, bm):
    e = pl.program_id(0)
    nb = nblk_ref[e]
    b0 = first_ref[e]
    rows = bm * SUBLANES

    def x_copy(j, slot, base=None):
        blk = (b0 if base is None else base) + j
        return pltpu.make_async_copy(
            xs_hbm.at[pl.ds(pl.multiple_of(blk * rows, rows), rows), :],
            xbuf_ref.at[pl.ds(pl.multiple_of(slot * rows, rows), rows), :], in_sem.at[slot])

    def y_copy(j, slot):
        return pltpu.make_async_copy(
            obuf_ref.at[pl.ds(pl.multiple_of(slot * rows, rows), rows), :],
            yb_hbm.at[pl.ds(pl.multiple_of((b0 + j) * rows, rows), rows), :], out_sem.at[slot])

    def start_head(count, base=None):
        for j in range(X_SLOTS - 1):
            @pl.when(count > j)
            def _(j=j):
                x_copy(j, j, base).start(priority=ROW_DMA_PRIORITY)

    @pl.when(e == 0)
    def _():
        start_head(nb)

    @pl.when(nb > 0)
    def _():
        wgb_ref[...] = wg_ref[...].astype(BF16)
        wub_ref[...] = wu_ref[...].astype(BF16)
        wdb_ref[...] = wd_ref[...].astype(BF16)

    def block(j, carry):
        xslot = lax.rem(j, X_SLOTS)
        yslot = lax.rem(j, Y_SLOTS)
        x_copy(j, xslot).wait()

        ahead = j + (X_SLOTS - 1)

        @pl.when(ahead < nb)
        def _():
            x_copy(ahead, lax.rem(ahead, X_SLOTS)).start(priority=ROW_DMA_PRIORITY)

        @pl.when(j >= Y_SLOTS)
        def _():
            y_copy(j - Y_SLOTS, yslot).wait()

        xbase = xslot * rows
        ybase = yslot * rows
        sub = bm // EXPERT_CHUNKS
        for c in range(EXPERT_CHUNKS):
            cbase = xbase + c * sub * SUBLANES
            obase = ybase + c * sub * SUBLANES
            los, his = [], []
            for s in range(SUBLANES):
                lo, hi = _unpack_word(xbuf_ref[pl.ds(cbase + s, sub, stride=SUBLANES), :], stage_in_ref,
                                      c * SUBLANES + s)
                los.append(lo.astype(BF16))
                his.append(hi.astype(BF16))
            x = jnp.concatenate(los + his, axis=1)
            g = jnp.dot(x, wgb_ref[...], preferred_element_type=F32)
            u = jnp.dot(x, wub_ref[...], preferred_element_type=F32)
            a = (_silu(g) * u).astype(BF16)
            y = jnp.dot(a, wdb_ref[...], preferred_element_type=F32)
            _pack_rows(y, obuf_ref, stage_out_ref, sub, row0=obase, stage0=c * sub * 2 * SUBLANES)
        y_copy(j, yslot).start()
        return carry

    lax.fori_loop(0, nb, block, 0)

    @pl.when(e + 1 < pl.num_programs(0))
    def _():
        nxt = jnp.minimum(e + 1, pl.num_programs(0) - 1)
        start_head(nblk_ref[nxt], first_ref[nxt])

    for back in range(Y_SLOTS, 0, -1):
        @pl.when(nb >= back)
        def _(back=back):
            y_copy(nb - back, lax.rem(nb - back, Y_SLOTS)).wait()

    @pl.when(e == pl.num_programs(0) - 1)
    def _():
        obuf_ref[pl.ds(0, rows), :] = pltpu.bitcast(jnp.zeros((2 * rows, LANES), BF16), I32)

        def clear(j, carry):
            cp = pltpu.make_async_copy(obuf_ref.at[pl.ds(0, rows), :],
                                       yb_hbm.at[pl.ds(pl.multiple_of(j * rows, rows), rows), :], out_sem.at[0])
            cp.start()
            cp.wait()
            return carry

        lax.fori_loop(b0 + nb, yb_hbm.shape[0] // rows, clear, 0)


def _experts(first_block, n_blocks_e, xs, w_gate, w_up, w_down, bm):
    n_exp, d, de = w_gate.shape
    rows = bm * SUBLANES
    grid_spec = pltpu.PrefetchScalarGridSpec(
        num_scalar_prefetch=2,
        grid=(n_exp,),
        in_specs=[pl.BlockSpec(memory_space=pl.ANY),
                  pl.BlockSpec((None, d, de), lambda e, fb, nb: (e, 0, 0)),
                  pl.BlockSpec((None, d, de), lambda e, fb, nb: (e, 0, 0)),
                  pl.BlockSpec((None, de, d), lambda e, fb, nb: (e, 0, 0))],
        out_specs=pl.BlockSpec(memory_space=pl.ANY),
        scratch_shapes=[pltpu.VMEM((X_SLOTS * rows, LANES), I32), pltpu.VMEM((Y_SLOTS * rows, LANES), I32),
                        pltpu.SemaphoreType.DMA((X_SLOTS,)), pltpu.SemaphoreType.DMA((Y_SLOTS,)),
                        pltpu.VMEM((d, de), BF16), pltpu.VMEM((d, de), BF16), pltpu.VMEM((de, d), BF16),
                        pltpu.VMEM((SUBLANES * 2 * bm, LANES), F32), pltpu.VMEM((SUBLANES * 2 * bm, LANES), F32)],
    )
    return pl.pallas_call(
        functools.partial(_expert_kernel, bm=bm),
        grid_spec=grid_spec,
        out_shape=jax.ShapeDtypeStruct(xs.shape, I32),
        compiler_params=_params(1, 56),
        name="routed_experts",
    )(first_block, n_blocks_e, xs, w_gate, w_up, w_down)


def _final_kernel(dest_ref, w_ref, yb_hbm, x1_ref, h2_ref, gate_ref, gpost_ref, wsg_ref, wsu_ref,
                  wsd_ref, o_ref, buf_ref, sem, stage_ref, *, tiles_per_row):
    tm = x1_ref.shape[0]
    i = pl.program_id(0)
    last = pl.num_programs(0) - 1
    r = i // tiles_per_row
    tile_rows = TOP_K * tm * SUBLANES
    cur = (i % 2) * tile_rows
    nxt = tile_rows - cur
    i_next = jnp.minimum(i + 1, last)

    def request(tile, base, slot, t, k):
        src = pl.multiple_of(dest_ref[tile, k, t] * SUBLANES, SUBLANES)
        dst = pl.multiple_of(base + (k * tm + t) * SUBLANES, SUBLANES)
        pltpu.make_async_copy(yb_hbm.at[pl.ds(src, SUBLANES), :], buf_ref.at[pl.ds(dst, SUBLANES), :],
                              sem.at[slot]).start(priority=k % 2)

    def wait_tile(base, slot):
        pltpu.make_async_copy(yb_hbm.at[pl.ds(0, tile_rows), :],
                              buf_ref.at[pl.ds(pl.multiple_of(base, tile_rows), tile_rows), :], sem.at[slot]).wait()

    @pl.when(i == 0)
    def _():
        def first(t, carry):
            for k in range(TOP_K):
                request(0, 0, 0, t, k)
            return carry
        lax.fori_loop(0, tm, first, 0)

    wait_tile(cur, i % 2)

    pending = [(t, k) for t in range(tm) for k in range(TOP_K)]
    per_piece = 2 * len(pending) // (SUBLANES * TOP_K)

    def request_piece(n):
        for t, k in pending[n * per_piece:(n + 1) * per_piece]:
            request(i_next, nxt, (i + 1) % 2, t, k)

    h = h2_ref[...]
    g = jnp.dot(h, wsg_ref[...], preferred_element_type=F32)
    u = jnp.dot(h, wsu_ref[...], preferred_element_type=F32)
    shared = jnp.dot((_silu(g) * u).astype(BF16), wsd_ref[...], preferred_element_type=F32)

    w = w_ref[...]
    los, his = [], []
    for s in range(SUBLANES):
        lo_acc = jnp.zeros((tm, LANES), F32)
        hi_acc = jnp.zeros((tm, LANES), F32)
        for k in range(TOP_K):
            lo, hi = _unpack_word(buf_ref[pl.ds(cur + k * tm * SUBLANES + s, tm, stride=SUBLANES), :],
                                  stage_ref, s * TOP_K + k)
            wk = w[:, k:k + 1]
            lo_acc = lo_acc + wk * lo
            hi_acc = hi_acc + wk * hi
            request_piece(s * TOP_K + k)
        los.append(lo_acc)
        his.append(hi_acc)
    ffn = jnp.concatenate(los + his, axis=1) + shared
    o_ref[...] = x1_ref[...] + gate_ref[pl.ds(r, 1), :] * _rms(ffn, gpost_ref[...])

    @pl.when(i == last)
    def _():
        wait_tile(nxt, (i + 1) % 2)


def _final(dest, w_tok, yb, x1, h2b, mod, g_post, ws_gate, ws_up, ws_down):
    t, d = x1.shape
    nb = mod.shape[0]
    ds_ = ws_gate.shape[1]
    tm = dest.shape[2]
    n_tiles = t // tm
    row = lambda i, dst: (i, 0)
    fixed = lambda i, dst: (0, 0)
    grid_spec = pltpu.PrefetchScalarGridSpec(
        num_scalar_prefetch=1,
        grid=(n_tiles,),
        in_specs=[pl.BlockSpec((tm, TOP_K), row),
                  pl.BlockSpec(memory_space=pl.ANY),
                  pl.BlockSpec((tm, d), row), pl.BlockSpec((tm, d), row),
                  pl.BlockSpec((nb, d), lambda i, dst: (0, 5)),
                  pl.BlockSpec((1, d), fixed),
                  pl.BlockSpec((d, ds_), fixed), pl.BlockSpec((d, ds_), fixed), pl.BlockSpec((ds_, d), fixed)],
        out_specs=pl.BlockSpec((tm, d), row),
        scratch_shapes=[pltpu.VMEM((2 * TOP_K * tm * SUBLANES, LANES), I32), pltpu.SemaphoreType.DMA((2,)),
                        pltpu.VMEM((SUBLANES * TOP_K * 2 * tm, LANES), F32)],
    )
    return pl.pallas_call(
        functools.partial(_final_kernel, tiles_per_row=t // nb // tm),
        grid_spec=grid_spec,
        out_shape=jax.ShapeDtypeStruct((t, d), F32),
        compiler_params=_params(1, 48),
        name="combine_final",
    )(dest, w_tok, yb, x1, h2b, mod, g_post, ws_gate, ws_up, ws_down)


def _rope_tables(n_pos):
    half = LANES // 2
    pos = jnp.arange(n_pos)
    row = (pos // GRID_W).astype(F32)
    col = (pos % GRID_W).astype(F32)
    inv = ROPE_THETA ** (-jnp.arange(0, half, 2, dtype=F32) / half)
    ang = jnp.concatenate([row[:, None] * inv, col[:, None] * inv], axis=-1)
    cos = jnp.repeat(jnp.cos(ang), 2, axis=-1)
    sin = jnp.repeat(jnp.sin(ang), 2, axis=-1) * jnp.tile(jnp.array([-1.0, 1.0], F32), half)
    return cos, sin


def _sublayer1(x, mod, p, seq_len, rope, cache, n_heads, n_kv, tq):
    t = x.shape[0]
    h = _prenorm(x, mod, p['g_pre1'], 0, 1)
    z = _matmul_wcast(h, p['w_in'], _z_col_map, 9 * 1024, BF16)
    kv = _matmul_wcast(h, p['w_in'], lambda j: j + 5, 1024, F32)
    qn, kn, vb, k32, v32 = _qk_prep(z, kv, p['g_q'], p['g_k'], rope, seq_len, n_heads, n_kv)
    attn = _attention(qn, kn, vb, cache, t // seq_len, seq_len, n_heads, n_kv, tq)
    cvg = _gated_conv(z, p['conv_w'], p['conv_b'], seq_len, 6)
    merged = _merge(cvg, attn, z, p['w_conv_out'], p['w_attn_out'], 4, 8)
    x1, h2b, h2p, logits_t = _out_proj(merged, x, mod, p['g_post1'], p['g_pre2'], p['w_out_b'], p['w_router_t'])
    return x1, h2b, h2p, logits_t, k32, v32


def kernel(x_prompt, x_sample, cache_k, cache_v, c, c_ctx, w_mod, b_mod, g_pre1, w_in, conv_w, conv_b, g_q, g_k, w_conv_out, w_attn_out, w_out, g_post1, g_pre2, w_router, b_router, w_e_gate, w_e_up, w_e_down, w_s_gate, w_s_up, w_s_down, g_post2):
    batch, seq, d = x_prompt.shape
    dec_batch, dec_seq, _ = x_sample.shape
    depth = w_mod.shape[0]
    assert depth == 1
    past, n_kv, head_dim = cache_k.shape[2:]
    assert head_dim == LANES
    n_heads = w_attn_out.shape[1] // head_dim
    n_exp = w_router.shape[2]
    t_ctx, t_lat = batch * seq, dec_batch * dec_seq
    t_all = t_ctx + t_lat
    l = 0

    p = {
        'g_pre1': g_pre1[l][None], 'w_in': w_in[l], 'conv_w': conv_w[l], 'conv_b': conv_b[l][None],
        'g_q': g_q[l][None], 'g_k': g_k[l][None], 'w_conv_out': w_conv_out[l], 'w_attn_out': w_attn_out[l],
        'w_out_b': w_out[l].astype(BF16), 'g_post1': g_post1[l][None], 'g_pre2': g_pre2[l][None],
        'w_router_t': w_router[l].T,
    }
    cond = jnp.concatenate([c_ctx[None], c, jnp.zeros((SUBLANES - 1 - dec_batch, d), F32)], axis=0)
    mod = _modulation(cond, w_mod[l], b_mod[l][None])
    mod_ctx, mod_lat = mod[0:1], mod[1:1 + dec_batch]

    xc = x_prompt.reshape(t_ctx, d)
    xl = x_sample.reshape(t_lat, d)
    cache = (cache_k[:, l].reshape(dec_batch, past, n_kv * head_dim),
             cache_v[:, l].reshape(dec_batch, past, n_kv * head_dim))
    rope = _rope_tables(dec_seq)

    x1c, h2bc, h2pc, logc, k32, v32 = _sublayer1(xc, mod_ctx, p, seq, None, None, n_heads, n_kv, 256)
    x1l, h2bl, h2pl, logl, _, _ = _sublayer1(xl, mod_lat, p, dec_seq, rope, cache, n_heads, n_kv, 256)
    idx_all, w_all, rank_all, cnt = _router(jnp.concatenate([logc, logl], axis=1), b_router[l][:, None])

    bm = 256
    n_blocks = (t_all * TOP_K + n_exp * (bm - 1)) // bm + 1
    n_slots = n_blocks * bm
    blocks_e = jnp.floor((cnt[:, 0] + (bm - 1)) / bm)
    first_e = jnp.cumsum(blocks_e) - blocks_e
    pstart = (first_e * bm).astype(I32)

    tmf = 128
    w_tok = w_all.T
    dest = _dest_slots(pstart, idx_all, rank_all)
    dest3 = dest.reshape(TOP_K, t_all // tmf, tmf).transpose(1, 0, 2)
    used_end = jnp.sum(blocks_e, keepdims=True) * bm
    pad_lo = jnp.concatenate([first_e * bm + cnt[:, 0], used_end]).astype(I32)
    pad_hi = jnp.concatenate([(first_e + blocks_e) * bm, jnp.full((1,), float(n_slots), F32)]).astype(I32)
    slot_tab = _slot_table(pad_lo, pad_hi, dest3, n_slots)

    h2p = jnp.concatenate([h2pc, h2pl], axis=0)
    xs = _gather_rows(slot_tab, h2p, GATHER_BLOCK)
    yb = _experts(first_e.astype(I32), blocks_e.astype(I32), xs, w_e_gate[l], w_e_up[l], w_e_down[l], bm)

    ws = (w_s_gate[l].astype(BF16), w_s_up[l].astype(BF16), w_s_down[l].astype(BF16))

    def finish(lo, n, x1, h2b, mod_g):
        dest_g = dest3[lo // tmf:(lo + n) // tmf]
        return _final(dest_g, w_tok[lo:lo + n], yb, x1, h2b, mod_g, g_post2[l][None], *ws)

    y_ctx = finish(0, t_ctx, x1c, h2bc, mod_ctx)
    y_lat = finish(t_ctx, t_lat, x1l, h2bl, mod_lat)

    new_k = k32.reshape(batch, 1, seq, n_kv, head_dim)
    new_v = v32.reshape(batch, 1, seq, n_kv, head_dim)
    return (y_ctx.reshape(batch, seq, d), y_lat.reshape(dec_batch, dec_seq, d), new_k, new_v)
```

```python
import functools

import jax
import jax.numpy as jnp
from jax import lax
from jax.experimental import pallas as pl
from jax.experimental.pallas import tpu as pltpu

GRID_W = 64
ROPE_THETA = 10000.0
N_GROUPS = 8
TOPK_GROUPS = 4
TOP_K = 8
ROUTED_SCALE = 2.5
EPS = 1e-6

LANES = 128
SUBLANES = 8
V7X_VMEM_BYTES = 64 * 1024 * 1024
MIB = 1024 * 1024

F32 = jnp.float32
BF16 = jnp.bfloat16
I32 = jnp.int32


def _params(n_grid, vmem_mib):
    assert vmem_mib * MIB < V7X_VMEM_BYTES
    return pltpu.CompilerParams(
        dimension_semantics=("arbitrary",) * n_grid, vmem_limit_bytes=vmem_mib * MIB)


def _silu(x):
    return x * jax.nn.sigmoid(x)


def _rms(x, g):
    return x * lax.rsqrt(jnp.mean(x * x, axis=-1, keepdims=True) + EPS) * g


def _pack_rows(val, out_ref, stage_ref, rows, row0=0, stage0=0):
    half = val.shape[1] // 2
    for s in range(half // LANES):
        base = stage0 + s * 2 * rows
        stage_ref[pl.ds(base, rows, stride=2), :] = val[:, s * LANES:(s + 1) * LANES]
        stage_ref[pl.ds(base + 1, rows, stride=2), :] = val[:, half + s * LANES:half + (s + 1) * LANES]
        pair = stage_ref[pl.ds(base, 2 * rows), :].astype(BF16)
        out_ref[pl.ds(row0 + s, rows, stride=SUBLANES), :] = pltpu.bitcast(pair, I32)


def _unpack_word(word, stage_ref, slot):
    rows = word.shape[0]
    base = slot * 2 * rows
    stage_ref[pl.ds(base, 2 * rows), :] = pltpu.bitcast(word, BF16).astype(F32)
    return stage_ref[pl.ds(base, rows, stride=2), :], stage_ref[pl.ds(base + 1, rows, stride=2), :]


def _mod_kernel(c_ref, w_ref, b_ref, o_ref):
    s = _silu(c_ref[...]).astype(BF16)
    o_ref[...] = jnp.dot(s, w_ref[...].astype(BF16), preferred_element_type=F32) + b_ref[...]


def _modulation(cond, w, b):
    rows, d = cond.shape
    n = w.shape[1]
    tn = 1024
    return pl.pallas_call(
        _mod_kernel,
        grid=(n // tn,),
        in_specs=[pl.BlockSpec((rows, d), lambda j: (0, 0)),
                  pl.BlockSpec((d, tn), lambda j: (0, j)),
                  pl.BlockSpec((1, tn), lambda j: (0, j))],
        out_specs=pl.BlockSpec((rows, tn), lambda j: (0, j)),
        out_shape=jax.ShapeDtypeStruct((rows, n), F32),
        compiler_params=_params(1, 40),
        name="modulation",
    )(cond, w, b)


def _prenorm_kernel(x_ref, shift_ref, scale_ref, g_ref, o_ref, *, tiles_per_row):
    r = pl.program_id(0) // tiles_per_row
    y = _rms(x_ref[...], g_ref[...])
    o_ref[...] = (y * (1.0 + scale_ref[pl.ds(r, 1), :]) + shift_ref[pl.ds(r, 1), :]).astype(o_ref.dtype)


def _prenorm(x, mod, g, shift_col, scale_col):
    t, d = x.shape
    nb = mod.shape[0]
    tm = 512
    return pl.pallas_call(
        functools.partial(_prenorm_kernel, tiles_per_row=t // nb // tm),
        grid=(t // tm,),
        in_specs=[pl.BlockSpec((tm, d), lambda i: (i, 0)),
                  pl.BlockSpec((nb, d), lambda i: (0, shift_col)),
                  pl.BlockSpec((nb, d), lambda i: (0, scale_col)),
                  pl.BlockSpec((1, d), lambda i: (0, 0))],
        out_specs=pl.BlockSpec((tm, d), lambda i: (i, 0)),
        out_shape=jax.ShapeDtypeStruct((t, d), BF16),
        compiler_params=_params(1, 32),
        name="prenorm",
    )(x, mod, mod, g)


def _mm_wcast_kernel(a_ref, w_ref, o_ref, wb_ref):
    @pl.when(pl.program_id(1) == 0)
    def _():
        wb_ref[...] = w_ref[...].astype(BF16)

    o_ref[...] = jnp.dot(a_ref[...], wb_ref[...], preferred_element_type=F32).astype(o_ref.dtype)


def _matmul_wcast(a, w, col_map, n_out, out_dtype):
    m, k = a.shape
    tn = 1024
    tm = 2048 if jnp.dtype(out_dtype).itemsize == 2 else 1024
    return pl.pallas_call(
        _mm_wcast_kernel,
        grid=(n_out // tn, m // tm),
        in_specs=[pl.BlockSpec((tm, k), lambda j, i: (i, 0)),
                  pl.BlockSpec((k, tn), lambda j, i: (0, col_map(j)))],
        out_specs=pl.BlockSpec((tm, tn), lambda j, i: (i, j)),
        out_shape=jax.ShapeDtypeStruct((m, n_out), out_dtype),
        scratch_shapes=[pltpu.VMEM((k, tn), BF16)],
        compiler_params=_params(2, 56),
        name="in_proj",
    )(a, w)


def _z_col_map(j):
    return jnp.where(j < 2, j + 3, jnp.where(j < 6, j + 4, j - 6))


def _qk_kernel(*refs, use_rope, n_heads, n_kv, q_scale):
    if use_rope:
        q_ref, kv_ref, gq_ref, gk_ref, cos_ref, sin_ref, qn_ref, kn_ref, vb_ref, k32_ref, v32_ref = refs
        cos = cos_ref[...]
        sin = sin_ref[...]
        even = lax.broadcasted_iota(I32, cos.shape, 1) % 2 == 0
    else:
        q_ref, kv_ref, gq_ref, gk_ref, qn_ref, kn_ref, vb_ref, k32_ref, v32_ref = refs

    def norm_rope(xh, g):
        y = _rms(xh, g)
        if use_rope:
            sw = jnp.where(even, pltpu.roll(y, LANES - 1, 1), pltpu.roll(y, 1, 1))
            y = y * cos + sw * sin
        return y

    gq = gq_ref[...]
    gk = gk_ref[...]
    for h in range(n_heads):
        sl = slice(h * LANES, (h + 1) * LANES)
        qn_ref[:, sl] = (norm_rope(q_ref[:, sl].astype(F32), gq) * q_scale).astype(BF16)
    kw = n_kv * LANES
    for h in range(n_kv):
        sl = slice(h * LANES, (h + 1) * LANES)
        kh = norm_rope(kv_ref[:, sl], gk)
        k32_ref[:, sl] = kh
        kn_ref[:, sl] = kh.astype(BF16)
    v = kv_ref[:, kw:2 * kw]
    v32_ref[...] = v
    vb_ref[...] = v.astype(BF16)


def _qk_prep(z, kv, gq, gk, rope, seq_len, n_heads, n_kv):
    t = z.shape[0]
    dq = n_heads * LANES
    dk = n_kv * LANES
    tm = 256
    in_specs = [pl.BlockSpec((tm, dq), lambda i: (i, 0)),
                pl.BlockSpec((tm, 2 * dk), lambda i: (i, 0)),
                pl.BlockSpec((1, LANES), lambda i: (0, 0)),
                pl.BlockSpec((1, LANES), lambda i: (0, 0))]
    args = [z, kv, gq, gk]
    if rope is not None:
        per_seq = seq_len // tm
        in_specs += [pl.BlockSpec((tm, LANES), lambda i: (i % per_seq, 0))] * 2
        args += list(rope)
    out_specs = [pl.BlockSpec((tm, dq), lambda i: (i, 0))] + [pl.BlockSpec((tm, dk), lambda i: (i, 0))] * 4
    out_shape = [jax.ShapeDtypeStruct((t, dq), BF16), jax.ShapeDtypeStruct((t, dk), BF16),
                 jax.ShapeDtypeStruct((t, dk), BF16), jax.ShapeDtypeStruct((t, dk), F32),
                 jax.ShapeDtypeStruct((t, dk), F32)]
    return pl.pallas_call(
        functools.partial(_qk_kernel, use_rope=rope is not None, n_heads=n_heads, n_kv=n_kv,
                          q_scale=LANES ** -0.5),
        grid=(t // tm,),
        in_specs=in_specs, out_specs=out_specs, out_shape=out_shape,
        compiler_params=_params(1, 32),
        name="qk_prep",
    )(*args)


def _attn_kernel(*refs, has_cache, group, chunks):
    if has_cache:
        q_ref, k_ref, v_ref, ck_ref, cv_ref, o_ref = refs
    else:
        q_ref, k_ref, v_ref, o_ref = refs
    tq = q_ref.shape[0]
    nt = (((1,), (1,)), ((), ()))
    if has_cache:
        ck = ck_ref[...].astype(BF16)
        cv = cv_ref[...].astype(BF16)
    per = group // chunks
    for c in range(chunks):
        heads = range(c * per, (c + 1) * per)
        q = jnp.concatenate([q_ref[:, g * LANES:(g + 1) * LANES] for g in heads], axis=0)
        s_own = lax.dot_general(q, k_ref[...], nt, preferred_element_type=F32)
        m = jnp.max(s_own, axis=-1, keepdims=True)
        if has_cache:
            s_ctx = lax.dot_general(q, ck, nt, preferred_element_type=F32)
            m = jnp.maximum(m, jnp.max(s_ctx, axis=-1, keepdims=True))
        p = jnp.exp(s_own - m)
        denom = jnp.sum(p, axis=-1, keepdims=True)
        acc = jnp.dot(p.astype(BF16), v_ref[...], preferred_element_type=F32)
        if has_cache:
            pc = jnp.exp(s_ctx - m)
            denom = denom + jnp.sum(pc, axis=-1, keepdims=True)
            acc = acc + jnp.dot(pc.astype(BF16), cv, preferred_element_type=F32)
        o = acc / denom
        for n, g in enumerate(heads):
            o_ref[:, g * LANES:(g + 1) * LANES] = o[n * tq:(n + 1) * tq].astype(o_ref.dtype)


def _attention(qn, kn, vb, cache, batch, seq_len, n_heads, n_kv, tq):
    t = qn.shape[0]
    group = n_heads // n_kv
    nq = seq_len // tq
    in_specs = [pl.BlockSpec((tq, group * LANES), lambda b, h, i: (b * nq + i, h)),
                pl.BlockSpec((seq_len, LANES), lambda b, h, i: (b, h)),
                pl.BlockSpec((seq_len, LANES), lambda b, h, i: (b, h))]
    args = [qn, kn, vb]
    if cache is not None:
        past = cache[0].shape[1]
        in_specs += [pl.BlockSpec((None, past, LANES), lambda b, h, i: (b, 0, h))] * 2
        args += list(cache)
    return pl.pallas_call(
        functools.partial(_attn_kernel, has_cache=cache is not None, group=group,
                          chunks=group if cache is not None else group // 2),
        grid=(batch, n_kv, nq),
        in_specs=in_specs,
        out_specs=pl.BlockSpec((tq, group * LANES), lambda b, h, i: (b * nq + i, h)),
        out_shape=jax.ShapeDtypeStruct((t, n_heads * LANES), BF16),
        compiler_params=_params(3, 40),
        name="attention",
    )(*args)


CONV_HALO = 16


def _conv_kernel(u_ref, b_ref, c_ref, up_ref, cp_ref, un_ref, cn_ref, w_ref, bias_ref, o_ref, *,
                 tiles_per_seq):
    tm = u_ref.shape[0]
    pos = pl.program_id(0) % tiles_per_seq
    cu = c_ref[...].astype(F32) * u_ref[...].astype(F32)
    halo_prev = (cp_ref[...].astype(F32) * up_ref[...].astype(F32))[CONV_HALO - 1:CONV_HALO, :]
    halo_next = (cn_ref[...].astype(F32) * un_ref[...].astype(F32))[0:1, :]
    halo_prev = jnp.where(pos == 0, 0.0, halo_prev)
    halo_next = jnp.where(pos == tiles_per_seq - 1, 0.0, halo_next)
    row = lax.broadcasted_iota(I32, cu.shape, 0)
    prev = jnp.where(row == 0, halo_prev, pltpu.roll(cu, 1, 0))
    nxt = jnp.where(row == tm - 1, halo_next, pltpu.roll(cu, tm - 1, 0))
    w = w_ref[...]
    conv = prev * w[0:1, :] + cu * w[1:2, :] + nxt * w[2:3, :] + bias_ref[...]
    o_ref[...] = (b_ref[...].astype(F32) * conv).astype(o_ref.dtype)


def _gated_conv(z, conv_w, conv_b, seq_len, col0):
    t = z.shape[0]
    dc = conv_w.shape[1]
    tm = 256
    hb = tm // CONV_HALO
    last = t // CONV_HALO - 1
    prev_map = lambda c: (lambda i: (jnp.maximum(i * hb - 1, 0), c))
    next_map = lambda c: (lambda i: (jnp.minimum((i + 1) * hb, last), c))
    in_specs = [pl.BlockSpec((tm, dc), lambda i: (i, col0)),
                pl.BlockSpec((tm, dc), lambda i: (i, col0 + 1)),
                pl.BlockSpec((tm, dc), lambda i: (i, col0 + 2)),
                pl.BlockSpec((CONV_HALO, dc), prev_map(col0)),
                pl.BlockSpec((CONV_HALO, dc), prev_map(col0 + 2)),
                pl.BlockSpec((CONV_HALO, dc), next_map(col0)),
                pl.BlockSpec((CONV_HALO, dc), next_map(col0 + 2)),
                pl.BlockSpec(conv_w.shape, lambda i: (0, 0)),
                pl.BlockSpec((1, dc), lambda i: (0, 0))]
    return pl.pallas_call(
        functools.partial(_conv_kernel, tiles_per_seq=seq_len // tm),
        grid=(t // tm,),
        in_specs=in_specs,
        out_specs=pl.BlockSpec((tm, dc), lambda i: (i, 0)),
        out_shape=jax.ShapeDtypeStruct((t, dc), BF16),
        compiler_params=_params(1, 32),
        name="gated_conv",
    )(z, z, z, z, z, z, z, conv_w, conv_b)


def _merge_kernel(cv_ref, at_ref, gc_ref, ga_ref, wc_ref, wa_ref, o_ref, wcb_ref, wab_ref):
    @pl.when(pl.program_id(1) == 0)
    def _():
        wcb_ref[...] = wc_ref[...].astype(BF16)
        wab_ref[...] = wa_ref[...].astype(BF16)

    conv_out = jnp.dot(cv_ref[...], wcb_ref[...], preferred_element_type=F32)
    attn_out = jnp.dot(at_ref[...], wab_ref[...], preferred_element_type=F32)
    merged = (jax.nn.sigmoid(gc_ref[...].astype(F32)) * conv_out
              + jax.nn.sigmoid(ga_ref[...].astype(F32)) * attn_out)
    o_ref[...] = merged.astype(o_ref.dtype)


def _merge(cvg, attn, z, w_conv_out, w_attn_out, gc_col0, ga_col0):
    t, dc = cvg.shape
    dq = attn.shape[1]
    d = w_conv_out.shape[1]
    tm, tn = 512, 512
    return pl.pallas_call(
        _merge_kernel,
        grid=(d // tn, t // tm),
        in_specs=[pl.BlockSpec((tm, dc), lambda j, i: (i, 0)),
                  pl.BlockSpec((tm, dq), lambda j, i: (i, 0)),
                  pl.BlockSpec((tm, tn), lambda j, i: (i, gc_col0 + j)),
                  pl.BlockSpec((tm, tn), lambda j, i: (i, ga_col0 + j)),
                  pl.BlockSpec((dc, tn), lambda j, i: (0, j)),
                  pl.BlockSpec((dq, tn), lambda j, i: (0, j))],
        out_specs=pl.BlockSpec((tm, tn), lambda j, i: (i, j)),
        out_shape=jax.ShapeDtypeStruct((t, d), BF16),
        scratch_shapes=[pltpu.VMEM((dc, tn), BF16), pltpu.VMEM((dq, tn), BF16)],
        compiler_params=_params(2, 40),
        name="merge",
    )(cvg, attn, z, z, w_conv_out, w_attn_out)


def _route(logits_t, bias_col, carry):
    n_exp, tm = logits_t.shape
    per = n_exp // N_GROUPS
    assert per == SUBLANES
    neg = -jnp.inf
    scores = jax.nn.sigmoid(logits_t)
    biased = scores + bias_col
    sub = lax.broadcasted_iota(I32, (per, tm), 0).astype(F32)
    xs = [biased[g * per:(g + 1) * per, :] for g in range(N_GROUPS)]
    sc = [scores[g * per:(g + 1) * per, :] for g in range(N_GROUPS)]
    ids = [sub + float(g * per) for g in range(N_GROUPS)]

    def colmax(a):
        return jnp.max(a, axis=0, keepdims=True)

    def colmin(a):
        return jnp.min(a, axis=0, keepdims=True)

    rows = []
    for g in range(N_GROUPS):
        m1 = colmax(xs[g])
        j1 = colmin(jnp.where(xs[g] == m1, sub, float(per)))
        m2 = colmax(jnp.where(sub == j1, neg, xs[g]))
        rows.append(m1 + m2)
    gs = jnp.concatenate(rows, axis=0)
    gsel = jnp.zeros_like(gs)
    for _ in range(TOPK_GROUPS):
        m = colmax(gs)
        j = colmin(jnp.where(gs == m, sub, float(N_GROUPS)))
        hit = sub == j
        gsel = jnp.where(hit, 1.0, gsel)
        gs = jnp.where(hit, neg, gs)
    masked = [jnp.where(gsel[g:g + 1, :] > 0.0, xs[g], neg) for g in range(N_GROUPS)]
    idx_rows, w_rows = [], []
    member = [jnp.zeros((per, tm), F32) for _ in range(N_GROUPS)]
    for _ in range(TOP_K):
        mm = masked[0]
        for g in range(1, N_GROUPS):
            mm = jnp.maximum(mm, masked[g])
        m = colmax(mm)
        idx = colmin(jnp.where(masked[0] == m, ids[0], float(n_exp)))
        for g in range(1, N_GROUPS):
            idx = jnp.minimum(idx, colmin(jnp.where(masked[g] == m, ids[g], float(n_exp))))
        wk = jnp.zeros_like(idx)
        for g in range(N_GROUPS):
            hit = ids[g] == idx
            wk = wk + jnp.sum(jnp.where(hit, sc[g], 0.0), axis=0, keepdims=True)
            masked[g] = jnp.where(hit, neg, masked[g])
            member[g] = jnp.where(hit, 1.0, member[g])
        idx_rows.append(idx)
        w_rows.append(wk)
    w = jnp.concatenate(w_rows, axis=0)
    w = w / jnp.sum(w, axis=0, keepdims=True) * ROUTED_SCALE

    earlier = (lax.broadcasted_iota(I32, (tm, tm), 0) < lax.broadcasted_iota(I32, (tm, tm), 1)).astype(BF16)
    before = jnp.dot(jnp.concatenate(member, axis=0).astype(BF16), earlier, preferred_element_type=F32)
    rank_rows = []
    for k in range(TOP_K):
        rk = jnp.zeros_like(idx_rows[k])
        for g in range(N_GROUPS):
            pos = before[g * per:(g + 1) * per, :] + carry[g]
            rk = rk + jnp.sum(jnp.where(ids[g] == idx_rows[k], pos, 0.0), axis=0, keepdims=True)
        rank_rows.append(rk)
    new_carry = [carry[g] + jnp.sum(member[g], axis=1, keepdims=True) for g in range(N_GROUPS)]
    idx = jnp.concatenate(idx_rows, axis=0).astype(I32)
    rank = jnp.concatenate(rank_rows, axis=0).astype(I32)
    return idx, w, rank, new_carry


def _out_kernel(mg_ref, x_ref, gate_ref, shift_ref, scale_ref, gpost_ref, gpre_ref, wo_ref, wr_ref,
                x1_ref, h2b_ref, h2p_ref, logit_ref, stage_ref, *, tiles_per_row):
    tm = x_ref.shape[0]
    r = pl.program_id(0) // tiles_per_row
    gate = gate_ref[pl.ds(r, 1), :]
    scale = 1.0 + scale_ref[pl.ds(r, 1), :]
    shift = shift_ref[pl.ds(r, 1), :]
    rows = tm // OUT_CHUNKS
    for c in range(OUT_CHUNKS):
        sl = pl.ds(c * rows, rows)
        mix = jnp.dot(mg_ref[sl, :], wo_ref[...], preferred_element_type=F32)
        x1 = x_ref[sl, :] + gate * _rms(mix, gpost_ref[...])
        x1_ref[sl, :] = x1
        h2 = _rms(x1, gpre_ref[...]) * scale + shift
        h2b_ref[sl, :] = h2.astype(BF16)
        _pack_rows(h2, h2p_ref, stage_ref, rows, row0=c * rows * SUBLANES, stage0=c * rows * 2 * SUBLANES)
        logit_ref[:, sl] = lax.dot_general(wr_ref[...], h2, (((1,), (1,)), ((), ())),
                                           preferred_element_type=F32, precision=lax.Precision.HIGHEST)


OUT_CHUNKS = 4


def _out_proj(merged, x, mod, g_post, g_pre, w_out_b, w_router_t):
    t, d = x.shape
    nb = mod.shape[0]
    n_exp = w_router_t.shape[0]
    tm = 512
    row = lambda i: (i, 0)
    fixed = lambda i: (0, 0)
    once = pl.Buffered(1)
    in_specs = [pl.BlockSpec((tm, d), row), pl.BlockSpec((tm, d), row),
                pl.BlockSpec((nb, d), lambda i: (0, 2)),
                pl.BlockSpec((nb, d), lambda i: (0, 3)),
                pl.BlockSpec((nb, d), lambda i: (0, 4)),
                pl.BlockSpec((1, d), fixed), pl.BlockSpec((1, d), fixed),
                pl.BlockSpec((d, d), fixed, pipeline_mode=once),
                pl.BlockSpec((n_exp, d), fixed, pipeline_mode=once)]
    out_specs = [pl.BlockSpec((tm, d), row), pl.BlockSpec((tm, d), row),
                 pl.BlockSpec((tm * SUBLANES, LANES), row),
                 pl.BlockSpec((n_exp, tm), lambda i: (0, i))]
    out_shape = [jax.ShapeDtypeStruct((t, d), F32), jax.ShapeDtypeStruct((t, d), BF16),
                 jax.ShapeDtypeStruct((t * SUBLANES, LANES), I32),
                 jax.ShapeDtypeStruct((n_exp, t), F32)]
    return pl.pallas_call(
        functools.partial(_out_kernel, tiles_per_row=t // nb // tm),
        grid=(t // tm,),
        in_specs=in_specs, out_specs=out_specs, out_shape=out_shape,
        scratch_shapes=[pltpu.VMEM((SUBLANES * 2 * tm, LANES), F32)],
        compiler_params=_params(1, 56),
        name="out_proj",
    )(merged, x, mod, mod, mod, g_post, g_pre, w_out_b, w_router_t)


def _router_kernel(logit_ref, br_ref, idx_ref, wsel_ref, rank_ref, cnt_ref):
    per = SUBLANES

    @pl.when(pl.program_id(0) == 0)
    def _():
        cnt_ref[...] = jnp.zeros_like(cnt_ref)

    carry = [cnt_ref[g * per:(g + 1) * per, 0:1] for g in range(N_GROUPS)]
    idx, w, rank, carry = _route(logit_ref[...], br_ref[...], carry)
    idx_ref[...] = idx
    wsel_ref[...] = w
    rank_ref[...] = rank
    for g in range(N_GROUPS):
        cnt_ref[g * per:(g + 1) * per, :] = jnp.broadcast_to(carry[g], (per, LANES))


def _router(logits_t, b_router_col):
    n_exp, t = logits_t.shape
    tr = 1024
    tile = lambda i: (0, i)
    fixed = lambda i: (0, 0)
    return pl.pallas_call(
        _router_kernel,
        grid=(t // tr,),
        in_specs=[pl.BlockSpec((n_exp, tr), tile), pl.BlockSpec((n_exp, 1), fixed)],
        out_specs=[pl.BlockSpec((TOP_K, tr), tile), pl.BlockSpec((TOP_K, tr), tile),
                   pl.BlockSpec((TOP_K, tr), tile), pl.BlockSpec((n_exp, LANES), fixed)],
        out_shape=[jax.ShapeDtypeStruct((TOP_K, t), I32), jax.ShapeDtypeStruct((TOP_K, t), F32),
                   jax.ShapeDtypeStruct((TOP_K, t), I32), jax.ShapeDtypeStruct((n_exp, LANES), F32)],
        compiler_params=_params(1, 32),
        name="router",
    )(logits_t, b_router_col)


def _dest_kernel(pstart_ref, idx_ref, rank_ref, o_ref):
    idx = idx_ref[...]
    acc = rank_ref[...]
    for e in range(pstart_ref.shape[0]):
        acc = acc + jnp.where(idx == e, pstart_ref[e], 0)
    o_ref[...] = acc


def _dest_slots(pstart, idx, rank):
    return pl.pallas_call(
        _dest_kernel,
        in_specs=[pl.BlockSpec(memory_space=pltpu.SMEM),
                  pl.BlockSpec(memory_space=pltpu.VMEM), pl.BlockSpec(memory_space=pltpu.VMEM)],
        out_specs=pl.BlockSpec(memory_space=pltpu.VMEM),
        out_shape=jax.ShapeDtypeStruct(idx.shape, I32),
        name="dest_slots",
    )(pstart, idx, rank)


def _slot_table_kernel(pad_lo_ref, pad_hi_ref, dest_ref, tab_ref):
    i = pl.program_id(0)
    tm = dest_ref.shape[2]

    @pl.when(i == 0)
    def _():
        def clear_range(e, carry):
            def clear(s, c):
                tab_ref[s] = 0
                return c
            return lax.fori_loop(pad_lo_ref[e], pad_hi_ref[e], clear, carry)
        lax.fori_loop(0, pad_lo_ref.shape[0], clear_range, 0)

    def fill(t, carry):
        for k in range(TOP_K):
            tab_ref[dest_ref[0, k, t]] = (i * tm + t) * TOP_K + k
        return carry

    lax.fori_loop(0, tm, fill, 0)


def _slot_table(pad_lo, pad_hi, dest3, n_slots):
    nt, _, tm = dest3.shape
    grid_spec = pltpu.PrefetchScalarGridSpec(
        num_scalar_prefetch=2,
        grid=(nt,),
        in_specs=[pl.BlockSpec((1, TOP_K, tm), lambda i, lo, hi: (i, 0, 0), memory_space=pltpu.SMEM)],
        out_specs=pl.BlockSpec(memory_space=pltpu.SMEM),
    )
    return pl.pallas_call(
        _slot_table_kernel,
        grid_spec=grid_spec,
        out_shape=jax.ShapeDtypeStruct((n_slots,), I32),
        compiler_params=_params(1, 16),
        name="slot_table",
    )(pad_lo, pad_hi, dest3)


GATHER_UNROLL = 16
GATHER_BLOCK = 1024
assert TOP_K == SUBLANES


def _gather_kernel(tab_ref, src_ref, o_ref):
    n = tab_ref.shape[2]

    def move(c, carry):
        for u in range(GATHER_UNROLL):
            r = c * GATHER_UNROLL + u
            src = pl.multiple_of(tab_ref[0, 0, r] & -SUBLANES, SUBLANES)
            dst = pl.multiple_of(r * SUBLANES, SUBLANES)
            o_ref[pl.ds(dst, SUBLANES), :] = src_ref[pl.ds(src, SUBLANES), :]
        return carry

    lax.fori_loop(0, n // GATHER_UNROLL, move, 0)


def _gather_rows(slot_tab, h2p, gb):
    n_slots = slot_tab.shape[0]
    nblk = n_slots // gb
    resident = h2p.size * h2p.dtype.itemsize
    return pl.pallas_call(
        _gather_kernel,
        grid=(nblk,),
        in_specs=[pl.BlockSpec((1, 1, gb), lambda b: (b, 0, 0), memory_space=pltpu.SMEM),
                  pl.BlockSpec(memory_space=pltpu.VMEM)],
        out_specs=pl.BlockSpec((gb * SUBLANES, LANES), lambda b: (b, 0)),
        out_shape=jax.ShapeDtypeStruct((n_slots * SUBLANES, LANES), I32),
        compiler_params=_params(1, resident // MIB + 8),
        name="dispatch_gather",
    )(slot_tab.reshape(nblk, 1, gb), h2p)


ROW_DMA_PRIORITY = 1
EXPERT_CHUNKS = 1
X_SLOTS = 6
Y_SLOTS = 3


def _expert_kernel(first_ref, nblk_ref, xs_hbm, wg_ref, wu_ref, wd_ref, yb_hbm,
                   xbuf_ref, obuf_ref, in_sem, out_sem, wgb_ref, wub_ref, wdb_ref, stage_in_ref, stage_out_ref,
                   *, bm):
    e = pl.program_id(0)
    nb = nblk_ref[e]
    b0 = first_ref[e]
    rows = bm * SUBLANES

    def x_copy(j, slot, base=None):
        blk = (b0 if base is None else base) + j
        return pltpu.make_async_copy(
            xs_hbm.at[pl.ds(pl.multiple_of(blk * rows, rows), rows), :],
            xbuf_ref.at[pl.ds(pl.multiple_of(slot * rows, rows), rows), :], in_sem.at[slot])

    def y_copy(j, slot):
        return pltpu.make_async_copy(
            obuf_ref.at[pl.ds(pl.multiple_of(slot * rows, rows), rows), :],
            yb_hbm.at[pl.ds(pl.multiple_of((b0 + j) * rows, rows), rows), :], out_sem.at[slot])

    def start_head(count, base=None):
        for j in range(X_SLOTS - 1):
            @pl.when(count > j)
            def _(j=j):
                x_copy(j, j, base).start(priority=ROW_DMA_PRIORITY)

    @pl.when(e == 0)
    def _():
        start_head(nb)

    @pl.when(nb > 0)
    def _():
        wgb_ref[...] = wg_ref[...].astype(BF16)
        wub_ref[...] = wu_ref[...].astype(BF16)
        wdb_ref[...] = wd_ref[...].astype(BF16)

    def block(j, carry):
        xslot = lax.rem(j, X_SLOTS)
        yslot = lax.rem(j, Y_SLOTS)
        x_copy(j, xslot).wait()

        ahead = j + (X_SLOTS - 1)

        @pl.when(ahead < nb)
        def _():
            x_copy(ahead, lax.rem(ahead, X_SLOTS)).start(priority=ROW_DMA_PRIORITY)

        @pl.when(j >= Y_SLOTS)
        def _():
            y_copy(j - Y_SLOTS, yslot).wait()

        xbase = xslot * rows
        ybase = yslot * rows
        sub = bm // EXPERT_CHUNKS
        for c in range(EXPERT_CHUNKS):
            cbase = xbase + c * sub * SUBLANES
            obase = ybase + c * sub * SUBLANES
            los, his = [], []
            for s in range(SUBLANES):
                lo, hi = _unpack_word(xbuf_ref[pl.ds(cbase + s, sub, stride=SUBLANES), :], stage_in_ref,
                                      c * SUBLANES + s)
                los.append(lo.astype(BF16))
                his.append(hi.astype(BF16))
            x = jnp.concatenate(los + his, axis=1)
            g = jnp.dot(x, wgb_ref[...], preferred_element_type=F32)
            u = jnp.dot(x, wub_ref[...], preferred_element_type=F32)
            a = (_silu(g) * u).astype(BF16)
            y = jnp.dot(a, wdb_ref[...], preferred_element_type=F32)
            _pack_rows(y, obuf_ref, stage_out_ref, sub, row0=obase, stage0=c * sub * 2 * SUBLANES)
        y_copy(j, yslot).start()
        return carry

    lax.fori_loop(0, nb, block, 0)

    @pl.when(e + 1 < pl.num_programs(0))
    def _():
        nxt = jnp.minimum(e + 1, pl.num_programs(0) - 1)
        start_head(nblk_ref[nxt], first_ref[nxt])

    for back in range(Y_SLOTS, 0, -1):
        @pl.when(nb >= back)
        def _(back=back):
            y_copy(nb - back, lax.rem(nb - back, Y_SLOTS)).wait()

    @pl.when(e == pl.num_programs(0) - 1)
    def _():
        obuf_ref[pl.ds(0, rows), :] = pltpu.bitcast(jnp.zeros((2 * rows, LANES), BF16), I32)

        def clear(j, carry):
            cp = pltpu.make_async_copy(obuf_ref.at[pl.ds(0, rows), :],
                                       yb_hbm.at[pl.ds(pl.multiple_of(j * rows, rows), rows), :], out_sem.at[0])
            cp.start()
            cp.wait()
            return carry

        lax.fori_loop(b0 + nb, yb_hbm.shape[0] // rows, clear, 0)


def _experts(first_block, n_blocks_e, xs, w_gate, w_up, w_down, bm):
    n_exp, d, de = w_gate.shape
    rows = bm * SUBLANES
    grid_spec = pltpu.PrefetchScalarGridSpec(
        num_scalar_prefetch=2,
        grid=(n_exp,),
        in_specs=[pl.BlockSpec(memory_space=pl.ANY),
                  pl.BlockSpec((None, d, de), lambda e, fb, nb: (e, 0, 0)),
                  pl.BlockSpec((None, d, de), lambda e, fb, nb: (e, 0, 0)),
                  pl.BlockSpec((None, de, d), lambda e, fb, nb: (e, 0, 0))],
        out_specs=pl.BlockSpec(memory_space=pl.ANY),
        scratch_shapes=[pltpu.VMEM((X_SLOTS * rows, LANES), I32), pltpu.VMEM((Y_SLOTS * rows, LANES), I32),
                        pltpu.SemaphoreType.DMA((X_SLOTS,)), pltpu.SemaphoreType.DMA((Y_SLOTS,)),
                        pltpu.VMEM((d, de), BF16), pltpu.VMEM((d, de), BF16), pltpu.VMEM((de, d), BF16),
                        pltpu.VMEM((SUBLANES * 2 * bm, LANES), F32), pltpu.VMEM((SUBLANES * 2 * bm, LANES), F32)],
    )
    return pl.pallas_call(
        functools.partial(_expert_kernel, bm=bm),
        grid_spec=grid_spec,
        out_shape=jax.ShapeDtypeStruct(xs.shape, I32),
        compiler_params=_params(1, 56),
        name="routed_experts",
    )(first_block, n_blocks_e, xs, w_gate, w_up, w_down)


def _final_kernel(dest_ref, w_ref, yb_hbm, x1_ref, h2_ref, gate_ref, gpost_ref, wsg_ref, wsu_ref,
                  wsd_ref, o_ref, buf_ref, sem, stage_ref, *, steps_per_row):
    tm = x1_ref.shape[0] // 2
    i = pl.program_id(0)
    last = pl.num_programs(0) - 1
    r = i // steps_per_row
    tile_rows = TOP_K * tm * SUBLANES
    gate = gate_ref[pl.ds(r, 1), :]

    def request(tile, slot, t, k):
        src = pl.multiple_of(dest_ref[tile, k, t], SUBLANES)
        dst = slot * tile_rows + (k * tm + t) * SUBLANES
        pltpu.make_async_copy(yb_hbm.at[pl.ds(src, SUBLANES), :], buf_ref.at[pl.ds(dst, SUBLANES), :],
                              sem.at[slot]).start(priority=k % 2)

    def wait_tile(slot):
        pltpu.make_async_copy(yb_hbm.at[pl.ds(0, tile_rows), :],
                              buf_ref.at[pl.ds(slot * tile_rows, tile_rows), :], sem.at[slot]).wait()

    @pl.when(i == 0)
    def _():
        def first(t, carry):
            for k in range(TOP_K):
                request(0, 0, t, k)
            return carry
        lax.fori_loop(0, tm, first, 0)

    pending = [(t, k) for t in range(tm) for k in range(TOP_K)]
    per_piece = 2 * len(pending) // (SUBLANES * TOP_K)

    def combine(half, slot, next_tile):
        rows = pl.ds(half * tm, tm)
        h = h2_ref[rows, :]
        g = jnp.dot(h, wsg_ref[...], preferred_element_type=F32)
        u = jnp.dot(h, wsu_ref[...], preferred_element_type=F32)
        shared = jnp.dot((_silu(g) * u).astype(BF16), wsd_ref[...], preferred_element_type=F32)
        w = w_ref[rows, :]
        base = slot * tile_rows
        los, his = [], []
        for s in range(SUBLANES):
            lo_acc = jnp.zeros((tm, LANES), F32)
            hi_acc = jnp.zeros((tm, LANES), F32)
            for k in range(TOP_K):
                lo, hi = _unpack_word(buf_ref[pl.ds(base + k * tm * SUBLANES + s, tm, stride=SUBLANES), :],
                                      stage_ref, s * TOP_K + k)
                wk = w[:, k:k + 1]
                lo_acc = lo_acc + wk * lo
                hi_acc = hi_acc + wk * hi
                n = s * TOP_K + k
                for t, kk in pending[n * per_piece:(n + 1) * per_piece]:
                    request(next_tile, 1 - slot, t, kk)
            los.append(lo_acc)
            his.append(hi_acc)
        ffn = jnp.concatenate(los + his, axis=1) + shared
        o_ref[rows, :] = x1_ref[rows, :] + gate * _rms(ffn, gpost_ref[...])

    wait_tile(0)
    combine(0, 0, 2 * i + 1)
    wait_tile(1)
    combine(1, 1, jnp.minimum(2 * i + 2, 2 * last + 1))

    @pl.when(i == last)
    def _():
        wait_tile(0)


def _final(dest8, w_tok, yb, x1, h2b, mod, g_post, ws_gate, ws_up, ws_down):
    t, d = x1.shape
    nb = mod.shape[0]
    ds_ = ws_gate.shape[1]
    tm = dest8.shape[2]
    n_steps = t // (2 * tm)
    row = lambda i, dst: (i, 0)
    fixed = lambda i, dst: (0, 0)
    grid_spec = pltpu.PrefetchScalarGridSpec(
        num_scalar_prefetch=1,
        grid=(n_steps,),
        in_specs=[pl.BlockSpec((2 * tm, TOP_K), row),
                  pl.BlockSpec(memory_space=pl.ANY),
                  pl.BlockSpec((2 * tm, d), row), pl.BlockSpec((2 * tm, d), row),
                  pl.BlockSpec((nb, d), lambda i, dst: (0, 5)),
                  pl.BlockSpec((1, d), fixed),
                  pl.BlockSpec((d, ds_), fixed), pl.BlockSpec((d, ds_), fixed), pl.BlockSpec((ds_, d), fixed)],
        out_specs=pl.BlockSpec((2 * tm, d), row),
        scratch_shapes=[pltpu.VMEM((2 * TOP_K * tm * SUBLANES, LANES), I32), pltpu.SemaphoreType.DMA((2,)),
                        pltpu.VMEM((SUBLANES * TOP_K * 2 * tm, LANES), F32)],
    )
    return pl.pallas_call(
        functools.partial(_final_kernel, steps_per_row=t // nb // (2 * tm)),
        grid_spec=grid_spec,
        out_shape=jax.ShapeDtypeStruct((t, d), F32),
        compiler_params=_params(1, 48),
        name="combine_final",
    )(dest8, w_tok, yb, x1, h2b, mod, g_post, ws_gate, ws_up, ws_down)


def _rope_tables(n_pos):
    half = LANES // 2
    pos = jnp.arange(n_pos)
    row = (pos // GRID_W).astype(F32)
    col = (pos % GRID_W).astype(F32)
    inv = ROPE_THETA ** (-jnp.arange(0, half, 2, dtype=F32) / half)
    ang = jnp.concatenate([row[:, None] * inv, col[:, None] * inv], axis=-1)
    cos = jnp.repeat(jnp.cos(ang), 2, axis=-1)
    sin = jnp.repeat(jnp.sin(ang), 2, axis=-1) * jnp.tile(jnp.array([-1.0, 1.0], F32), half)
    return cos, sin


def _sublayer1(x, mod, p, seq_len, rope, cache, n_heads, n_kv, tq):
    t = x.shape[0]
    h = _prenorm(x, mod, p['g_pre1'], 0, 1)
    z = _matmul_wcast(h, p['w_in'], _z_col_map, 9 * 1024, BF16)
    kv = _matmul_wcast(h, p['w_in'], lambda j: j + 5, 1024, F32)
    qn, kn, vb, k32, v32 = _qk_prep(z, kv, p['g_q'], p['g_k'], rope, seq_len, n_heads, n_kv)
    attn = _attention(qn, kn, vb, cache, t // seq_len, seq_len, n_heads, n_kv, tq)
    cvg = _gated_conv(z, p['conv_w'], p['conv_b'], seq_len, 6)
    merged = _merge(cvg, attn, z, p['w_conv_out'], p['w_attn_out'], 4, 8)
    x1, h2b, h2p, logits_t = _out_proj(merged, x, mod, p['g_post1'], p['g_pre2'], p['w_out_b'], p['w_router_t'])
    return x1, h2b, h2p, logits_t, k32, v32


def kernel(x_prompt, x_sample, cache_k, cache_v, c, c_ctx, w_mod, b_mod, g_pre1, w_in, conv_w, conv_b, g_q, g_k, w_conv_out, w_attn_out, w_out, g_post1, g_pre2, w_router, b_router, w_e_gate, w_e_up, w_e_down, w_s_gate, w_s_up, w_s_down, g_post2):
    batch, seq, d = x_prompt.shape
    dec_batch, dec_seq, _ = x_sample.shape
    depth = w_mod.shape[0]
    assert depth == 1
    past, n_kv, head_dim = cache_k.shape[2:]
    assert head_dim == LANES
    n_heads = w_attn_out.shape[1] // head_dim
    n_exp = w_router.shape[2]
    t_ctx, t_lat = batch * seq, dec_batch * dec_seq
    t_all = t_ctx + t_lat
    l = 0

    p = {
        'g_pre1': g_pre1[l][None], 'w_in': w_in[l], 'conv_w': conv_w[l], 'conv_b': conv_b[l][None],
        'g_q': g_q[l][None], 'g_k': g_k[l][None], 'w_conv_out': w_conv_out[l], 'w_attn_out': w_attn_out[l],
        'w_out_b': w_out[l].astype(BF16), 'g_post1': g_post1[l][None], 'g_pre2': g_pre2[l][None],
        'w_router_t': w_router[l].T,
    }
    cond = jnp.concatenate([c_ctx[None], c, jnp.zeros((SUBLANES - 1 - dec_batch, d), F32)], axis=0)
    mod = _modulation(cond, w_mod[l], b_mod[l][None])
    mod_ctx, mod_lat = mod[0:1], mod[1:1 + dec_batch]

    xc = x_prompt.reshape(t_ctx, d)
    xl = x_sample.reshape(t_lat, d)
    cache = (cache_k[:, l].reshape(dec_batch, past, n_kv * head_dim),
             cache_v[:, l].reshape(dec_batch, past, n_kv * head_dim))
    rope = _rope_tables(dec_seq)

    x1c, h2bc, h2pc, logc, k32, v32 = _sublayer1(xc, mod_ctx, p, seq, None, None, n_heads, n_kv, 256)
    x1l, h2bl, h2pl, logl, _, _ = _sublayer1(xl, mod_lat, p, dec_seq, rope, cache, n_heads, n_kv, 256)
    idx_all, w_all, rank_all, cnt = _router(jnp.concatenate([logc, logl], axis=1), b_router[l][:, None])

    bm = 256
    n_blocks = (t_all * TOP_K + n_exp * (bm - 1)) // bm + 1
    n_slots = n_blocks * bm
    blocks_e = jnp.floor((cnt[:, 0] + (bm - 1)) / bm)
    first_e = jnp.cumsum(blocks_e) - blocks_e
    pstart = (first_e * bm).astype(I32)

    tmf = 128
    w_tok = w_all.T
    dest = _dest_slots(pstart, idx_all, rank_all)
    dest3 = dest.reshape(TOP_K, t_all // tmf, tmf).transpose(1, 0, 2)
    used_end = jnp.sum(blocks_e, keepdims=True) * bm
    pad_lo = jnp.concatenate([first_e * bm + cnt[:, 0], used_end]).astype(I32)
    pad_hi = jnp.concatenate([(first_e + blocks_e) * bm, jnp.full((1,), float(n_slots), F32)]).astype(I32)
    slot_tab = _slot_table(pad_lo, pad_hi, dest3, n_slots)

    h2p = jnp.concatenate([h2pc, h2pl], axis=0)
    xs = _gather_rows(slot_tab, h2p, GATHER_BLOCK)
    yb = _experts(first_e.astype(I32), blocks_e.astype(I32), xs, w_e_gate[l], w_e_up[l], w_e_down[l], bm)

    ws = (w_s_gate[l].astype(BF16), w_s_up[l].astype(BF16), w_s_down[l].astype(BF16))

    def finish(lo, n, x1, h2b, mod_g):
        dest_g = dest3[lo // tmf:(lo + n) // tmf] * SUBLANES
        return _final(dest_g, w_tok[lo:lo + n], yb, x1, h2b, mod_g, g_post2[l][None], *ws)

    y_ctx = finish(0, t_ctx, x1c, h2bc, mod_ctx)
    y_lat = finish(t_ctx, t_lat, x1l, h2bl, mod_lat)

    new_k = k32.reshape(batch, 1, seq, n_kv, head_dim)
    new_v = v32.reshape(batch, 1, seq, n_kv, head_dim)
    return (y_ctx.reshape(batch, seq, d), y_lat.reshape(dec_batch, dec_seq, d), new_k, new_v)
```

```python
import functools

import jax
import jax.numpy as jnp
from jax import lax
from jax.experimental import pallas as pl
from jax.experimental.pallas import tpu as pltpu

GRID_W = 64
ROPE_THETA = 10000.0
N_GROUPS = 8
TOPK_GROUPS = 4
TOP_K = 8
ROUTED_SCALE = 2.5
EPS = 1e-6

LANES = 128
SUBLANES = 8
V7X_VMEM_BYTES = 64 * 1024 * 1024
MIB = 1024 * 1024

F32 = jnp.float32
BF16 = jnp.bfloat16
I32 = jnp.int32


def _params(n_grid, vmem_mib):
    assert vmem_mib * MIB < V7X_VMEM_BYTES
    return pltpu.CompilerParams(
        dimension_semantics=("arbitrary",) * n_grid, vmem_limit_bytes=vmem_mib * MIB)


def _silu(x):
    return x * jax.nn.sigmoid(x)


def _rms(x, g):
    return x * lax.rsqrt(jnp.mean(x * x, axis=-1, keepdims=True) + EPS) * g


def _pack_rows(val, out_ref, stage_ref, rows, row0=0, stage0=0):
    half = val.shape[1] // 2
    for s in range(half // LANES):
        base = stage0 + s * 2 * rows
        stage_ref[pl.ds(base, rows, stride=2), :] = val[:, s * LANES:(s + 1) * LANES]
        stage_ref[pl.ds(base + 1, rows, stride=2), :] = val[:, half + s * LANES:half + (s + 1) * LANES]
        pair = stage_ref[pl.ds(base, 2 * rows), :].astype(BF16)
        out_ref[pl.ds(row0 + s, rows, stride=SUBLANES), :] = pltpu.bitcast(pair, I32)


def _unpack_word(word, stage_ref, slot):
    rows = word.shape[0]
    base = slot * 2 * rows
    stage_ref[pl.ds(base, 2 * rows), :] = pltpu.bitcast(word, BF16).astype(F32)
    return stage_ref[pl.ds(base, rows, stride=2), :], stage_ref[pl.ds(base + 1, rows, stride=2), :]


def _mod_kernel(c_ref, w_ref, b_ref, o_ref):
    s = _silu(c_ref[...]).astype(BF16)
    o_ref[...] = jnp.dot(s, w_ref[...].astype(BF16), preferred_element_type=F32) + b_ref[...]


def _modulation(cond, w, b):
    rows, d = cond.shape
    n = w.shape[1]
    tn = 1024
    return pl.pallas_call(
        _mod_kernel,
        grid=(n // tn,),
        in_specs=[pl.BlockSpec((rows, d), lambda j: (0, 0)),
                  pl.BlockSpec((d, tn), lambda j: (0, j)),
                  pl.BlockSpec((1, tn), lambda j: (0, j))],
        out_specs=pl.BlockSpec((rows, tn), lambda j: (0, j)),
        out_shape=jax.ShapeDtypeStruct((rows, n), F32),
        compiler_params=_params(1, 40),
        name="modulation",
    )(cond, w, b)


def _prenorm_kernel(x_ref, shift_ref, scale_ref, g_ref, o_ref, *, tiles_per_row):
    r = pl.program_id(0) // tiles_per_row
    y = _rms(x_ref[...], g_ref[...])
    o_ref[...] = (y * (1.0 + scale_ref[pl.ds(r, 1), :]) + shift_ref[pl.ds(r, 1), :]).astype(o_ref.dtype)


def _prenorm(x, mod, g, shift_col, scale_col):
    t, d = x.shape
    nb = mod.shape[0]
    tm = 512
    return pl.pallas_call(
        functools.partial(_prenorm_kernel, tiles_per_row=t // nb // tm),
        grid=(t // tm,),
        in_specs=[pl.BlockSpec((tm, d), lambda i: (i, 0)),
                  pl.BlockSpec((nb, d), lambda i: (0, shift_col)),
                  pl.BlockSpec((nb, d), lambda i: (0, scale_col)),
                  pl.BlockSpec((1, d), lambda i: (0, 0))],
        out_specs=pl.BlockSpec((tm, d), lambda i: (i, 0)),
        out_shape=jax.ShapeDtypeStruct((t, d), BF16),
        compiler_params=_params(1, 32),
        name="prenorm",
    )(x, mod, mod, g)


def _mm_wcast_kernel(a_ref, w_ref, o_ref, wb_ref):
    @pl.when(pl.program_id(1) == 0)
    def _():
        wb_ref[...] = w_ref[...].astype(BF16)

    o_ref[...] = jnp.dot(a_ref[...], wb_ref[...], preferred_element_type=F32).astype(o_ref.dtype)


def _matmul_wcast(a, w, col_map, n_out, out_dtype):
    m, k = a.shape
    tn = 1024
    tm = 2048 if jnp.dtype(out_dtype).itemsize == 2 else 1024
    return pl.pallas_call(
        _mm_wcast_kernel,
        grid=(n_out // tn, m // tm),
        in_specs=[pl.BlockSpec((tm, k), lambda j, i: (i, 0)),
                  pl.BlockSpec((k, tn), lambda j, i: (0, col_map(j)))],
        out_specs=pl.BlockSpec((tm, tn), lambda j, i: (i, j)),
        out_shape=jax.ShapeDtypeStruct((m, n_out), out_dtype),
        scratch_shapes=[pltpu.VMEM((k, tn), BF16)],
        compiler_params=_params(2, 56),
        name="in_proj",
    )(a, w)


def _z_col_map(j):
    return jnp.where(j < 2, j + 3, jnp.where(j < 6, j + 4, j - 6))


def _qk_kernel(*refs, use_rope, n_heads, n_kv, q_scale):
    if use_rope:
        q_ref, kv_ref, gq_ref, gk_ref, cos_ref, sin_ref, qn_ref, kn_ref, vb_ref, k32_ref, v32_ref = refs
        cos = cos_ref[...]
        sin = sin_ref[...]
        even = lax.broadcasted_iota(I32, cos.shape, 1) % 2 == 0
    else:
        q_ref, kv_ref, gq_ref, gk_ref, qn_ref, kn_ref, vb_ref, k32_ref, v32_ref = refs

    def norm_rope(xh, g):
        y = _rms(xh, g)
        if use_rope:
            sw = jnp.where(even, pltpu.roll(y, LANES - 1, 1), pltpu.roll(y, 1, 1))
            y = y * cos + sw * sin
        return y

    gq = gq_ref[...]
    gk = gk_ref[...]
    for h in range(n_heads):
        sl = slice(h * LANES, (h + 1) * LANES)
        qn_ref[:, sl] = (norm_rope(q_ref[:, sl].astype(F32), gq) * q_scale).astype(BF16)
    kw = n_kv * LANES
    for h in range(n_kv):
        sl = slice(h * LANES, (h + 1) * LANES)
        kh = norm_rope(kv_ref[:, sl], gk)
        k32_ref[:, sl] = kh
        kn_ref[:, sl] = kh.astype(BF16)
    v = kv_ref[:, kw:2 * kw]
    v32_ref[...] = v
    vb_ref[...] = v.astype(BF16)


def _qk_prep(z, kv, gq, gk, rope, seq_len, n_heads, n_kv):
    t = z.shape[0]
    dq = n_heads * LANES
    dk = n_kv * LANES
    tm = 256
    in_specs = [pl.BlockSpec((tm, dq), lambda i: (i, 0)),
                pl.BlockSpec((tm, 2 * dk), lambda i: (i, 0)),
                pl.BlockSpec((1, LANES), lambda i: (0, 0)),
                pl.BlockSpec((1, LANES), lambda i: (0, 0))]
    args = [z, kv, gq, gk]
    if rope is not None:
        per_seq = seq_len // tm
        in_specs += [pl.BlockSpec((tm, LANES), lambda i: (i % per_seq, 0))] * 2
        args += list(rope)
    out_specs = [pl.BlockSpec((tm, dq), lambda i: (i, 0))] + [pl.BlockSpec((tm, dk), lambda i: (i, 0))] * 4
    out_shape = [jax.ShapeDtypeStruct((t, dq), BF16), jax.ShapeDtypeStruct((t, dk), BF16),
                 jax.ShapeDtypeStruct((t, dk), BF16), jax.ShapeDtypeStruct((t, dk), F32),
                 jax.ShapeDtypeStruct((t, dk), F32)]
    return pl.pallas_call(
        functools.partial(_qk_kernel, use_rope=rope is not None, n_heads=n_heads, n_kv=n_kv,
                          q_scale=LANES ** -0.5),
        grid=(t // tm,),
        in_specs=in_specs, out_specs=out_specs, out_shape=out_shape,
        compiler_params=_params(1, 32),
        name="qk_prep",
    )(*args)


def _attn_kernel(*refs, has_cache, group, chunks):
    if has_cache:
        q_ref, k_ref, v_ref, ck_ref, cv_ref, o_ref = refs
    else:
        q_ref, k_ref, v_ref, o_ref = refs
    tq = q_ref.shape[0]
    nt = (((1,), (1,)), ((), ()))
    if has_cache:
        ck = ck_ref[...].astype(BF16)
        cv = cv_ref[...].astype(BF16)
    per = group // chunks
    for c in range(chunks):
        heads = range(c * per, (c + 1) * per)
        q = jnp.concatenate([q_ref[:, g * LANES:(g + 1) * LANES] for g in heads], axis=0)
        s_own = lax.dot_general(q, k_ref[...], nt, preferred_element_type=F32)
        m = jnp.max(s_own, axis=-1, keepdims=True)
        if has_cache:
            s_ctx = lax.dot_general(q, ck, nt, preferred_element_type=F32)
            m = jnp.maximum(m, jnp.max(s_ctx, axis=-1, keepdims=True))
        p = jnp.exp(s_own - m)
        denom = jnp.sum(p, axis=-1, keepdims=True)
        acc = jnp.dot(p.astype(BF16), v_ref[...], preferred_element_type=F32)
        if has_cache:
            pc = jnp.exp(s_ctx - m)
            denom = denom + jnp.sum(pc, axis=-1, keepdims=True)
            acc = acc + jnp.dot(pc.astype(BF16), cv, preferred_element_type=F32)
        o = acc / denom
        for n, g in enumerate(heads):
            o_ref[:, g * LANES:(g + 1) * LANES] = o[n * tq:(n + 1) * tq].astype(o_ref.dtype)


def _attention(qn, kn, vb, cache, batch, seq_len, n_heads, n_kv, tq):
    t = qn.shape[0]
    group = n_heads // n_kv
    nq = seq_len // tq
    in_specs = [pl.BlockSpec((tq, group * LANES), lambda b, h, i: (b * nq + i, h)),
                pl.BlockSpec((seq_len, LANES), lambda b, h, i: (b, h)),
                pl.BlockSpec((seq_len, LANES), lambda b, h, i: (b, h))]
    args = [qn, kn, vb]
    if cache is not None:
        past = cache[0].shape[1]
        in_specs += [pl.BlockSpec((None, past, LANES), lambda b, h, i: (b, 0, h))] * 2
        args += list(cache)
    return pl.pallas_call(
        functools.partial(_attn_kernel, has_cache=cache is not None, group=group,
                          chunks=group if cache is not None else group // 2),
        grid=(batch, n_kv, nq),
        in_specs=in_specs,
        out_specs=pl.BlockSpec((tq, group * LANES), lambda b, h, i: (b * nq + i, h)),
        out_shape=jax.ShapeDtypeStruct((t, n_heads * LANES), BF16),
        compiler_params=_params(3, 40),
        name="attention",
    )(*args)


CONV_HALO = 16


def _conv_kernel(u_ref, b_ref, c_ref, up_ref, cp_ref, un_ref, cn_ref, w_ref, bias_ref, o_ref, *,
                 tiles_per_seq):
    tm = u_ref.shape[0]
    pos = pl.program_id(0) % tiles_per_seq
    cu = c_ref[...].astype(F32) * u_ref[...].astype(F32)
    halo_prev = (cp_ref[...].astype(F32) * up_ref[...].astype(F32))[CONV_HALO - 1:CONV_HALO, :]
    halo_next = (cn_ref[...].astype(F32) * un_ref[...].astype(F32))[0:1, :]
    halo_prev = jnp.where(pos == 0, 0.0, halo_prev)
    halo_next = jnp.where(pos == tiles_per_seq - 1, 0.0, halo_next)
    row = lax.broadcasted_iota(I32, cu.shape, 0)
    prev = jnp.where(row == 0, halo_prev, pltpu.roll(cu, 1, 0))
    nxt = jnp.where(row == tm - 1, halo_next, pltpu.roll(cu, tm - 1, 0))
    w = w_ref[...]
    conv = prev * w[0:1, :] + cu * w[1:2, :] + nxt * w[2:3, :] + bias_ref[...]
    o_ref[...] = (b_ref[...].astype(F32) * conv).astype(o_ref.dtype)


def _gated_conv(z, conv_w, conv_b, seq_len, col0):
    t = z.shape[0]
    dc = conv_w.shape[1]
    tm = 256
    hb = tm // CONV_HALO
    last = t // CONV_HALO - 1
    prev_map = lambda c: (lambda i: (jnp.maximum(i * hb - 1, 0), c))
    next_map = lambda c: (lambda i: (jnp.minimum((i + 1) * hb, last), c))
    in_specs = [pl.BlockSpec((tm, dc), lambda i: (i, col0)),
                pl.BlockSpec((tm, dc), lambda i: (i, col0 + 1)),
                pl.BlockSpec((tm, dc), lambda i: (i, col0 + 2)),
                pl.BlockSpec((CONV_HALO, dc), prev_map(col0)),
                pl.BlockSpec((CONV_HALO, dc), prev_map(col0 + 2)),
                pl.BlockSpec((CONV_HALO, dc), next_map(col0)),
                pl.BlockSpec((CONV_HALO, dc), next_map(col0 + 2)),
                pl.BlockSpec(conv_w.shape, lambda i: (0, 0)),
                pl.BlockSpec((1, dc), lambda i: (0, 0))]
    return pl.pallas_call(
        functools.partial(_conv_kernel, tiles_per_seq=seq_len // tm),
        grid=(t // tm,),
        in_specs=in_specs,
        out_specs=pl.BlockSpec((tm, dc), lambda i: (i, 0)),
        out_shape=jax.ShapeDtypeStruct((t, dc), BF16),
        compiler_params=_params(1, 32),
        name="gated_conv",
    )(z, z, z, z, z, z, z, conv_w, conv_b)


def _merge_kernel(cv_ref, at_ref, gc_ref, ga_ref, wc_ref, wa_ref, o_ref, wcb_ref, wab_ref):
    @pl.when(pl.program_id(1) == 0)
    def _():
        wcb_ref[...] = wc_ref[...].astype(BF16)
        wab_ref[...] = wa_ref[...].astype(BF16)

    conv_out = jnp.dot(cv_ref[...], wcb_ref[...], preferred_element_type=F32)
    attn_out = jnp.dot(at_ref[...], wab_ref[...], preferred_element_type=F32)
    merged = (jax.nn.sigmoid(gc_ref[...].astype(F32)) * conv_out
              + jax.nn.sigmoid(ga_ref[...].astype(F32)) * attn_out)
    o_ref[...] = merged.astype(o_ref.dtype)


def _merge(cvg, attn, z, w_conv_out, w_attn_out, gc_col0, ga_col0):
    t, dc = cvg.shape
    dq = attn.shape[1]
    d = w_conv_out.shape[1]
    tm, tn = 512, 512
    return pl.pallas_call(
        _merge_kernel,
        grid=(d // tn, t // tm),
        in_specs=[pl.BlockSpec((tm, dc), lambda j, i: (i, 0)),
                  pl.BlockSpec((tm, dq), lambda j, i: (i, 0)),
                  pl.BlockSpec((tm, tn), lambda j, i: (i, gc_col0 + j)),
                  pl.BlockSpec((tm, tn), lambda j, i: (i, ga_col0 + j)),
                  pl.BlockSpec((dc, tn), lambda j, i: (0, j)),
                  pl.BlockSpec((dq, tn), lambda j, i: (0, j))],
        out_specs=pl.BlockSpec((tm, tn), lambda j, i: (i, j)),
        out_shape=jax.ShapeDtypeStruct((t, d), BF16),
        scratch_shapes=[pltpu.VMEM((dc, tn), BF16), pltpu.VMEM((dq, tn), BF16)],
        compiler_params=_params(2, 40),
        name="merge",
    )(cvg, attn, z, z, w_conv_out, w_attn_out)


def _route(logits_t, bias_col, carry):
    n_exp, tm = logits_t.shape
    per = n_exp // N_GROUPS
    assert per == SUBLANES
    neg = -jnp.inf
    scores = jax.nn.sigmoid(logits_t)
    biased = scores + bias_col
    sub = lax.broadcasted_iota(I32, (per, tm), 0).astype(F32)
    xs = [biased[g * per:(g + 1) * per, :] for g in range(N_GROUPS)]
    sc = [scores[g * per:(g + 1) * per, :] for g in range(N_GROUPS)]
    ids = [sub + float(g * per) for g in range(N_GROUPS)]

    def colmax(a):
        return jnp.max(a, axis=0, keepdims=True)

    def colmin(a):
        return jnp.min(a, axis=0, keepdims=True)

    rows = []
    for g in range(N_GROUPS):
        m1 = colmax(xs[g])
        j1 = colmin(jnp.where(xs[g] == m1, sub, float(per)))
        m2 = colmax(jnp.where(sub == j1, neg, xs[g]))
        rows.append(m1 + m2)
    gs = jnp.concatenate(rows, axis=0)
    gsel = jnp.zeros_like(gs)
    for _ in range(TOPK_GROUPS):
        m = colmax(gs)
        j = colmin(jnp.where(gs == m, sub, float(N_GROUPS)))
        hit = sub == j
        gsel = jnp.where(hit, 1.0, gsel)
        gs = jnp.where(hit, neg, gs)
    masked = [jnp.where(gsel[g:g + 1, :] > 0.0, xs[g], neg) for g in range(N_GROUPS)]
    idx_rows, w_rows = [], []
    member = [jnp.zeros((per, tm), F32) for _ in range(N_GROUPS)]
    for _ in range(TOP_K):
        mm = masked[0]
        for g in range(1, N_GROUPS):
            mm = jnp.maximum(mm, masked[g])
        m = colmax(mm)
        idx = colmin(jnp.where(masked[0] == m, ids[0], float(n_exp)))
        for g in range(1, N_GROUPS):
            idx = jnp.minimum(idx, colmin(jnp.where(masked[g] == m, ids[g], float(n_exp))))
        wk = jnp.zeros_like(idx)
        for g in range(N_GROUPS):
            hit = ids[g] == idx
            wk = wk + jnp.sum(jnp.where(hit, sc[g], 0.0), axis=0, keepdims=True)
            masked[g] = jnp.where(hit, neg, masked[g])
            member[g] = jnp.where(hit, 1.0, member[g])
        idx_rows.append(idx)
        w_rows.append(wk)
    w = jnp.concatenate(w_rows, axis=0)
    w = w / jnp.sum(w, axis=0, keepdims=True) * ROUTED_SCALE

    earlier = (lax.broadcasted_iota(I32, (tm, tm), 0) < lax.broadcasted_iota(I32, (tm, tm), 1)).astype(BF16)
    before = jnp.dot(jnp.concatenate(member, axis=0).astype(BF16), earlier, preferred_element_type=F32)
    rank_rows = []
    for k in range(TOP_K):
        rk = jnp.zeros_like(idx_rows[k])
        for g in range(N_GROUPS):
            pos = before[g * per:(g + 1) * per, :] + carry[g]
            rk = rk + jnp.sum(jnp.where(ids[g] == idx_rows[k], pos, 0.0), axis=0, keepdims=True)
        rank_rows.append(rk)
    new_carry = [carry[g] + jnp.sum(member[g], axis=1, keepdims=True) for g in range(N_GROUPS)]
    idx = jnp.concatenate(idx_rows, axis=0).astype(I32)
    rank = jnp.concatenate(rank_rows, axis=0).astype(I32)
    return idx, w, rank, new_carry


def _out_kernel(mg_ref, x_ref, gate_ref, shift_ref, scale_ref, gpost_ref, gpre_ref, wo_ref, wr_ref,
                x1_ref, h2b_ref, h2p_ref, logit_ref, stage_ref, *, tiles_per_row):
    tm = x_ref.shape[0]
    r = pl.program_id(0) // tiles_per_row
    gate = gate_ref[pl.ds(r, 1), :]
    scale = 1.0 + scale_ref[pl.ds(r, 1), :]
    shift = shift_ref[pl.ds(r, 1), :]
    rows = tm // OUT_CHUNKS
    for c in range(OUT_CHUNKS):
        sl = pl.ds(c * rows, rows)
        mix = jnp.dot(mg_ref[sl, :], wo_ref[...], preferred_element_type=F32)
        x1 = x_ref[sl, :] + gate * _rms(mix, gpost_ref[...])
        x1_ref[sl, :] = x1
        h2 = _rms(x1, gpre_ref[...]) * scale + shift
        h2b_ref[sl, :] = h2.astype(BF16)
        _pack_rows(h2, h2p_ref, stage_ref, rows, row0=c * rows * SUBLANES, stage0=c * rows * 2 * SUBLANES)
        logit_ref[:, sl] = lax.dot_general(wr_ref[...], h2, (((1,), (1,)), ((), ())),
                                           preferred_element_type=F32, precision=lax.Precision.HIGHEST)


OUT_CHUNKS = 4


def _out_proj(merged, x, mod, g_post, g_pre, w_out_b, w_router_t):
    t, d = x.shape
    nb = mod.shape[0]
    n_exp = w_router_t.shape[0]
    tm = 512
    row = lambda i: (i, 0)
    fixed = lambda i: (0, 0)
    once = pl.Buffered(1)
    in_specs = [pl.BlockSpec((tm, d), row), pl.BlockSpec((tm, d), row),
                pl.BlockSpec((nb, d), lambda i: (0, 2)),
                pl.BlockSpec((nb, d), lambda i: (0, 3)),
                pl.BlockSpec((nb, d), lambda i: (0, 4)),
                pl.BlockSpec((1, d), fixed), pl.BlockSpec((1, d), fixed),
                pl.BlockSpec((d, d), fixed, pipeline_mode=once),
                pl.BlockSpec((n_exp, d), fixed, pipeline_mode=once)]
    out_specs = [pl.BlockSpec((tm, d), row), pl.BlockSpec((tm, d), row),
                 pl.BlockSpec((tm * SUBLANES, LANES), row),
                 pl.BlockSpec((n_exp, tm), lambda i: (0, i))]
    out_shape = [jax.ShapeDtypeStruct((t, d), F32), jax.ShapeDtypeStruct((t, d), BF16),
                 jax.ShapeDtypeStruct((t * SUBLANES, LANES), I32),
                 jax.ShapeDtypeStruct((n_exp, t), F32)]
    return pl.pallas_call(
        functools.partial(_out_kernel, tiles_per_row=t // nb // tm),
        grid=(t // tm,),
        in_specs=in_specs, out_specs=out_specs, out_shape=out_shape,
        scratch_shapes=[pltpu.VMEM((SUBLANES * 2 * tm, LANES), F32)],
        compiler_params=_params(1, 56),
        name="out_proj",
    )(merged, x, mod, mod, mod, g_post, g_pre, w_out_b, w_router_t)


def _router_kernel(logit_ref, br_ref, idx_ref, wsel_ref, rank_ref, cnt_ref):
    per = SUBLANES

    @pl.when(pl.program_id(0) == 0)
    def _():
        cnt_ref[...] = jnp.zeros_like(cnt_ref)

    carry = [cnt_ref[g * per:(g + 1) * per, 0:1] for g in range(N_GROUPS)]
    idx, w, rank, carry = _route(logit_ref[...], br_ref[...], carry)
    idx_ref[...] = idx
    wsel_ref[...] = w
    rank_ref[...] = rank
    for g in range(N_GROUPS):
        cnt_ref[g * per:(g + 1) * per, :] = jnp.broadcast_to(carry[g], (per, LANES))


def _router(logits_t, b_router_col):
    n_exp, t = logits_t.shape
    tr = 1024
    tile = lambda i: (0, i)
    fixed = lambda i: (0, 0)
    return pl.pallas_call(
        _router_kernel,
        grid=(t // tr,),
        in_specs=[pl.BlockSpec((n_exp, tr), tile), pl.BlockSpec((n_exp, 1), fixed)],
        out_specs=[pl.BlockSpec((TOP_K, tr), tile), pl.BlockSpec((TOP_K, tr), tile),
                   pl.BlockSpec((TOP_K, tr), tile), pl.BlockSpec((n_exp, LANES), fixed)],
        out_shape=[jax.ShapeDtypeStruct((TOP_K, t), I32), jax.ShapeDtypeStruct((TOP_K, t), F32),
                   jax.ShapeDtypeStruct((TOP_K, t), I32), jax.ShapeDtypeStruct((n_exp, LANES), F32)],
        compiler_params=_params(1, 32),
        name="router",
    )(logits_t, b_router_col)


def _dest_kernel(pstart_ref, idx_ref, rank_ref, o_ref):
    idx = idx_ref[...]
    acc = rank_ref[...]
    for e in range(pstart_ref.shape[0]):
        acc = acc + jnp.where(idx == e, pstart_ref[e], 0)
    o_ref[...] = acc


def _dest_slots(pstart, idx, rank):
    return pl.pallas_call(
        _dest_kernel,
        in_specs=[pl.BlockSpec(memory_space=pltpu.SMEM),
                  pl.BlockSpec(memory_space=pltpu.VMEM), pl.BlockSpec(memory_space=pltpu.VMEM)],
        out_specs=pl.BlockSpec(memory_space=pltpu.VMEM),
        out_shape=jax.ShapeDtypeStruct(idx.shape, I32),
        name="dest_slots",
    )(pstart, idx, rank)


def _slot_table_kernel(pad_lo_ref, pad_hi_ref, dest_ref, tab_ref):
    i = pl.program_id(0)
    tm = dest_ref.shape[2]

    @pl.when(i == 0)
    def _():
        def clear_range(e, carry):
            def clear(s, c):
                tab_ref[s] = 0
                return c
            return lax.fori_loop(pad_lo_ref[e], pad_hi_ref[e], clear, carry)
        lax.fori_loop(0, pad_lo_ref.shape[0], clear_range, 0)

    def fill(t, carry):
        for k in range(TOP_K):
            tab_ref[dest_ref[0, k, t]] = (i * tm + t) * TOP_K + k
        return carry

    lax.fori_loop(0, tm, fill, 0)


def _slot_table(pad_lo, pad_hi, dest3, n_slots):
    nt, _, tm = dest3.shape
    grid_spec = pltpu.PrefetchScalarGridSpec(
        num_scalar_prefetch=2,
        grid=(nt,),
        in_specs=[pl.BlockSpec((1, TOP_K, tm), lambda i, lo, hi: (i, 0, 0), memory_space=pltpu.SMEM)],
        out_specs=pl.BlockSpec(memory_space=pltpu.SMEM),
    )
    return pl.pallas_call(
        _slot_table_kernel,
        grid_spec=grid_spec,
        out_shape=jax.ShapeDtypeStruct((n_slots,), I32),
        compiler_params=_params(1, 16),
        name="slot_table",
    )(pad_lo, pad_hi, dest3)


GATHER_UNROLL = 16
GATHER_BLOCK = 1024
assert TOP_K == SUBLANES


def _gather_kernel(tab_ref, src_ref, o_ref):
    n = tab_ref.shape[2]

    def move(c, carry):
        for u in range(GATHER_UNROLL):
            r = c * GATHER_UNROLL + u
            src = pl.multiple_of(tab_ref[0, 0, r] & -SUBLANES, SUBLANES)
            dst = pl.multiple_of(r * SUBLANES, SUBLANES)
            o_ref[pl.ds(dst, SUBLANES), :] = src_ref[pl.ds(src, SUBLANES), :]
        return carry

    lax.fori_loop(0, n // GATHER_UNROLL, move, 0)


def _gather_rows(slot_tab, h2p, gb):
    n_slots = slot_tab.shape[0]
    nblk = n_slots // gb
    resident = h2p.size * h2p.dtype.itemsize
    return pl.pallas_call(
        _gather_kernel,
        grid=(nblk,),
        in_specs=[pl.BlockSpec((1, 1, gb), lambda b: (b, 0, 0), memory_space=pltpu.SMEM),
                  pl.BlockSpec(memory_space=pltpu.VMEM)],
        out_specs=pl.BlockSpec((gb * SUBLANES, LANES), lambda b: (b, 0)),
        out_shape=jax.ShapeDtypeStruct((n_slots * SUBLANES, LANES), I32),
        compiler_params=_params(1, resident // MIB + 8),
        name="dispatch_gather",
    )(slot_tab.reshape(nblk, 1, gb), h2p)


ROW_DMA_PRIORITY = 1
EXPERT_CHUNKS = 1
X_SLOTS = 6
Y_SLOTS = 3


def _expert_kernel(first_ref, nblk_ref, xs_hbm, wg_ref, wu_ref, wd_ref, yb_hbm,
                   xbuf_ref, obuf_ref, in_sem, out_sem, wgb_ref, wub_ref, wdb_ref, stage_in_ref, stage_out_ref,
                   *, bm):
    e = pl.program_id(0)
    nb = nblk_ref[e]
    b0 = first_ref[e]
    rows = bm * SUBLANES

    def x_copy(j, slot, base=None):
        blk = (b0 if base is None else base) + j
        return pltpu.make_async_copy(
            xs_hbm.at[pl.ds(pl.multiple_of(blk * rows, rows), rows), :],
            xbuf_ref.at[pl.ds(pl.multiple_of(slot * rows, rows), rows), :], in_sem.at[slot])

    def y_copy(blk):
        slot = lax.rem(blk, Y_SLOTS)
        return pltpu.make_async_copy(
            obuf_ref.at[pl.ds(pl.multiple_of(slot * rows, rows), rows), :],
            yb_hbm.at[pl.ds(pl.multiple_of(blk * rows, rows), rows), :], out_sem.at[slot])

    def start_head(count, base=None):
        for j in range(X_SLOTS - 1):
            @pl.when(count > j)
            def _(j=j):
                x_copy(j, j, base).start(priority=ROW_DMA_PRIORITY)

    @pl.when(e == 0)
    def _():
        start_head(nb)

    @pl.when(nb > 0)
    def _():
        wgb_ref[...] = wg_ref[...].astype(BF16)
        wub_ref[...] = wu_ref[...].astype(BF16)
        wdb_ref[...] = wd_ref[...].astype(BF16)

    def block(j, carry):
        xslot = lax.rem(j, X_SLOTS)
        yblk = b0 + j
        yslot = lax.rem(yblk, Y_SLOTS)
        x_copy(j, xslot).wait()

        ahead = j + (X_SLOTS - 1)

        @pl.when(ahead < nb)
        def _():
            x_copy(ahead, lax.rem(ahead, X_SLOTS)).start(priority=ROW_DMA_PRIORITY)

        @pl.when(yblk >= Y_SLOTS)
        def _():
            y_copy(yblk - Y_SLOTS).wait()

        xbase = xslot * rows
        ybase = yslot * rows
        sub = bm // EXPERT_CHUNKS
        for c in range(EXPERT_CHUNKS):
            cbase = xbase + c * sub * SUBLANES
            obase = ybase + c * sub * SUBLANES
            los, his = [], []
            for s in range(SUBLANES):
                lo, hi = _unpack_word(xbuf_ref[pl.ds(cbase + s, sub, stride=SUBLANES), :], stage_in_ref,
                                      c * SUBLANES + s)
                los.append(lo.astype(BF16))
                his.append(hi.astype(BF16))
            x = jnp.concatenate(los + his, axis=1)
            g = jnp.dot(x, wgb_ref[...], preferred_element_type=F32)
            u = jnp.dot(x, wub_ref[...], preferred_element_type=F32)
            a = (_silu(g) * u).astype(BF16)
            y = jnp.dot(a, wdb_ref[...], preferred_element_type=F32)
            _pack_rows(y, obuf_ref, stage_out_ref, sub, row0=obase, stage0=c * sub * 2 * SUBLANES)
        y_copy(yblk).start()
        return carry

    lax.fori_loop(0, nb, block, 0)

    @pl.when(e + 1 < pl.num_programs(0))
    def _():
        nxt = jnp.minimum(e + 1, pl.num_programs(0) - 1)
        start_head(nblk_ref[nxt], first_ref[nxt])

    @pl.when(e == pl.num_programs(0) - 1)
    def _():
        total = b0 + nb
        for back in range(Y_SLOTS, 0, -1):
            @pl.when(total >= back)
            def _(back=back):
                y_copy(total - back).wait()

        obuf_ref[pl.ds(0, rows), :] = pltpu.bitcast(jnp.zeros((2 * rows, LANES), BF16), I32)

        def clear(j, carry):
            cp = pltpu.make_async_copy(obuf_ref.at[pl.ds(0, rows), :],
                                       yb_hbm.at[pl.ds(pl.multiple_of(j * rows, rows), rows), :], out_sem.at[0])
            cp.start()
            cp.wait()
            return carry

        lax.fori_loop(b0 + nb, yb_hbm.shape[0] // rows, clear, 0)


def _experts(first_block, n_blocks_e, xs, w_gate, w_up, w_down, bm):
    n_exp, d, de = w_gate.shape
    rows = bm * SUBLANES
    grid_spec = pltpu.PrefetchScalarGridSpec(
        num_scalar_prefetch=2,
        grid=(n_exp,),
        in_specs=[pl.BlockSpec(memory_space=pl.ANY),
                  pl.BlockSpec((None, d, de), lambda e, fb, nb: (e, 0, 0)),
                  pl.BlockSpec((None, d, de), lambda e, fb, nb: (e, 0, 0)),
                  pl.BlockSpec((None, de, d), lambda e, fb, nb: (e, 0, 0))],
        out_specs=pl.BlockSpec(memory_space=pl.ANY),
        scratch_shapes=[pltpu.VMEM((X_SLOTS * rows, LANES), I32), pltpu.VMEM((Y_SLOTS * rows, LANES), I32),
                        pltpu.SemaphoreType.DMA((X_SLOTS,)), pltpu.SemaphoreType.DMA((Y_SLOTS,)),
                        pltpu.VMEM((d, de), BF16), pltpu.VMEM((d, de), BF16), pltpu.VMEM((de, d), BF16),
                        pltpu.VMEM((SUBLANES * 2 * bm, LANES), F32), pltpu.VMEM((SUBLANES * 2 * bm, LANES), F32)],
    )
    return pl.pallas_call(
        functools.partial(_expert_kernel, bm=bm),
        grid_spec=grid_spec,
        out_shape=jax.ShapeDtypeStruct(xs.shape, I32),
        compiler_params=_params(1, 56),
        name="routed_experts",
    )(first_block, n_blocks_e, xs, w_gate, w_up, w_down)


def _final_kernel(dest_ref, w_ref, yb_hbm, x1_ref, h2_ref, gate_ref, gpost_ref, wsg_ref, wsu_ref,
                  wsd_ref, o_ref, buf_ref, sem, stage_ref, *, steps_per_row):
    tm = x1_ref.shape[0] // 2
    i = pl.program_id(0)
    last = pl.num_programs(0) - 1
    r = i // steps_per_row
    tile_rows = TOP_K * tm * SUBLANES
    gate = gate_ref[pl.ds(r, 1), :]

    def request(tile, slot, t, k):
        src = pl.multiple_of(dest_ref[tile, k, t], SUBLANES)
        dst = slot * tile_rows + (k * tm + t) * SUBLANES
        pltpu.make_async_copy(yb_hbm.at[pl.ds(src, SUBLANES), :], buf_ref.at[pl.ds(dst, SUBLANES), :],
                              sem.at[slot]).start(priority=k % 2)

    def wait_tile(slot):
        pltpu.make_async_copy(yb_hbm.at[pl.ds(0, tile_rows), :],
                              buf_ref.at[pl.ds(slot * tile_rows, tile_rows), :], sem.at[slot]).wait()

    @pl.when(i == 0)
    def _():
        def first(t, carry):
            for k in range(TOP_K):
                request(0, 0, t, k)
            return carry
        lax.fori_loop(0, tm, first, 0)

    pending = [(t, k) for t in range(tm) for k in range(TOP_K)]
    per_piece = 2 * len(pending) // (SUBLANES * TOP_K)

    def combine(half, slot, next_tile):
        rows = pl.ds(half * tm, tm)
        h = h2_ref[rows, :]
        g = jnp.dot(h, wsg_ref[...], preferred_element_type=F32)
        u = jnp.dot(h, wsu_ref[...], preferred_element_type=F32)
        shared = jnp.dot((_silu(g) * u).astype(BF16), wsd_ref[...], preferred_element_type=F32)
        w = w_ref[rows, :]
        base = slot * tile_rows
        los, his = [], []
        for s in range(SUBLANES):
            lo_acc = jnp.zeros((tm, LANES), F32)
            hi_acc = jnp.zeros((tm, LANES), F32)
            for k in range(TOP_K):
                lo, hi = _unpack_word(buf_ref[pl.ds(base + k * tm * SUBLANES + s, tm, stride=SUBLANES), :],
                                      stage_ref, s * TOP_K + k)
                wk = w[:, k:k + 1]
                lo_acc = lo_acc + wk * lo
                hi_acc = hi_acc + wk * hi
                n = s * TOP_K + k
                for t, kk in pending[n * per_piece:(n + 1) * per_piece]:
                    request(next_tile, 1 - slot, t, kk)
            los.append(lo_acc)
            his.append(hi_acc)
        ffn = jnp.concatenate(los + his, axis=1) + shared
        o_ref[rows, :] = x1_ref[rows, :] + gate * _rms(ffn, gpost_ref[...])

    wait_tile(0)
    combine(0, 0, 2 * i + 1)
    wait_tile(1)
    combine(1, 1, jnp.minimum(2 * i + 2, 2 * last + 1))

    @pl.when(i == last)
    def _():
        wait_tile(0)


def _final(dest8, w_tok, yb, x1, h2b, mod, g_post, ws_gate, ws_up, ws_down):
    t, d = x1.shape
    nb = mod.shape[0]
    ds_ = ws_gate.shape[1]
    tm = dest8.shape[2]
    n_steps = t // (2 * tm)
    row = lambda i, dst: (i, 0)
    fixed = lambda i, dst: (0, 0)
    grid_spec = pltpu.PrefetchScalarGridSpec(
        num_scalar_prefetch=1,
        grid=(n_steps,),
        in_specs=[pl.BlockSpec((2 * tm, TOP_K), row),
                  pl.BlockSpec(memory_space=pl.ANY),
                  pl.BlockSpec((2 * tm, d), row), pl.BlockSpec((2 * tm, d), row),
                  pl.BlockSpec((nb, d), lambda i, dst: (0, 5)),
                  pl.BlockSpec((1, d), fixed),
                  pl.BlockSpec((d, ds_), fixed), pl.BlockSpec((d, ds_), fixed), pl.BlockSpec((ds_, d), fixed)],
        out_specs=pl.BlockSpec((2 * tm, d), row),
        scratch_shapes=[pltpu.VMEM((2 * TOP_K * tm * SUBLANES, LANES), I32), pltpu.SemaphoreType.DMA((2,)),
                        pltpu.VMEM((SUBLANES * TOP_K * 2 * tm, LANES), F32)],
    )
    return pl.pallas_call(
        functools.partial(_final_kernel, steps_per_row=t // nb // (2 * tm)),
        grid_spec=grid_spec,
        out_shape=jax.ShapeDtypeStruct((t, d), F32),
        compiler_params=_params(1, 48),
        name="combine_final",
    )(dest8, w_tok, yb, x1, h2b, mod, g_post, ws_gate, ws_up, ws_down)


def _rope_tables(n_pos):
    half = LANES // 2
    pos = jnp.arange(n_pos)
    row = (pos // GRID_W).astype(F32)
    col = (pos % GRID_W).astype(F32)
    inv = ROPE_THETA ** (-jnp.arange(0, half, 2, dtype=F32) / half)
    ang = jnp.concatenate([row[:, None] * inv, col[:, None] * inv], axis=-1)
    cos = jnp.repeat(jnp.cos(ang), 2, axis=-1)
    sin = jnp.repeat(jnp.sin(ang), 2, axis=-1) * jnp.tile(jnp.array([-1.0, 1.0], F32), half)
    return cos, sin


def _sublayer1(x, mod, p, seq_len, rope, cache, n_heads, n_kv, tq):
    t = x.shape[0]
    h = _prenorm(x, mod, p['g_pre1'], 0, 1)
    z = _matmul_wcast(h, p['w_in'], _z_col_map, 9 * 1024, BF16)
    kv = _matmul_wcast(h, p['w_in'], lambda j: j + 5, 1024, F32)
    qn, kn, vb, k32, v32 = _qk_prep(z, kv, p['g_q'], p['g_k'], rope, seq_len, n_heads, n_kv)
    attn = _attention(qn, kn, vb, cache, t // seq_len, seq_len, n_heads, n_kv, tq)
    cvg = _gated_conv(z, p['conv_w'], p['conv_b'], seq_len, 6)
    merged = _merge(cvg, attn, z, p['w_conv_out'], p['w_attn_out'], 4, 8)
    x1, h2b, h2p, logits_t = _out_proj(merged, x, mod, p['g_post1'], p['g_pre2'], p['w_out_b'], p['w_router_t'])
    return x1, h2b, h2p, logits_t, k32, v32


def kernel(x_prompt, x_sample, cache_k, cache_v, c, c_ctx, w_mod, b_mod, g_pre1, w_in, conv_w, conv_b, g_q, g_k, w_conv_out, w_attn_out, w_out, g_post1, g_pre2, w_router, b_router, w_e_gate, w_e_up, w_e_down, w_s_gate, w_s_up, w_s_down, g_post2):
    batch, seq, d = x_prompt.shape
    dec_batch, dec_seq, _ = x_sample.shape
    depth = w_mod.shape[0]
    assert depth == 1
    past, n_kv, head_dim = cache_k.shape[2:]
    assert head_dim == LANES
    n_heads = w_attn_out.shape[1] // head_dim
    n_exp = w_router.shape[2]
    t_ctx, t_lat = batch * seq, dec_batch * dec_seq
    t_all = t_ctx + t_lat
    l = 0

    p = {
        'g_pre1': g_pre1[l][None], 'w_in': w_in[l], 'conv_w': conv_w[l], 'conv_b': conv_b[l][None],
        'g_q': g_q[l][None], 'g_k': g_k[l][None], 'w_conv_out': w_conv_out[l], 'w_attn_out': w_attn_out[l],
        'w_out_b': w_out[l].astype(BF16), 'g_post1': g_post1[l][None], 'g_pre2': g_pre2[l][None],
        'w_router_t': w_router[l].T,
    }
    cond = jnp.concatenate([c_ctx[None], c, jnp.zeros((SUBLANES - 1 - dec_batch, d), F32)], axis=0)
    mod = _modulation(cond, w_mod[l], b_mod[l][None])
    mod_ctx, mod_lat = mod[0:1], mod[1:1 + dec_batch]

    xc = x_prompt.reshape(t_ctx, d)
    xl = x_sample.reshape(t_lat, d)
    cache = (cache_k[:, l].reshape(dec_batch, past, n_kv * head_dim),
             cache_v[:, l].reshape(dec_batch, past, n_kv * head_dim))
    rope = _rope_tables(dec_seq)

    x1c, h2bc, h2pc, logc, k32, v32 = _sublayer1(xc, mod_ctx, p, seq, None, None, n_heads, n_kv, 256)
    x1l, h2bl, h2pl, logl, _, _ = _sublayer1(xl, mod_lat, p, dec_seq, rope, cache, n_heads, n_kv, 256)
    idx_all, w_all, rank_all, cnt = _router(jnp.concatenate([logc, logl], axis=1), b_router[l][:, None])

    bm = 256
    n_blocks = (t_all * TOP_K + n_exp * (bm - 1)) // bm + 1
    n_slots = n_blocks * bm
    blocks_e = jnp.floor((cnt[:, 0] + (bm - 1)) / bm)
    first_e = jnp.cumsum(blocks_e) - blocks_e
    pstart = (first_e * bm).astype(I32)

    tmf = 128
    w_tok = w_all.T
    dest = _dest_slots(pstart, idx_all, rank_all)
    dest3 = dest.reshape(TOP_K, t_all // tmf, tmf).transpose(1, 0, 2)
    used_end = jnp.sum(blocks_e, keepdims=True) * bm
    pad_lo = jnp.concatenate([first_e * bm + cnt[:, 0], used_end]).astype(I32)
    pad_hi = jnp.concatenate([(first_e + blocks_e) * bm, jnp.full((1,), float(n_slots), F32)]).astype(I32)
    slot_tab = _slot_table(pad_lo, pad_hi, dest3, n_slots)

    h2p = jnp.concatenate([h2pc, h2pl], axis=0)
    xs = _gather_rows(slot_tab, h2p, GATHER_BLOCK)
    yb = _experts(first_e.astype(I32), blocks_e.astype(I32), xs, w_e_gate[l], w_e_up[l], w_e_down[l], bm)

    ws = (w_s_gate[l].astype(BF16), w_s_up[l].astype(BF16), w_s_down[l].astype(BF16))

    def finish(lo, n, x1, h2b, mod_g):
        dest_g = dest3[lo // tmf:(lo + n) // tmf] * SUBLANES
        return _final(dest_g, w_tok[lo:lo + n], yb, x1, h2b, mod_g, g_post2[l][None], *ws)

    y_ctx = finish(0, t_ctx, x1c, h2bc, mod_ctx)
    y_lat = finish(t_ctx, t_lat, x1l, h2bl, mod_lat)

    new_k = k32.reshape(batch, 1, seq, n_kv, head_dim)
    new_v = v32.reshape(batch, 1, seq, n_kv, head_dim)
    return (y_ctx.reshape(batch, seq, d), y_lat.reshape(dec_batch, dec_seq, d), new_k, new_v)
```

```python
import functools

import jax
import jax.numpy as jnp
from jax import lax
from jax.experimental import pallas as pl
from jax.experimental.pallas import tpu as pltpu

GRID_W = 64
ROPE_THETA = 10000.0
N_GROUPS = 8
TOPK_GROUPS = 4
TOP_K = 8
ROUTED_SCALE = 2.5
EPS = 1e-6

LANES = 128
SUBLANES = 8
V7X_VMEM_BYTES = 64 * 1024 * 1024
MIB = 1024 * 1024

F32 = jnp.float32
BF16 = jnp.bfloat16
I32 = jnp.int32


def _params(n_grid, vmem_mib):
    assert vmem_mib * MIB < V7X_VMEM_BYTES
    return pltpu.CompilerParams(
        dimension_semantics=("arbitrary",) * n_grid, vmem_limit_bytes=vmem_mib * MIB)


def _silu(x):
    return x * jax.nn.sigmoid(x)


def _rms(x, g):
    return x * lax.rsqrt(jnp.mean(x * x, axis=-1, keepdims=True) + EPS) * g


def _pack_rows(val, out_ref, stage_ref, rows, row0=0, stage0=0):
    half = val.shape[1] // 2
    for s in range(half // LANES):
        base = stage0 + s * 2 * rows
        stage_ref[pl.ds(base, rows, stride=2), :] = val[:, s * LANES:(s + 1) * LANES]
        stage_ref[pl.ds(base + 1, rows, stride=2), :] = val[:, half + s * LANES:half + (s + 1) * LANES]
        pair = stage_ref[pl.ds(base, 2 * rows), :].astype(BF16)
        out_ref[pl.ds(row0 + s, rows, stride=SUBLANES), :] = pltpu.bitcast(pair, I32)


def _unpack_word(word, stage_ref, slot):
    rows = word.shape[0]
    base = slot * 2 * rows
    stage_ref[pl.ds(base, 2 * rows), :] = pltpu.bitcast(word, BF16).astype(F32)
    return stage_ref[pl.ds(base, rows, stride=2), :], stage_ref[pl.ds(base + 1, rows, stride=2), :]


def _mod_kernel(c_ref, w_ref, b_ref, o_ref):
    s = _silu(c_ref[...]).astype(BF16)
    o_ref[...] = jnp.dot(s, w_ref[...].astype(BF16), preferred_element_type=F32) + b_ref[...]


def _modulation(cond, w, b):
    rows, d = cond.shape
    n = w.shape[1]
    tn = 1024
    return pl.pallas_call(
        _mod_kernel,
        grid=(n // tn,),
        in_specs=[pl.BlockSpec((rows, d), lambda j: (0, 0)),
                  pl.BlockSpec((d, tn), lambda j: (0, j)),
                  pl.BlockSpec((1, tn), lambda j: (0, j))],
        out_specs=pl.BlockSpec((rows, tn), lambda j: (0, j)),
        out_shape=jax.ShapeDtypeStruct((rows, n), F32),
        compiler_params=_params(1, 40),
        name="modulation",
    )(cond, w, b)


def _prenorm_kernel(x_ref, shift_ref, scale_ref, g_ref, o_ref, *, tiles_per_row):
    r = pl.program_id(0) // tiles_per_row
    y = _rms(x_ref[...], g_ref[...])
    o_ref[...] = (y * (1.0 + scale_ref[pl.ds(r, 1), :]) + shift_ref[pl.ds(r, 1), :]).astype(o_ref.dtype)


def _prenorm(x, mod, g, shift_col, scale_col):
    t, d = x.shape
    nb = mod.shape[0]
    tm = 512
    return pl.pallas_call(
        functools.partial(_prenorm_kernel, tiles_per_row=t // nb // tm),
        grid=(t // tm,),
        in_specs=[pl.BlockSpec((tm, d), lambda i: (i, 0)),
                  pl.BlockSpec((nb, d), lambda i: (0, shift_col)),
                  pl.BlockSpec((nb, d), lambda i: (0, scale_col)),
                  pl.BlockSpec((1, d), lambda i: (0, 0))],
        out_specs=pl.BlockSpec((tm, d), lambda i: (i, 0)),
        out_shape=jax.ShapeDtypeStruct((t, d), BF16),
        compiler_params=_params(1, 32),
        name="prenorm",
    )(x, mod, mod, g)


def _mm_wcast_kernel(a_ref, w_ref, o_ref, wb_ref):
    @pl.when(pl.program_id(1) == 0)
    def _():
        wb_ref[...] = w_ref[...].astype(BF16)

    o_ref[...] = jnp.dot(a_ref[...], wb_ref[...], preferred_element_type=F32).astype(o_ref.dtype)


def _matmul_wcast(a, w, col_map, n_out, out_dtype):
    m, k = a.shape
    tn = 1024
    tm = 2048 if jnp.dtype(out_dtype).itemsize == 2 else 1024
    return pl.pallas_call(
        _mm_wcast_kernel,
        grid=(n_out // tn, m // tm),
        in_specs=[pl.BlockSpec((tm, k), lambda j, i: (i, 0)),
                  pl.BlockSpec((k, tn), lambda j, i: (0, col_map(j)))],
        out_specs=pl.BlockSpec((tm, tn), lambda j, i: (i, j)),
        out_shape=jax.ShapeDtypeStruct((m, n_out), out_dtype),
        scratch_shapes=[pltpu.VMEM((k, tn), BF16)],
        compiler_params=_params(2, 56),
        name="in_proj",
    )(a, w)


def _z_col_map(j):
    return jnp.where(j < 2, j + 3, jnp.where(j < 6, j + 4, j - 6))


def _qk_kernel(*refs, use_rope, n_heads, n_kv, q_scale):
    if use_rope:
        q_ref, kv_ref, gq_ref, gk_ref, cos_ref, sin_ref, qn_ref, kn_ref, vb_ref, k32_ref, v32_ref = refs
        cos = cos_ref[...]
        sin = sin_ref[...]
        even = lax.broadcasted_iota(I32, cos.shape, 1) % 2 == 0
    else:
        q_ref, kv_ref, gq_ref, gk_ref, qn_ref, kn_ref, vb_ref, k32_ref, v32_ref = refs

    def norm_rope(xh, g):
        y = _rms(xh, g)
        if use_rope:
            sw = jnp.where(even, pltpu.roll(y, LANES - 1, 1), pltpu.roll(y, 1, 1))
            y = y * cos + sw * sin
        return y

    gq = gq_ref[...]
    gk = gk_ref[...]
    for h in range(n_heads):
        sl = slice(h * LANES, (h + 1) * LANES)
        qn_ref[:, sl] = (norm_rope(q_ref[:, sl].astype(F32), gq) * q_scale).astype(BF16)
    kw = n_kv * LANES
    for h in range(n_kv):
        sl = slice(h * LANES, (h + 1) * LANES)
        kh = norm_rope(kv_ref[:, sl], gk)
        k32_ref[:, sl] = kh
        kn_ref[:, sl] = kh.astype(BF16)
    v = kv_ref[:, kw:2 * kw]
    v32_ref[...] = v
    vb_ref[...] = v.astype(BF16)


def _qk_prep(z, kv, gq, gk, rope, seq_len, n_heads, n_kv):
    t = z.shape[0]
    dq = n_heads * LANES
    dk = n_kv * LANES
    tm = 256
    in_specs = [pl.BlockSpec((tm, dq), lambda i: (i, 0)),
                pl.BlockSpec((tm, 2 * dk), lambda i: (i, 0)),
                pl.BlockSpec((1, LANES), lambda i: (0, 0)),
                pl.BlockSpec((1, LANES), lambda i: (0, 0))]
    args = [z, kv, gq, gk]
    if rope is not None:
        per_seq = seq_len // tm
        in_specs += [pl.BlockSpec((tm, LANES), lambda i: (i % per_seq, 0))] * 2
        args += list(rope)
    out_specs = [pl.BlockSpec((tm, dq), lambda i: (i, 0))] + [pl.BlockSpec((tm, dk), lambda i: (i, 0))] * 4
    out_shape = [jax.ShapeDtypeStruct((t, dq), BF16), jax.ShapeDtypeStruct((t, dk), BF16),
                 jax.ShapeDtypeStruct((t, dk), BF16), jax.ShapeDtypeStruct((t, dk), F32),
                 jax.ShapeDtypeStruct((t, dk), F32)]
    return pl.pallas_call(
        functools.partial(_qk_kernel, use_rope=rope is not None, n_heads=n_heads, n_kv=n_kv,
                          q_scale=LANES ** -0.5),
        grid=(t // tm,),
        in_specs=in_specs, out_specs=out_specs, out_shape=out_shape,
        compiler_params=_params(1, 32),
        name="qk_prep",
    )(*args)


def _attn_kernel(*refs, has_cache, group, chunks):
    if has_cache:
        q_ref, k_ref, v_ref, ck_ref, cv_ref, o_ref = refs
    else:
        q_ref, k_ref, v_ref, o_ref = refs
    tq = q_ref.shape[0]
    nt = (((1,), (1,)), ((), ()))
    if has_cache:
        ck = ck_ref[...].astype(BF16)
        cv = cv_ref[...].astype(BF16)
    per = group // chunks
    for c in range(chunks):
        heads = range(c * per, (c + 1) * per)
        q = jnp.concatenate([q_ref[:, g * LANES:(g + 1) * LANES] for g in heads], axis=0)
        s_own = lax.dot_general(q, k_ref[...], nt, preferred_element_type=F32)
        m = jnp.max(s_own, axis=-1, keepdims=True)
        if has_cache:
            s_ctx = lax.dot_general(q, ck, nt, preferred_element_type=F32)
            m = jnp.maximum(m, jnp.max(s_ctx, axis=-1, keepdims=True))
        p = jnp.exp(s_own - m)
        denom = jnp.sum(p, axis=-1, keepdims=True)
        acc = jnp.dot(p.astype(BF16), v_ref[...], preferred_element_type=F32)
        if has_cache:
            pc = jnp.exp(s_ctx - m)
            denom = denom + jnp.sum(pc, axis=-1, keepdims=True)
            acc = acc + jnp.dot(pc.astype(BF16), cv, preferred_element_type=F32)
        o = acc / denom
        for n, g in enumerate(heads):
            o_ref[:, g * LANES:(g + 1) * LANES] = o[n * tq:(n + 1) * tq].astype(o_ref.dtype)


def _attention(qn, kn, vb, cache, batch, seq_len, n_heads, n_kv, tq):
    t = qn.shape[0]
    group = n_heads // n_kv
    nq = seq_len // tq
    in_specs = [pl.BlockSpec((tq, group * LANES), lambda b, h, i: (b * nq + i, h)),
                pl.BlockSpec((seq_len, LANES), lambda b, h, i: (b, h)),
                pl.BlockSpec((seq_len, LANES), lambda b, h, i: (b, h))]
    args = [qn, kn, vb]
    if cache is not None:
        past = cache[0].shape[1]
        in_specs += [pl.BlockSpec((None, past, LANES), lambda b, h, i: (b, 0, h))] * 2
        args += list(cache)
    return pl.pallas_call(
        functools.partial(_attn_kernel, has_cache=cache is not None, group=group,
                          chunks=group if cache is not None else group // 2),
        grid=(batch, n_kv, nq),
        in_specs=in_specs,
        out_specs=pl.BlockSpec((tq, group * LANES), lambda b, h, i: (b * nq + i, h)),
        out_shape=jax.ShapeDtypeStruct((t, n_heads * LANES), BF16),
        compiler_params=_params(3, 40),
        name="attention",
    )(*args)


CONV_HALO = 16


def _conv_kernel(u_ref, b_ref, c_ref, up_ref, cp_ref, un_ref, cn_ref, w_ref, bias_ref, o_ref, *,
                 tiles_per_seq):
    tm = u_ref.shape[0]
    pos = pl.program_id(0) % tiles_per_seq
    cu = c_ref[...].astype(F32) * u_ref[...].astype(F32)
    halo_prev = (cp_ref[...].astype(F32) * up_ref[...].astype(F32))[CONV_HALO - 1:CONV_HALO, :]
    halo_next = (cn_ref[...].astype(F32) * un_ref[...].astype(F32))[0:1, :]
    halo_prev = jnp.where(pos == 0, 0.0, halo_prev)
    halo_next = jnp.where(pos == tiles_per_seq - 1, 0.0, halo_next)
    row = lax.broadcasted_iota(I32, cu.shape, 0)
    prev = jnp.where(row == 0, halo_prev, pltpu.roll(cu, 1, 0))
    nxt = jnp.where(row == tm - 1, halo_next, pltpu.roll(cu, tm - 1, 0))
    w = w_ref[...]
    conv = prev * w[0:1, :] + cu * w[1:2, :] + nxt * w[2:3, :] + bias_ref[...]
    o_ref[...] = (b_ref[...].astype(F32) * conv).astype(o_ref.dtype)


def _gated_conv(z, conv_w, conv_b, seq_len, col0):
    t = z.shape[0]
    dc = conv_w.shape[1]
    tm = 256
    hb = tm // CONV_HALO
    last = t // CONV_HALO - 1
    prev_map = lambda c: (lambda i: (jnp.maximum(i * hb - 1, 0), c))
    next_map = lambda c: (lambda i: (jnp.minimum((i + 1) * hb, last), c))
    in_specs = [pl.BlockSpec((tm, dc), lambda i: (i, col0)),
                pl.BlockSpec((tm, dc), lambda i: (i, col0 + 1)),
                pl.BlockSpec((tm, dc), lambda i: (i, col0 + 2)),
                pl.BlockSpec((CONV_HALO, dc), prev_map(col0)),
                pl.BlockSpec((CONV_HALO, dc), prev_map(col0 + 2)),
                pl.BlockSpec((CONV_HALO, dc), next_map(col0)),
                pl.BlockSpec((CONV_HALO, dc), next_map(col0 + 2)),
                pl.BlockSpec(conv_w.shape, lambda i: (0, 0)),
                pl.BlockSpec((1, dc), lambda i: (0, 0))]
    return pl.pallas_call(
        functools.partial(_conv_kernel, tiles_per_seq=seq_len // tm),
        grid=(t // tm,),
        in_specs=in_specs,
        out_specs=pl.BlockSpec((tm, dc), lambda i: (i, 0)),
        out_shape=jax.ShapeDtypeStruct((t, dc), BF16),
        compiler_params=_params(1, 32),
        name="gated_conv",
    )(z, z, z, z, z, z, z, conv_w, conv_b)


def _merge_kernel(cv_ref, at_ref, gc_ref, ga_ref, wc_ref, wa_ref, o_ref, wcb_ref, wab_ref):
    @pl.when(pl.program_id(1) == 0)
    def _():
        wcb_ref[...] = wc_ref[...].astype(BF16)
        wab_ref[...] = wa_ref[...].astype(BF16)

    conv_out = jnp.dot(cv_ref[...], wcb_ref[...], preferred_element_type=F32)
    attn_out = jnp.dot(at_ref[...], wab_ref[...], preferred_element_type=F32)
    merged = (jax.nn.sigmoid(gc_ref[...].astype(F32)) * conv_out
              + jax.nn.sigmoid(ga_ref[...].astype(F32)) * attn_out)
    o_ref[...] = merged.astype(o_ref.dtype)


def _merge(cvg, attn, z, w_conv_out, w_attn_out, gc_col0, ga_col0):
    t, dc = cvg.shape
    dq = attn.shape[1]
    d = w_conv_out.shape[1]
    tm, tn = 512, 512
    return pl.pallas_call(
        _merge_kernel,
        grid=(d // tn, t // tm),
        in_specs=[pl.BlockSpec((tm, dc), lambda j, i: (i, 0)),
                  pl.BlockSpec((tm, dq), lambda j, i: (i, 0)),
                  pl.BlockSpec((tm, tn), lambda j, i: (i, gc_col0 + j)),
                  pl.BlockSpec((tm, tn), lambda j, i: (i, ga_col0 + j)),
                  pl.BlockSpec((dc, tn), lambda j, i: (0, j)),
                  pl.BlockSpec((dq, tn), lambda j, i: (0, j))],
        out_specs=pl.BlockSpec((tm, tn), lambda j, i: (i, j)),
        out_shape=jax.ShapeDtypeStruct((t, d), BF16),
        scratch_shapes=[pltpu.VMEM((dc, tn), BF16), pltpu.VMEM((dq, tn), BF16)],
        compiler_params=_params(2, 40),
        name="merge",
    )(cvg, attn, z, z, w_conv_out, w_attn_out)


def _route(logits_t, bias_col, carry):
    n_exp, tm = logits_t.shape
    per = n_exp // N_GROUPS
    assert per == SUBLANES
    neg = -jnp.inf
    scores = jax.nn.sigmoid(logits_t)
    biased = scores + bias_col
    sub = lax.broadcasted_iota(I32, (per, tm), 0).astype(F32)
    xs = [biased[g * per:(g + 1) * per, :] for g in range(N_GROUPS)]
    sc = [scores[g * per:(g + 1) * per, :] for g in range(N_GROUPS)]
    ids = [sub + float(g * per) for g in range(N_GROUPS)]

    def colmax(a):
        return jnp.max(a, axis=0, keepdims=True)

    def colmin(a):
        return jnp.min(a, axis=0, keepdims=True)

    rows = []
    for g in range(N_GROUPS):
        m1 = colmax(xs[g])
        j1 = colmin(jnp.where(xs[g] == m1, sub, float(per)))
        m2 = colmax(jnp.where(sub == j1, neg, xs[g]))
        rows.append(m1 + m2)
    gs = jnp.concatenate(rows, axis=0)
    gsel = jnp.zeros_like(gs)
    for _ in range(TOPK_GROUPS):
        m = colmax(gs)
        j = colmin(jnp.where(gs == m, sub, float(N_GROUPS)))
        hit = sub == j
        gsel = jnp.where(hit, 1.0, gsel)
        gs = jnp.where(hit, neg, gs)
    masked = [jnp.where(gsel[g:g + 1, :] > 0.0, xs[g], neg) for g in range(N_GROUPS)]
    idx_rows, w_rows = [], []
    member = [jnp.zeros((per, tm), F32) for _ in range(N_GROUPS)]
    for _ in range(TOP_K):
        mm = masked[0]
        for g in range(1, N_GROUPS):
            mm = jnp.maximum(mm, masked[g])
        m = colmax(mm)
        idx = colmin(jnp.where(masked[0] == m, ids[0], float(n_exp)))
        for g in range(1, N_GROUPS):
            idx = jnp.minimum(idx, colmin(jnp.where(masked[g] == m, ids[g], float(n_exp))))
        wk = jnp.zeros_like(idx)
        for g in range(N_GROUPS):
            hit = ids[g] == idx
            wk = wk + jnp.sum(jnp.where(hit, sc[g], 0.0), axis=0, keepdims=True)
            masked[g] = jnp.where(hit, neg, masked[g])
            member[g] = jnp.where(hit, 1.0, member[g])
        idx_rows.append(idx)
        w_rows.append(wk)
    w = jnp.concatenate(w_rows, axis=0)
    w = w / jnp.sum(w, axis=0, keepdims=True) * ROUTED_SCALE

    earlier = (lax.broadcasted_iota(I32, (tm, tm), 0) < lax.broadcasted_iota(I32, (tm, tm), 1)).astype(BF16)
    before = jnp.dot(jnp.concatenate(member, axis=0).astype(BF16), earlier, preferred_element_type=F32)
    rank_rows = []
    for k in range(TOP_K):
        rk = jnp.zeros_like(idx_rows[k])
        for g in range(N_GROUPS):
            pos = before[g * per:(g + 1) * per, :] + carry[g]
            rk = rk + jnp.sum(jnp.where(ids[g] == idx_rows[k], pos, 0.0), axis=0, keepdims=True)
        rank_rows.append(rk)
    new_carry = [carry[g] + jnp.sum(member[g], axis=1, keepdims=True) for g in range(N_GROUPS)]
    idx = jnp.concatenate(idx_rows, axis=0).astype(I32)
    rank = jnp.concatenate(rank_rows, axis=0).astype(I32)
    return idx, w, rank, new_carry


def _out_kernel(mg_ref, x_ref, gate_ref, shift_ref, scale_ref, gpost_ref, gpre_ref, wo_ref, wr_ref,
                x1_ref, h2b_ref, h2p_ref, logit_ref, stage_ref, *, tiles_per_row):
    tm = x_ref.shape[0]
    r = pl.program_id(0) // tiles_per_row
    gate = gate_ref[pl.ds(r, 1), :]
    scale = 1.0 + scale_ref[pl.ds(r, 1), :]
    shift = shift_ref[pl.ds(r, 1), :]
    rows = tm // OUT_CHUNKS
    for c in range(OUT_CHUNKS):
        sl = pl.ds(c * rows, rows)
        mix = jnp.dot(mg_ref[sl, :], wo_ref[...], preferred_element_type=F32)
        x1 = x_ref[sl, :] + gate * _rms(mix, gpost_ref[...])
        x1_ref[sl, :] = x1
        h2 = _rms(x1, gpre_ref[...]) * scale + shift
        h2b_ref[sl, :] = h2.astype(BF16)
        _pack_rows(h2, h2p_ref, stage_ref, rows, row0=c * rows * SUBLANES, stage0=c * rows * 2 * SUBLANES)
        logit_ref[:, sl] = lax.dot_general(wr_ref[...], h2, (((1,), (1,)), ((), ())),
                                           preferred_element_type=F32, precision=lax.Precision.HIGHEST)


OUT_CHUNKS = 4


def _out_proj(merged, x, mod, g_post, g_pre, w_out_b, w_router_t):
    t, d = x.shape
    nb = mod.shape[0]
    n_exp = w_router_t.shape[0]
    tm = 512
    row = lambda i: (i, 0)
    fixed = lambda i: (0, 0)
    once = pl.Buffered(1)
    in_specs = [pl.BlockSpec((tm, d), row), pl.BlockSpec((tm, d), row),
                pl.BlockSpec((nb, d), lambda i: (0, 2)),
                pl.BlockSpec((nb, d), lambda i: (0, 3)),
                pl.BlockSpec((nb, d), lambda i: (0, 4)),
                pl.BlockSpec((1, d), fixed), pl.BlockSpec((1, d), fixed),
                pl.BlockSpec((d, d), fixed, pipeline_mode=once),
                pl.BlockSpec((n_exp, d), fixed, pipeline_mode=once)]
    out_specs = [pl.BlockSpec((tm, d), row), pl.BlockSpec((tm, d), row),
                 pl.BlockSpec((tm * SUBLANES, LANES), row),
                 pl.BlockSpec((n_exp, tm), lambda i: (0, i))]
    out_shape = [jax.ShapeDtypeStruct((t, d), F32), jax.ShapeDtypeStruct((t, d), BF16),
                 jax.ShapeDtypeStruct((t * SUBLANES, LANES), I32),
                 jax.ShapeDtypeStruct((n_exp, t), F32)]
    return pl.pallas_call(
        functools.partial(_out_kernel, tiles_per_row=t // nb // tm),
        grid=(t // tm,),
        in_specs=in_specs, out_specs=out_specs, out_shape=out_shape,
        scratch_shapes=[pltpu.VMEM((SUBLANES * 2 * tm, LANES), F32)],
        compiler_params=_params(1, 56),
        name="out_proj",
    )(merged, x, mod, mod, mod, g_post, g_pre, w_out_b, w_router_t)


def _router_kernel(logit_ref, br_ref, idx_ref, wsel_ref, rank_ref, cnt_ref):
    per = SUBLANES

    @pl.when(pl.program_id(0) == 0)
    def _():
        cnt_ref[...] = jnp.zeros_like(cnt_ref)

    carry = [cnt_ref[g * per:(g + 1) * per, 0:1] for g in range(N_GROUPS)]
    idx, w, rank, carry = _route(logit_ref[...], br_ref[...], carry)
    idx_ref[...] = idx
    wsel_ref[...] = w
    rank_ref[...] = rank
    for g in range(N_GROUPS):
        cnt_ref[g * per:(g + 1) * per, :] = jnp.broadcast_to(carry[g], (per, LANES))


def _router(logits_t, b_router_col):
    n_exp, t = logits_t.shape
    tr = 1024
    tile = lambda i: (0, i)
    fixed = lambda i: (0, 0)
    return pl.pallas_call(
        _router_kernel,
        grid=(t // tr,),
        in_specs=[pl.BlockSpec((n_exp, tr), tile), pl.BlockSpec((n_exp, 1), fixed)],
        out_specs=[pl.BlockSpec((TOP_K, tr), tile), pl.BlockSpec((TOP_K, tr), tile),
                   pl.BlockSpec((TOP_K, tr), tile), pl.BlockSpec((n_exp, LANES), fixed)],
        out_shape=[jax.ShapeDtypeStruct((TOP_K, t), I32), jax.ShapeDtypeStruct((TOP_K, t), F32),
                   jax.ShapeDtypeStruct((TOP_K, t), I32), jax.ShapeDtypeStruct((n_exp, LANES), F32)],
        compiler_params=_params(1, 32),
        name="router",
    )(logits_t, b_router_col)


def _dest_kernel(pstart_ref, idx_ref, rank_ref, o_ref):
    idx = idx_ref[...]
    acc = rank_ref[...]
    for e in range(pstart_ref.shape[0]):
        acc = acc + jnp.where(idx == e, pstart_ref[e], 0)
    o_ref[...] = acc


def _dest_slots(pstart, idx, rank):
    return pl.pallas_call(
        _dest_kernel,
        in_specs=[pl.BlockSpec(memory_space=pltpu.SMEM),
                  pl.BlockSpec(memory_space=pltpu.VMEM), pl.BlockSpec(memory_space=pltpu.VMEM)],
        out_specs=pl.BlockSpec(memory_space=pltpu.VMEM),
        out_shape=jax.ShapeDtypeStruct(idx.shape, I32),
        name="dest_slots",
    )(pstart, idx, rank)


def _slot_table_kernel(pad_lo_ref, pad_hi_ref, dest_ref, tab_ref):
    i = pl.program_id(0)
    tm = dest_ref.shape[2]

    @pl.when(i == 0)
    def _():
        def clear_range(e, carry):
            def clear(s, c):
                tab_ref[s] = 0
                return c
            return lax.fori_loop(pad_lo_ref[e], pad_hi_ref[e], clear, carry)
        lax.fori_loop(0, pad_lo_ref.shape[0], clear_range, 0)

    def fill(t, carry):
        for k in range(TOP_K):
            tab_ref[dest_ref[0, k, t]] = (i * tm + t) * TOP_K + k
        return carry

    lax.fori_loop(0, tm, fill, 0)


def _slot_table(pad_lo, pad_hi, dest3, n_slots):
    nt, _, tm = dest3.shape
    grid_spec = pltpu.PrefetchScalarGridSpec(
        num_scalar_prefetch=2,
        grid=(nt,),
        in_specs=[pl.BlockSpec((1, TOP_K, tm), lambda i, lo, hi: (i, 0, 0), memory_space=pltpu.SMEM)],
        out_specs=pl.BlockSpec(memory_space=pltpu.SMEM),
    )
    return pl.pallas_call(
        _slot_table_kernel,
        grid_spec=grid_spec,
        out_shape=jax.ShapeDtypeStruct((n_slots,), I32),
        compiler_params=_params(1, 16),
        name="slot_table",
    )(pad_lo, pad_hi, dest3)


GATHER_UNROLL = 16
GATHER_BLOCK = 1024
assert TOP_K == SUBLANES


def _gather_kernel(tab_ref, src_ref, o_ref):
    n = tab_ref.shape[2]

    def move(c, carry):
        for u in range(GATHER_UNROLL):
            r = c * GATHER_UNROLL + u
            src = pl.multiple_of(tab_ref[0, 0, r] & -SUBLANES, SUBLANES)
            dst = pl.multiple_of(r * SUBLANES, SUBLANES)
            o_ref[pl.ds(dst, SUBLANES), :] = src_ref[pl.ds(src, SUBLANES), :]
        return carry

    lax.fori_loop(0, n // GATHER_UNROLL, move, 0)


def _gather_rows(slot_tab, h2p, gb):
    n_slots = slot_tab.shape[0]
    nblk = n_slots // gb
    resident = h2p.size * h2p.dtype.itemsize
    return pl.pallas_call(
        _gather_kernel,
        grid=(nblk,),
        in_specs=[pl.BlockSpec((1, 1, gb), lambda b: (b, 0, 0), memory_space=pltpu.SMEM),
                  pl.BlockSpec(memory_space=pltpu.VMEM)],
        out_specs=pl.BlockSpec((gb * SUBLANES, LANES), lambda b: (b, 0)),
        out_shape=jax.ShapeDtypeStruct((n_slots * SUBLANES, LANES), I32),
        compiler_params=_params(1, resident // MIB + 8),
        name="dispatch_gather",
    )(slot_tab.reshape(nblk, 1, gb), h2p)


ROW_DMA_PRIORITY = 1
EXPERT_CHUNKS = 1
X_SLOTS = 6
Y_SLOTS = 3


def _expert_kernel(first_ref, nblk_ref, xs_hbm, wg_ref, wu_ref, wd_ref, yb_hbm,
                   xbuf_ref, obuf_ref, in_sem, out_sem, wgb_ref, wub_ref, wdb_ref, stage_in_ref, stage_out_ref,
                   *, bm):
    e = pl.program_id(0)
    last_e = pl.num_programs(0) - 1
    nb = nblk_ref[e]
    b0 = first_ref[e]
    total = first_ref[last_e] + nblk_ref[last_e]
    rows = bm * SUBLANES

    def x_copy(blk):
        slot = lax.rem(blk, X_SLOTS)
        return pltpu.make_async_copy(
            xs_hbm.at[pl.ds(pl.multiple_of(blk * rows, rows), rows), :],
            xbuf_ref.at[pl.ds(pl.multiple_of(slot * rows, rows), rows), :], in_sem.at[slot])

    def y_copy(blk):
        slot = lax.rem(blk, Y_SLOTS)
        return pltpu.make_async_copy(
            obuf_ref.at[pl.ds(pl.multiple_of(slot * rows, rows), rows), :],
            yb_hbm.at[pl.ds(pl.multiple_of(blk * rows, rows), rows), :], out_sem.at[slot])

    @pl.when(e == 0)
    def _():
        for j in range(X_SLOTS - 1):
            @pl.when(total > j)
            def _(j=j):
                x_copy(j).start(priority=ROW_DMA_PRIORITY)

    @pl.when(nb > 0)
    def _():
        wgb_ref[...] = wg_ref[...].astype(BF16)
        wub_ref[...] = wu_ref[...].astype(BF16)
        wdb_ref[...] = wd_ref[...].astype(BF16)

    def block(j, carry):
        blk = b0 + j
        xslot = lax.rem(blk, X_SLOTS)
        yblk = blk
        yslot = lax.rem(blk, Y_SLOTS)
        x_copy(blk).wait()

        ahead = blk + (X_SLOTS - 1)

        @pl.when(ahead < total)
        def _():
            x_copy(ahead).start(priority=ROW_DMA_PRIORITY)

        @pl.when(yblk >= Y_SLOTS)
        def _():
            y_copy(yblk - Y_SLOTS).wait()

        xbase = xslot * rows
        ybase = yslot * rows
        sub = bm // EXPERT_CHUNKS
        for c in range(EXPERT_CHUNKS):
            cbase = xbase + c * sub * SUBLANES
            obase = ybase + c * sub * SUBLANES
            los, his = [], []
            for s in range(SUBLANES):
                lo, hi = _unpack_word(xbuf_ref[pl.ds(cbase + s, sub, stride=SUBLANES), :], stage_in_ref,
                                      c * SUBLANES + s)
                los.append(lo.astype(BF16))
                his.append(hi.astype(BF16))
            x = jnp.concatenate(los + his, axis=1)
            g = jnp.dot(x, wgb_ref[...], preferred_element_type=F32)
            u = jnp.dot(x, wub_ref[...], preferred_element_type=F32)
            a = (_silu(g) * u).astype(BF16)
            y = jnp.dot(a, wdb_ref[...], preferred_element_type=F32)
            _pack_rows(y, obuf_ref, stage_out_ref, sub, row0=obase, stage0=c * sub * 2 * SUBLANES)
        y_copy(yblk).start()
        return carry

    lax.fori_loop(0, nb, block, 0)

    @pl.when(e == last_e)
    def _():
        for back in range(Y_SLOTS, 0, -1):
            @pl.when(total >= back)
            def _(back=back):
                y_copy(total - back).wait()

        obuf_ref[pl.ds(0, rows), :] = pltpu.bitcast(jnp.zeros((2 * rows, LANES), BF16), I32)

        def clear(j, carry):
            cp = pltpu.make_async_copy(obuf_ref.at[pl.ds(0, rows), :],
                                       yb_hbm.at[pl.ds(pl.multiple_of(j * rows, rows), rows), :], out_sem.at[0])
            cp.start()
            cp.wait()
            return carry

        lax.fori_loop(b0 + nb, yb_hbm.shape[0] // rows, clear, 0)


def _experts(first_block, n_blocks_e, xs, w_gate, w_up, w_down, bm):
    n_exp, d, de = w_gate.shape
    rows = bm * SUBLANES
    grid_spec = pltpu.PrefetchScalarGridSpec(
        num_scalar_prefetch=2,
        grid=(n_exp,),
        in_specs=[pl.BlockSpec(memory_space=pl.ANY),
                  pl.BlockSpec((None, d, de), lambda e, fb, nb: (e, 0, 0)),
                  pl.BlockSpec((None, d, de), lambda e, fb, nb: (e, 0, 0)),
                  pl.BlockSpec((None, de, d), lambda e, fb, nb: (e, 0, 0))],
        out_specs=pl.BlockSpec(memory_space=pl.ANY),
        scratch_shapes=[pltpu.VMEM((X_SLOTS * rows, LANES), I32), pltpu.VMEM((Y_SLOTS * rows, LANES), I32),
                        pltpu.SemaphoreType.DMA((X_SLOTS,)), pltpu.SemaphoreType.DMA((Y_SLOTS,)),
                        pltpu.VMEM((d, de), BF16), pltpu.VMEM((d, de), BF16), pltpu.VMEM((de, d), BF16),
                        pltpu.VMEM((SUBLANES * 2 * bm, LANES), F32), pltpu.VMEM((SUBLANES * 2 * bm, LANES), F32)],
    )
    return pl.pallas_call(
        functools.partial(_expert_kernel, bm=bm),
        grid_spec=grid_spec,
        out_shape=jax.ShapeDtypeStruct(xs.shape, I32),
        compiler_params=_params(1, 56),
        name="routed_experts",
    )(first_block, n_blocks_e, xs, w_gate, w_up, w_down)


def _final_kernel(dest_ref, w_ref, yb_hbm, x1_ref, h2_ref, gate_ref, gpost_ref, wsg_ref, wsu_ref,
                  wsd_ref, o_ref, buf_ref, sem, stage_ref, *, steps_per_row):
    tm = x1_ref.shape[0] // 2
    i = pl.program_id(0)
    last = pl.num_programs(0) - 1
    r = i // steps_per_row
    tile_rows = TOP_K * tm * SUBLANES
    gate = gate_ref[pl.ds(r, 1), :]

    def request(tile, slot, t, k):
        src = pl.multiple_of(dest_ref[tile, k, t], SUBLANES)
        dst = slot * tile_rows + (k * tm + t) * SUBLANES
        pltpu.make_async_copy(yb_hbm.at[pl.ds(src, SUBLANES), :], buf_ref.at[pl.ds(dst, SUBLANES), :],
                              sem.at[slot]).start(priority=k % 2)

    def wait_tile(slot):
        pltpu.make_async_copy(yb_hbm.at[pl.ds(0, tile_rows), :],
                              buf_ref.at[pl.ds(slot * tile_rows, tile_rows), :], sem.at[slot]).wait()

    @pl.when(i == 0)
    def _():
        def first(t, carry):
            for k in range(TOP_K):
                request(0, 0, t, k)
            return carry
        lax.fori_loop(0, tm, first, 0)

    pending = [(t, k) for t in range(tm) for k in range(TOP_K)]
    per_piece = 2 * len(pending) // (SUBLANES * TOP_K)

    def combine(half, slot, next_tile):
        rows = pl.ds(half * tm, tm)
        h = h2_ref[rows, :]
        g = jnp.dot(h, wsg_ref[...], preferred_element_type=F32)
        u = jnp.dot(h, wsu_ref[...], preferred_element_type=F32)
        shared = jnp.dot((_silu(g) * u).astype(BF16), wsd_ref[...], preferred_element_type=F32)
        w = w_ref[rows, :]
        base = slot * tile_rows
        los, his = [], []
        for s in range(SUBLANES):
            lo_acc = jnp.zeros((tm, LANES), F32)
            hi_acc = jnp.zeros((tm, LANES), F32)
            for k in range(TOP_K):
                lo, hi = _unpack_word(buf_ref[pl.ds(base + k * tm * SUBLANES + s, tm, stride=SUBLANES), :],
                                      stage_ref, s * TOP_K + k)
                wk = w[:, k:k + 1]
                lo_acc = lo_acc + wk * lo
                hi_acc = hi_acc + wk * hi
                n = s * TOP_K + k
                for t, kk in pending[n * per_piece:(n + 1) * per_piece]:
                    request(next_tile, 1 - slot, t, kk)
            los.append(lo_acc)
            his.append(hi_acc)
        ffn = jnp.concatenate(los + his, axis=1) + shared
        o_ref[rows, :] = x1_ref[rows, :] + gate * _rms(ffn, gpost_ref[...])

    wait_tile(0)
    combine(0, 0, 2 * i + 1)
    wait_tile(1)
    combine(1, 1, jnp.minimum(2 * i + 2, 2 * last + 1))

    @pl.when(i == last)
    def _():
        wait_tile(0)


def _final(dest8, w_tok, yb, x1, h2b, mod, g_post, ws_gate, ws_up, ws_down):
    t, d = x1.shape
    nb = mod.shape[0]
    ds_ = ws_gate.shape[1]
    tm = dest8.shape[2]
    n_steps = t // (2 * tm)
    row = lambda i, dst: (i, 0)
    fixed = lambda i, dst: (0, 0)
    grid_spec = pltpu.PrefetchScalarGridSpec(
        num_scalar_prefetch=1,
        grid=(n_steps,),
        in_specs=[pl.BlockSpec((2 * tm, TOP_K), row),
                  pl.BlockSpec(memory_space=pl.ANY),
                  pl.BlockSpec((2 * tm, d), row), pl.BlockSpec((2 * tm, d), row),
                  pl.BlockSpec((nb, d), lambda i, dst: (0, 5)),
                  pl.BlockSpec((1, d), fixed),
                  pl.BlockSpec((d, ds_), fixed), pl.BlockSpec((d, ds_), fixed), pl.BlockSpec((ds_, d), fixed)],
        out_specs=pl.BlockSpec((2 * tm, d), row),
        scratch_shapes=[pltpu.VMEM((2 * TOP_K * tm * SUBLANES, LANES), I32), pltpu.SemaphoreType.DMA((2,)),
                        pltpu.VMEM((SUBLANES * TOP_K * 2 * tm, LANES), F32)],
    )
    return pl.pallas_call(
        functools.partial(_final_kernel, steps_per_row=t // nb // (2 * tm)),
        grid_spec=grid_spec,
        out_shape=jax.ShapeDtypeStruct((t, d), F32),
        compiler_params=_params(1, 48),
        name="combine_final",
    )(dest8, w_tok, yb, x1, h2b, mod, g_post, ws_gate, ws_up, ws_down)


def _rope_tables(n_pos):
    half = LANES // 2
    pos = jnp.arange(n_pos)
    row = (pos // GRID_W).astype(F32)
    col = (pos % GRID_W).astype(F32)
    inv = ROPE_THETA ** (-jnp.arange(0, half, 2, dtype=F32) / half)
    ang = jnp.concatenate([row[:, None] * inv, col[:, None] * inv], axis=-1)
    cos = jnp.repeat(jnp.cos(ang), 2, axis=-1)
    sin = jnp.repeat(jnp.sin(ang), 2, axis=-1) * jnp.tile(jnp.array([-1.0, 1.0], F32), half)
    return cos, sin


def _sublayer1(x, mod, p, seq_len, rope, cache, n_heads, n_kv, tq):
    t = x.shape[0]
    h = _prenorm(x, mod, p['g_pre1'], 0, 1)
    z = _matmul_wcast(h, p['w_in'], _z_col_map, 9 * 1024, BF16)
    kv = _matmul_wcast(h, p['w_in'], lambda j: j + 5, 1024, F32)
    qn, kn, vb, k32, v32 = _qk_prep(z, kv, p['g_q'], p['g_k'], rope, seq_len, n_heads, n_kv)
    attn = _attention(qn, kn, vb, cache, t // seq_len, seq_len, n_heads, n_kv, tq)
    cvg = _gated_conv(z, p['conv_w'], p['conv_b'], seq_len, 6)
    merged = _merge(cvg, attn, z, p['w_conv_out'], p['w_attn_out'], 4, 8)
    x1, h2b, h2p, logits_t = _out_proj(merged, x, mod, p['g_post1'], p['g_pre2'], p['w_out_b'], p['w_router_t'])
    return x1, h2b, h2p, logits_t, k32, v32


def kernel(x_prompt, x_sample, cache_k, cache_v, c, c_ctx, w_mod, b_mod, g_pre1, w_in, conv_w, conv_b, g_q, g_k, w_conv_out, w_attn_out, w_out, g_post1, g_pre2, w_router, b_router, w_e_gate, w_e_up, w_e_down, w_s_gate, w_s_up, w_s_down, g_post2):
    batch, seq, d = x_prompt.shape
    dec_batch, dec_seq, _ = x_sample.shape
    depth = w_mod.shape[0]
    assert depth == 1
    past, n_kv, head_dim = cache_k.shape[2:]
    assert head_dim == LANES
    n_heads = w_attn_out.shape[1] // head_dim
    n_exp = w_router.shape[2]
    t_ctx, t_lat = batch * seq, dec_batch * dec_seq
    t_all = t_ctx + t_lat
    l = 0

    p = {
        'g_pre1': g_pre1[l][None], 'w_in': w_in[l], 'conv_w': conv_w[l], 'conv_b': conv_b[l][None],
        'g_q': g_q[l][None], 'g_k': g_k[l][None], 'w_conv_out': w_conv_out[l], 'w_attn_out': w_attn_out[l],
        'w_out_b': w_out[l].astype(BF16), 'g_post1': g_post1[l][None], 'g_pre2': g_pre2[l][None],
        'w_router_t': w_router[l].T,
    }
    cond = jnp.concatenate([c_ctx[None], c, jnp.zeros((SUBLANES - 1 - dec_batch, d), F32)], axis=0)
    mod = _modulation(cond, w_mod[l], b_mod[l][None])
    mod_ctx, mod_lat = mod[0:1], mod[1:1 + dec_batch]

    xc = x_prompt.reshape(t_ctx, d)
    xl = x_sample.reshape(t_lat, d)
    cache = (cache_k[:, l].reshape(dec_batch, past, n_kv * head_dim),
             cache_v[:, l].reshape(dec_batch, past, n_kv * head_dim))
    rope = _rope_tables(dec_seq)

    x1c, h2bc, h2pc, logc, k32, v32 = _sublayer1(xc, mod_ctx, p, seq, None, None, n_heads, n_kv, 256)
    x1l, h2bl, h2pl, logl, _, _ = _sublayer1(xl, mod_lat, p, dec_seq, rope, cache, n_heads, n_kv, 256)
    idx_all, w_all, rank_all, cnt = _router(jnp.concatenate([logc, logl], axis=1), b_router[l][:, None])

    bm = 256
    n_blocks = (t_all * TOP_K + n_exp * (bm - 1)) // bm + 1
    n_slots = n_blocks * bm
    blocks_e = jnp.floor((cnt[:, 0] + (bm - 1)) / bm)
    first_e = jnp.cumsum(blocks_e) - blocks_e
    pstart = (first_e * bm).astype(I32)

    tmf = 128
    w_tok = w_all.T
    dest = _dest_slots(pstart, idx_all, rank_all)
    dest3 = dest.reshape(TOP_K, t_all // tmf, tmf).transpose(1, 0, 2)
    used_end = jnp.sum(blocks_e, keepdims=True) * bm
    pad_lo = jnp.concatenate([first_e * bm + cnt[:, 0], used_end]).astype(I32)
    pad_hi = jnp.concatenate([(first_e + blocks_e) * bm, jnp.full((1,), float(n_slots), F32)]).astype(I32)
    slot_tab = _slot_table(pad_lo, pad_hi, dest3, n_slots)

    h2p = jnp.concatenate([h2pc, h2pl], axis=0)
    xs = _gather_rows(slot_tab, h2p, GATHER_BLOCK)
    yb = _experts(first_e.astype(I32), blocks_e.astype(I32), xs, w_e_gate[l], w_e_up[l], w_e_down[l], bm)

    ws = (w_s_gate[l].astype(BF16), w_s_up[l].astype(BF16), w_s_down[l].astype(BF16))

    def finish(lo, n, x1, h2b, mod_g):
        dest_g = dest3[lo // tmf:(lo + n) // tmf] * SUBLANES
        return _final(dest_g, w_tok[lo:lo + n], yb, x1, h2b, mod_g, g_post2[l][None], *ws)

    y_ctx = finish(0, t_ctx, x1c, h2bc, mod_ctx)
    y_lat = finish(t_ctx, t_lat, x1l, h2bl, mod_lat)

    new_k = k32.reshape(batch, 1, seq, n_kv, head_dim)
    new_v = v32.reshape(batch, 1, seq, n_kv, head_dim)
    return (y_ctx.reshape(batch, seq, d), y_lat.reshape(dec_batch, dec_seq, d), new_k, new_v)
```

```python
import functools

import jax
import jax.numpy as jnp
from jax import lax
from jax.experimental import pallas as pl
from jax.experimental.pallas import tpu as pltpu

GRID_W = 64
ROPE_THETA = 10000.0
N_GROUPS = 8
TOPK_GROUPS = 4
TOP_K = 8
ROUTED_SCALE = 2.5
EPS = 1e-6

LANES = 128
SUBLANES = 8
V7X_VMEM_BYTES = 64 * 1024 * 1024
MIB = 1024 * 1024

F32 = jnp.float32
BF16 = jnp.bfloat16
I32 = jnp.int32


def _params(n_grid, vmem_mib):
    assert vmem_mib * MIB < V7X_VMEM_BYTES
    return pltpu.CompilerParams(
        dimension_semantics=("arbitrary",) * n_grid, vmem_limit_bytes=vmem_mib * MIB)


def _silu(x):
    return x * jax.nn.sigmoid(x)


def _rms(x, g):
    return x * lax.rsqrt(jnp.mean(x * x, axis=-1, keepdims=True) + EPS) * g


def _pack_rows(val, out_ref, stage_ref, rows, row0=0, stage0=0):
    half = val.shape[1] // 2
    for s in range(half // LANES):
        base = stage0 + s * 2 * rows
        stage_ref[pl.ds(base, rows, stride=2), :] = val[:, s * LANES:(s + 1) * LANES]
        stage_ref[pl.ds(base + 1, rows, stride=2), :] = val[:, half + s * LANES:half + (s + 1) * LANES]
        pair = stage_ref[pl.ds(base, 2 * rows), :].astype(BF16)
        out_ref[pl.ds(row0 + s, rows, stride=SUBLANES), :] = pltpu.bitcast(pair, I32)


def _unpack_word(word, stage_ref, slot):
    rows = word.shape[0]
    base = slot * 2 * rows
    stage_ref[pl.ds(base, 2 * rows), :] = pltpu.bitcast(word, BF16).astype(F32)
    return stage_ref[pl.ds(base, rows, stride=2), :], stage_ref[pl.ds(base + 1, rows, stride=2), :]


def _mod_kernel(c_ref, w_ref, b_ref, o_ref):
    s = _silu(c_ref[...]).astype(BF16)
    o_ref[...] = jnp.dot(s, w_ref[...].astype(BF16), preferred_element_type=F32) + b_ref[...]


def _modulation(cond, w, b):
    rows, d = cond.shape
    n = w.shape[1]
    tn = 1024
    return pl.pallas_call(
        _mod_kernel,
        grid=(n // tn,),
        in_specs=[pl.BlockSpec((rows, d), lambda j: (0, 0)),
                  pl.BlockSpec((d, tn), lambda j: (0, j)),
                  pl.BlockSpec((1, tn), lambda j: (0, j))],
        out_specs=pl.BlockSpec((rows, tn), lambda j: (0, j)),
        out_shape=jax.ShapeDtypeStruct((rows, n), F32),
        compiler_params=_params(1, 40),
        name="modulation",
    )(cond, w, b)


def _prenorm_kernel(x_ref, shift_ref, scale_ref, g_ref, o_ref, *, tiles_per_row):
    r = pl.program_id(0) // tiles_per_row
    y = _rms(x_ref[...], g_ref[...])
    o_ref[...] = (y * (1.0 + scale_ref[pl.ds(r, 1), :]) + shift_ref[pl.ds(r, 1), :]).astype(o_ref.dtype)


def _prenorm(x, mod, g, shift_col, scale_col):
    t, d = x.shape
    nb = mod.shape[0]
    tm = 512
    return pl.pallas_call(
        functools.partial(_prenorm_kernel, tiles_per_row=t // nb // tm),
        grid=(t // tm,),
        in_specs=[pl.BlockSpec((tm, d), lambda i: (i, 0)),
                  pl.BlockSpec((nb, d), lambda i: (0, shift_col)),
                  pl.BlockSpec((nb, d), lambda i: (0, scale_col)),
                  pl.BlockSpec((1, d), lambda i: (0, 0))],
        out_specs=pl.BlockSpec((tm, d), lambda i: (i, 0)),
        out_shape=jax.ShapeDtypeStruct((t, d), BF16),
        compiler_params=_params(1, 32),
        name="prenorm",
    )(x, mod, mod, g)


def _mm_wcast_kernel(a_ref, w_ref, o_ref, wb_ref):
    @pl.when(pl.program_id(1) == 0)
    def _():
        wb_ref[...] = w_ref[...].astype(BF16)

    o_ref[...] = jnp.dot(a_ref[...], wb_ref[...], preferred_element_type=F32).astype(o_ref.dtype)


def _matmul_wcast(a, w, col_map, n_out, out_dtype):
    m, k = a.shape
    tn = 1024
    tm = 2048 if jnp.dtype(out_dtype).itemsize == 2 else 1024
    return pl.pallas_call(
        _mm_wcast_kernel,
        grid=(n_out // tn, m // tm),
        in_specs=[pl.BlockSpec((tm, k), lambda j, i: (i, 0)),
                  pl.BlockSpec((k, tn), lambda j, i: (0, col_map(j)))],
        out_specs=pl.BlockSpec((tm, tn), lambda j, i: (i, j)),
        out_shape=jax.ShapeDtypeStruct((m, n_out), out_dtype),
        scratch_shapes=[pltpu.VMEM((k, tn), BF16)],
        compiler_params=_params(2, 56),
        name="in_proj",
    )(a, w)


def _z_col_map(j):
    return jnp.where(j < 2, j + 3, jnp.where(j < 6, j + 4, j - 6))


def _qk_kernel(*refs, use_rope, n_heads, n_kv, q_scale):
    if use_rope:
        q_ref, kv_ref, gq_ref, gk_ref, cos_ref, sin_ref, qn_ref, kn_ref, vb_ref, k32_ref, v32_ref = refs
        cos = cos_ref[...]
        sin = sin_ref[...]
        even = lax.broadcasted_iota(I32, cos.shape, 1) % 2 == 0
    else:
        q_ref, kv_ref, gq_ref, gk_ref, qn_ref, kn_ref, vb_ref, k32_ref, v32_ref = refs

    def norm_rope(xh, g):
        y = _rms(xh, g)
        if use_rope:
            sw = jnp.where(even, pltpu.roll(y, LANES - 1, 1), pltpu.roll(y, 1, 1))
            y = y * cos + sw * sin
        return y

    gq = gq_ref[...]
    gk = gk_ref[...]
    for h in range(n_heads):
        sl = slice(h * LANES, (h + 1) * LANES)
        qn_ref[:, sl] = (norm_rope(q_ref[:, sl].astype(F32), gq) * q_scale).astype(BF16)
    kw = n_kv * LANES
    for h in range(n_kv):
        sl = slice(h * LANES, (h + 1) * LANES)
        kh = norm_rope(kv_ref[:, sl], gk)
        k32_ref[:, sl] = kh
        kn_ref[:, sl] = kh.astype(BF16)
    v = kv_ref[:, kw:2 * kw]
    v32_ref[...] = v
    vb_ref[...] = v.astype(BF16)


def _qk_prep(z, kv, gq, gk, rope, seq_len, n_heads, n_kv):
    t = z.shape[0]
    dq = n_heads * LANES
    dk = n_kv * LANES
    tm = 256
    in_specs = [pl.BlockSpec((tm, dq), lambda i: (i, 0)),
                pl.BlockSpec((tm, 2 * dk), lambda i: (i, 0)),
                pl.BlockSpec((1, LANES), lambda i: (0, 0)),
                pl.BlockSpec((1, LANES), lambda i: (0, 0))]
    args = [z, kv, gq, gk]
    if rope is not None:
        per_seq = seq_len // tm
        in_specs += [pl.BlockSpec((tm, LANES), lambda i: (i % per_seq, 0))] * 2
        args += list(rope)
    out_specs = [pl.BlockSpec((tm, dq), lambda i: (i, 0))] + [pl.BlockSpec((tm, dk), lambda i: (i, 0))] * 4
    out_shape = [jax.ShapeDtypeStruct((t, dq), BF16), jax.ShapeDtypeStruct((t, dk), BF16),
                 jax.ShapeDtypeStruct((t, dk), BF16), jax.ShapeDtypeStruct((t, dk), F32),
                 jax.ShapeDtypeStruct((t, dk), F32)]
    return pl.pallas_call(
        functools.partial(_qk_kernel, use_rope=rope is not None, n_heads=n_heads, n_kv=n_kv,
                          q_scale=LANES ** -0.5),
        grid=(t // tm,),
        in_specs=in_specs, out_specs=out_specs, out_shape=out_shape,
        compiler_params=_params(1, 32),
        name="qk_prep",
    )(*args)


def _attn_kernel(*refs, has_cache, group, chunks):
    if has_cache:
        q_ref, k_ref, v_ref, ck_ref, cv_ref, o_ref = refs
    else:
        q_ref, k_ref, v_ref, o_ref = refs
    tq = q_ref.shape[0]
    nt = (((1,), (1,)), ((), ()))
    if has_cache:
        ck = ck_ref[...].astype(BF16)
        cv = cv_ref[...].astype(BF16)
    per = group // chunks
    for c in range(chunks):
        heads = range(c * per, (c + 1) * per)
        q = jnp.concatenate([q_ref[:, g * LANES:(g + 1) * LANES] for g in heads], axis=0)
        s_own = lax.dot_general(q, k_ref[...], nt, preferred_element_type=F32)
        m = jnp.max(s_own, axis=-1, keepdims=True)
        if has_cache:
            s_ctx = lax.dot_general(q, ck, nt, preferred_element_type=F32)
            m = jnp.maximum(m, jnp.max(s_ctx, axis=-1, keepdims=True))
        p = jnp.exp(s_own - m)
        denom = jnp.sum(p, axis=-1, keepdims=True)
        acc = jnp.dot(p.astype(BF16), v_ref[...], preferred_element_type=F32)
        if has_cache:
            pc = jnp.exp(s_ctx - m)
            denom = denom + jnp.sum(pc, axis=-1, keepdims=True)
            acc = acc + jnp.dot(pc.astype(BF16), cv, preferred_element_type=F32)
        o = acc / denom
        for n, g in enumerate(heads):
            o_ref[:, g * LANES:(g + 1) * LANES] = o[n * tq:(n + 1) * tq].astype(o_ref.dtype)


def _attention(qn, kn, vb, cache, batch, seq_len, n_heads, n_kv, tq):
    t = qn.shape[0]
    group = n_heads // n_kv
    nq = seq_len // tq
    in_specs = [pl.BlockSpec((tq, group * LANES), lambda b, h, i: (b * nq + i, h)),
                pl.BlockSpec((seq_len, LANES), lambda b, h, i: (b, h)),
                pl.BlockSpec((seq_len, LANES), lambda b, h, i: (b, h))]
    args = [qn, kn, vb]
    if cache is not None:
        past = cache[0].shape[1]
        in_specs += [pl.BlockSpec((None, past, LANES), lambda b, h, i: (b, 0, h))] * 2
        args += list(cache)
    return pl.pallas_call(
        functools.partial(_attn_kernel, has_cache=cache is not None, group=group,
                          chunks=group if cache is not None else group // 2),
        grid=(batch, n_kv, nq),
        in_specs=in_specs,
        out_specs=pl.BlockSpec((tq, group * LANES), lambda b, h, i: (b * nq + i, h)),
        out_shape=jax.ShapeDtypeStruct((t, n_heads * LANES), BF16),
        compiler_params=_params(3, 40),
        name="attention",
    )(*args)


CONV_HALO = 16


def _conv_kernel(u_ref, b_ref, c_ref, up_ref, cp_ref, un_ref, cn_ref, w_ref, bias_ref, o_ref, *,
                 tiles_per_seq):
    tm = u_ref.shape[0]
    pos = pl.program_id(0) % tiles_per_seq
    cu = c_ref[...].astype(F32) * u_ref[...].astype(F32)
    halo_prev = (cp_ref[...].astype(F32) * up_ref[...].astype(F32))[CONV_HALO - 1:CONV_HALO, :]
    halo_next = (cn_ref[...].astype(F32) * un_ref[...].astype(F32))[0:1, :]
    halo_prev = jnp.where(pos == 0, 0.0, halo_prev)
    halo_next = jnp.where(pos == tiles_per_seq - 1, 0.0, halo_next)
    row = lax.broadcasted_iota(I32, cu.shape, 0)
    prev = jnp.where(row == 0, halo_prev, pltpu.roll(cu, 1, 0))
    nxt = jnp.where(row == tm - 1, halo_next, pltpu.roll(cu, tm - 1, 0))
    w = w_ref[...]
    conv = prev * w[0:1, :] + cu * w[1:2, :] + nxt * w[2:3, :] + bias_ref[...]
    o_ref[...] = (b_ref[...].astype(F32) * conv).astype(o_ref.dtype)


def _gated_conv(z, conv_w, conv_b, seq_len, col0):
    t = z.shape[0]
    dc = conv_w.shape[1]
    tm = 256
    hb = tm // CONV_HALO
    last = t // CONV_HALO - 1
    prev_map = lambda c: (lambda i: (jnp.maximum(i * hb - 1, 0), c))
    next_map = lambda c: (lambda i: (jnp.minimum((i + 1) * hb, last), c))
    in_specs = [pl.BlockSpec((tm, dc), lambda i: (i, col0)),
                pl.BlockSpec((tm, dc), lambda i: (i, col0 + 1)),
                pl.BlockSpec((tm, dc), lambda i: (i, col0 + 2)),
                pl.BlockSpec((CONV_HALO, dc), prev_map(col0)),
                pl.BlockSpec((CONV_HALO, dc), prev_map(col0 + 2)),
                pl.BlockSpec((CONV_HALO, dc), next_map(col0)),
                pl.BlockSpec((CONV_HALO, dc), next_map(col0 + 2)),
                pl.BlockSpec(conv_w.shape, lambda i: (0, 0)),
                pl.BlockSpec((1, dc), lambda i: (0, 0))]
    return pl.pallas_call(
        functools.partial(_conv_kernel, tiles_per_seq=seq_len // tm),
        grid=(t // tm,),
        in_specs=in_specs,
        out_specs=pl.BlockSpec((tm, dc), lambda i: (i, 0)),
        out_shape=jax.ShapeDtypeStruct((t, dc), BF16),
        compiler_params=_params(1, 32),
        name="gated_conv",
    )(z, z, z, z, z, z, z, conv_w, conv_b)


def _merge_kernel(cv_ref, at_ref, gc_ref, ga_ref, wc_ref, wa_ref, o_ref, wcb_ref, wab_ref):
    @pl.when(pl.program_id(1) == 0)
    def _():
        wcb_ref[...] = wc_ref[...].astype(BF16)
        wab_ref[...] = wa_ref[...].astype(BF16)

    conv_out = jnp.dot(cv_ref[...], wcb_ref[...], preferred_element_type=F32)
    attn_out = jnp.dot(at_ref[...], wab_ref[...], preferred_element_type=F32)
    merged = (jax.nn.sigmoid(gc_ref[...].astype(F32)) * conv_out
              + jax.nn.sigmoid(ga_ref[...].astype(F32)) * attn_out)
    o_ref[...] = merged.astype(o_ref.dtype)


def _merge(cvg, attn, z, w_conv_out, w_attn_out, gc_col0, ga_col0):
    t, dc = cvg.shape
    dq = attn.shape[1]
    d = w_conv_out.shape[1]
    tm, tn = 512, 512
    return pl.pallas_call(
        _merge_kernel,
        grid=(d // tn, t // tm),
        in_specs=[pl.BlockSpec((tm, dc), lambda j, i: (i, 0)),
                  pl.BlockSpec((tm, dq), lambda j, i: (i, 0)),
                  pl.BlockSpec((tm, tn), lambda j, i: (i, gc_col0 + j)),
                  pl.BlockSpec((tm, tn), lambda j, i: (i, ga_col0 + j)),
                  pl.BlockSpec((dc, tn), lambda j, i: (0, j)),
                  pl.BlockSpec((dq, tn), lambda j, i: (0, j))],
        out_specs=pl.BlockSpec((tm, tn), lambda j, i: (i, j)),
        out_shape=jax.ShapeDtypeStruct((t, d), BF16),
        scratch_shapes=[pltpu.VMEM((dc, tn), BF16), pltpu.VMEM((dq, tn), BF16)],
        compiler_params=_params(2, 40),
        name="merge",
    )(cvg, attn, z, z, w_conv_out, w_attn_out)


def _route(logits_t, bias_col, carry):
    n_exp, tm = logits_t.shape
    per = n_exp // N_GROUPS
    assert per == SUBLANES
    neg = -jnp.inf
    scores = jax.nn.sigmoid(logits_t)
    biased = scores + bias_col
    sub = lax.broadcasted_iota(I32, (per, tm), 0).astype(F32)
    xs = [biased[g * per:(g + 1) * per, :] for g in range(N_GROUPS)]
    sc = [scores[g * per:(g + 1) * per, :] for g in range(N_GROUPS)]
    ids = [sub + float(g * per) for g in range(N_GROUPS)]

    def colmax(a):
        return jnp.max(a, axis=0, keepdims=True)

    def colmin(a):
        return jnp.min(a, axis=0, keepdims=True)

    rows = []
    for g in range(N_GROUPS):
        m1 = colmax(xs[g])
        j1 = colmin(jnp.where(xs[g] == m1, sub, float(per)))
        m2 = colmax(jnp.where(sub == j1, neg, xs[g]))
        rows.append(m1 + m2)
    gs = jnp.concatenate(rows, axis=0)
    gsel = jnp.zeros_like(gs)
    for _ in range(TOPK_GROUPS):
        m = colmax(gs)
        j = colmin(jnp.where(gs == m, sub, float(N_GROUPS)))
        hit = sub == j
        gsel = jnp.where(hit, 1.0, gsel)
        gs = jnp.where(hit, neg, gs)
    masked = [jnp.where(gsel[g:g + 1, :] > 0.0, xs[g], neg) for g in range(N_GROUPS)]
    idx_rows, w_rows = [], []
    member = [jnp.zeros((per, tm), F32) for _ in range(N_GROUPS)]
    for _ in range(TOP_K):
        mm = masked[0]
        for g in range(1, N_GROUPS):
            mm = jnp.maximum(mm, masked[g])
        m = colmax(mm)
        idx = colmin(jnp.where(masked[0] == m, ids[0], float(n_exp)))
        for g in range(1, N_GROUPS):
            idx = jnp.minimum(idx, colmin(jnp.where(masked[g] == m, ids[g], float(n_exp))))
        wk = jnp.zeros_like(idx)
        for g in range(N_GROUPS):
            hit = ids[g] == idx
            wk = wk + jnp.sum(jnp.where(hit, sc[g], 0.0), axis=0, keepdims=True)
            masked[g] = jnp.where(hit, neg, masked[g])
            member[g] = jnp.where(hit, 1.0, member[g])
        idx_rows.append(idx)
        w_rows.append(wk)
    w = jnp.concatenate(w_rows, axis=0)
    w = w / jnp.sum(w, axis=0, keepdims=True) * ROUTED_SCALE

    earlier = (lax.broadcasted_iota(I32, (tm, tm), 0) < lax.broadcasted_iota(I32, (tm, tm), 1)).astype(BF16)
    before = jnp.dot(jnp.concatenate(member, axis=0).astype(BF16), earlier, preferred_element_type=F32)
    rank_rows = []
    for k in range(TOP_K):
        rk = jnp.zeros_like(idx_rows[k])
        for g in range(N_GROUPS):
            pos = before[g * per:(g + 1) * per, :] + carry[g]
            rk = rk + jnp.sum(jnp.where(ids[g] == idx_rows[k], pos, 0.0), axis=0, keepdims=True)
        rank_rows.append(rk)
    new_carry = [carry[g] + jnp.sum(member[g], axis=1, keepdims=True) for g in range(N_GROUPS)]
    idx = jnp.concatenate(idx_rows, axis=0).astype(I32)
    rank = jnp.concatenate(rank_rows, axis=0).astype(I32)
    return idx, w, rank, new_carry


def _out_kernel(mg_ref, x_ref, gate_ref, shift_ref, scale_ref, gpost_ref, gpre_ref, wo_ref, wr_ref,
                x1_ref, h2b_ref, h2p_ref, logit_ref, stage_ref, *, tiles_per_row):
    tm = x_ref.shape[0]
    r = pl.program_id(0) // tiles_per_row
    gate = gate_ref[pl.ds(r, 1), :]
    scale = 1.0 + scale_ref[pl.ds(r, 1), :]
    shift = shift_ref[pl.ds(r, 1), :]
    rows = tm // OUT_CHUNKS
    for c in range(OUT_CHUNKS):
        sl = pl.ds(c * rows, rows)
        mix = jnp.dot(mg_ref[sl, :], wo_ref[...], preferred_element_type=F32)
        x1 = x_ref[sl, :] + gate * _rms(mix, gpost_ref[...])
        x1_ref[sl, :] = x1
        h2 = _rms(x1, gpre_ref[...]) * scale + shift
        h2b_ref[sl, :] = h2.astype(BF16)
        _pack_rows(h2, h2p_ref, stage_ref, rows, row0=c * rows * SUBLANES, stage0=c * rows * 2 * SUBLANES)
        logit_ref[:, sl] = lax.dot_general(wr_ref[...], h2, (((1,), (1,)), ((), ())),
                                           preferred_element_type=F32, precision=lax.Precision.HIGHEST)


OUT_CHUNKS = 4


def _out_proj(merged, x, mod, g_post, g_pre, w_out_b, w_router_t):
    t, d = x.shape
    nb = mod.shape[0]
    n_exp = w_router_t.shape[0]
    tm = 512
    row = lambda i: (i, 0)
    fixed = lambda i: (0, 0)
    once = pl.Buffered(1)
    in_specs = [pl.BlockSpec((tm, d), row), pl.BlockSpec((tm, d), row),
                pl.BlockSpec((nb, d), lambda i: (0, 2)),
                pl.BlockSpec((nb, d), lambda i: (0, 3)),
                pl.BlockSpec((nb, d), lambda i: (0, 4)),
                pl.BlockSpec((1, d), fixed), pl.BlockSpec((1, d), fixed),
                pl.BlockSpec((d, d), fixed, pipeline_mode=once),
                pl.BlockSpec((n_exp, d), fixed, pipeline_mode=once)]
    out_specs = [pl.BlockSpec((tm, d), row), pl.BlockSpec((tm, d), row),
                 pl.BlockSpec((tm * SUBLANES, LANES), row),
                 pl.BlockSpec((n_exp, tm), lambda i: (0, i))]
    out_shape = [jax.ShapeDtypeStruct((t, d), F32), jax.ShapeDtypeStruct((t, d), BF16),
                 jax.ShapeDtypeStruct((t * SUBLANES, LANES), I32),
                 jax.ShapeDtypeStruct((n_exp, t), F32)]
    return pl.pallas_call(
        functools.partial(_out_kernel, tiles_per_row=t // nb // tm),
        grid=(t // tm,),
        in_specs=in_specs, out_specs=out_specs, out_shape=out_shape,
        scratch_shapes=[pltpu.VMEM((SUBLANES * 2 * tm, LANES), F32)],
        compiler_params=_params(1, 56),
        name="out_proj",
    )(merged, x, mod, mod, mod, g_post, g_pre, w_out_b, w_router_t)


def _router_kernel(logit_ref, br_ref, idx_ref, wsel_ref, rank_ref, cnt_ref):
    per = SUBLANES

    @pl.when(pl.program_id(0) == 0)
    def _():
        cnt_ref[...] = jnp.zeros_like(cnt_ref)

    carry = [cnt_ref[g * per:(g + 1) * per, 0:1] for g in range(N_GROUPS)]
    idx, w, rank, carry = _route(logit_ref[...], br_ref[...], carry)
    idx_ref[...] = idx
    wsel_ref[...] = w
    rank_ref[...] = rank
    for g in range(N_GROUPS):
        cnt_ref[g * per:(g + 1) * per, :] = jnp.broadcast_to(carry[g], (per, LANES))


def _router(logits_t, b_router_col):
    n_exp, t = logits_t.shape
    tr = 1024
    tile = lambda i: (0, i)
    fixed = lambda i: (0, 0)
    return pl.pallas_call(
        _router_kernel,
        grid=(t // tr,),
        in_specs=[pl.BlockSpec((n_exp, tr), tile), pl.BlockSpec((n_exp, 1), fixed)],
        out_specs=[pl.BlockSpec((TOP_K, tr), tile), pl.BlockSpec((TOP_K, tr), tile),
                   pl.BlockSpec((TOP_K, tr), tile), pl.BlockSpec((n_exp, LANES), fixed)],
        out_shape=[jax.ShapeDtypeStruct((TOP_K, t), I32), jax.ShapeDtypeStruct((TOP_K, t), F32),
                   jax.ShapeDtypeStruct((TOP_K, t), I32), jax.ShapeDtypeStruct((n_exp, LANES), F32)],
        compiler_params=_params(1, 32),
        name="router",
    )(logits_t, b_router_col)


def _dest_kernel(pstart_ref, idx_ref, rank_ref, o_ref):
    idx = idx_ref[...]
    acc = rank_ref[...]
    for e in range(pstart_ref.shape[0]):
        acc = acc + jnp.where(idx == e, pstart_ref[e], 0)
    o_ref[...] = acc


def _dest_slots(pstart, idx, rank):
    return pl.pallas_call(
        _dest_kernel,
        in_specs=[pl.BlockSpec(memory_space=pltpu.SMEM),
                  pl.BlockSpec(memory_space=pltpu.VMEM), pl.BlockSpec(memory_space=pltpu.VMEM)],
        out_specs=pl.BlockSpec(memory_space=pltpu.VMEM),
        out_shape=jax.ShapeDtypeStruct(idx.shape, I32),
        name="dest_slots",
    )(pstart, idx, rank)


def _slot_table_kernel(pad_lo_ref, pad_hi_ref, dest_ref, tab_ref):
    i = pl.program_id(0)
    tm = dest_ref.shape[2]

    @pl.when(i == 0)
    def _():
        def clear_range(e, carry):
            def clear(s, c):
                tab_ref[s] = 0
                return c
            return lax.fori_loop(pad_lo_ref[e], pad_hi_ref[e], clear, carry)
        lax.fori_loop(0, pad_lo_ref.shape[0], clear_range, 0)

    def fill(t, carry):
        for k in range(TOP_K):
            tab_ref[dest_ref[0, k, t]] = (i * tm + t) * TOP_K + k
        return carry

    lax.fori_loop(0, tm, fill, 0)


def _slot_table(pad_lo, pad_hi, dest3, n_slots):
    nt, _, tm = dest3.shape
    grid_spec = pltpu.PrefetchScalarGridSpec(
        num_scalar_prefetch=2,
        grid=(nt,),
        in_specs=[pl.BlockSpec((1, TOP_K, tm), lambda i, lo, hi: (i, 0, 0), memory_space=pltpu.SMEM)],
        out_specs=pl.BlockSpec(memory_space=pltpu.SMEM),
    )
    return pl.pallas_call(
        _slot_table_kernel,
        grid_spec=grid_spec,
        out_shape=jax.ShapeDtypeStruct((n_slots,), I32),
        compiler_params=_params(1, 16),
        name="slot_table",
    )(pad_lo, pad_hi, dest3)


GATHER_UNROLL = 16
GATHER_BLOCK = 1024
assert TOP_K == SUBLANES


def _gather_kernel(tab_ref, src_ref, o_ref):
    n = tab_ref.shape[2]

    def move(c, carry):
        for u in range(GATHER_UNROLL):
            r = c * GATHER_UNROLL + u
            src = pl.multiple_of(tab_ref[0, 0, r] & -SUBLANES, SUBLANES)
            dst = pl.multiple_of(r * SUBLANES, SUBLANES)
            o_ref[pl.ds(dst, SUBLANES), :] = src_ref[pl.ds(src, SUBLANES), :]
        return carry

    lax.fori_loop(0, n // GATHER_UNROLL, move, 0)


def _gather_rows(slot_tab, h2p, gb):
    n_slots = slot_tab.shape[0]
    nblk = n_slots // gb
    resident = h2p.size * h2p.dtype.itemsize
    return pl.pallas_call(
        _gather_kernel,
        grid=(nblk,),
        in_specs=[pl.BlockSpec((1, 1, gb), lambda b: (b, 0, 0), memory_space=pltpu.SMEM),
                  pl.BlockSpec(memory_space=pltpu.VMEM)],
        out_specs=pl.BlockSpec((gb * SUBLANES, LANES), lambda b: (b, 0)),
        out_shape=jax.ShapeDtypeStruct((n_slots * SUBLANES, LANES), I32),
        compiler_params=_params(1, resident // MIB + 8),
        name="dispatch_gather",
    )(slot_tab.reshape(nblk, 1, gb), h2p)


ROW_DMA_PRIORITY = 1
EXPERT_CHUNKS = 1
X_SLOTS = 6
Y_SLOTS = 3


def _expert_kernel(first_ref, nblk_ref, xs_hbm, wg_ref, wu_ref, wd_ref, yb_hbm,
                   xbuf_ref, obuf_ref, in_sem, out_sem, wgb_ref, wub_ref, wdb_ref, stage_in_ref, stage_out_ref,
                   *, bm):
    e = pl.program_id(0)
    last_e = pl.num_programs(0) - 1
    nb = nblk_ref[e]
    b0 = first_ref[e]
    total = first_ref[last_e] + nblk_ref[last_e]
    rows = bm * SUBLANES

    def x_copy(blk):
        slot = lax.rem(blk, X_SLOTS)
        return pltpu.make_async_copy(
            xs_hbm.at[pl.ds(pl.multiple_of(blk * rows, rows), rows), :],
            xbuf_ref.at[pl.ds(pl.multiple_of(slot * rows, rows), rows), :], in_sem.at[slot])

    def y_copy(blk):
        slot = lax.rem(blk, Y_SLOTS)
        return pltpu.make_async_copy(
            obuf_ref.at[pl.ds(pl.multiple_of(slot * rows, rows), rows), :],
            yb_hbm.at[pl.ds(pl.multiple_of(blk * rows, rows), rows), :], out_sem.at[slot])

    @pl.when(e == 0)
    def _():
        for j in range(X_SLOTS - 1):
            @pl.when(total > j)
            def _(j=j):
                x_copy(j).start(priority=ROW_DMA_PRIORITY)

    @pl.when(nb > 0)
    def _():
        wgb_ref[...] = wg_ref[...].astype(BF16)
        wub_ref[...] = wu_ref[...].astype(BF16)
        wdb_ref[...] = wd_ref[...].astype(BF16)

    def block(j, carry):
        blk = b0 + j
        xslot = lax.rem(blk, X_SLOTS)
        yblk = blk
        yslot = lax.rem(blk, Y_SLOTS)
        x_copy(blk).wait()

        ahead = blk + (X_SLOTS - 1)

        @pl.when(ahead < total)
        def _():
            x_copy(ahead).start(priority=ROW_DMA_PRIORITY)

        @pl.when(yblk >= Y_SLOTS)
        def _():
            y_copy(yblk - Y_SLOTS).wait()

        xbase = xslot * rows
        ybase = yslot * rows
        sub = bm // EXPERT_CHUNKS
        for c in range(EXPERT_CHUNKS):
            cbase = xbase + c * sub * SUBLANES
            obase = ybase + c * sub * SUBLANES
            los, his = [], []
            for s in range(SUBLANES):
                lo, hi = _unpack_word(xbuf_ref[pl.ds(cbase + s, sub, stride=SUBLANES), :], stage_in_ref,
                                      c * SUBLANES + s)
                los.append(lo.astype(BF16))
                his.append(hi.astype(BF16))
            x = jnp.concatenate(los + his, axis=1)
            g = jnp.dot(x, wgb_ref[...], preferred_element_type=F32)
            u = jnp.dot(x, wub_ref[...], preferred_element_type=F32)
            a = (_silu(g) * u).astype(BF16)
            y = jnp.dot(a, wdb_ref[...], preferred_element_type=F32)
            _pack_rows(y, obuf_ref, stage_out_ref, sub, row0=obase, stage0=c * sub * 2 * SUBLANES)
        y_copy(yblk).start()
        return carry

    lax.fori_loop(0, nb, block, 0)

    @pl.when(e == last_e)
    def _():
        for back in range(Y_SLOTS, 0, -1):
            @pl.when(total >= back)
            def _(back=back):
                y_copy(total - back).wait()

        obuf_ref[pl.ds(0, rows), :] = pltpu.bitcast(jnp.zeros((2 * rows, LANES), BF16), I32)

        def clear(j, carry):
            cp = pltpu.make_async_copy(obuf_ref.at[pl.ds(0, rows), :],
                                       yb_hbm.at[pl.ds(pl.multiple_of(j * rows, rows), rows), :], out_sem.at[0])
            cp.start()
            cp.wait()
            return carry

        lax.fori_loop(b0 + nb, yb_hbm.shape[0] // rows, clear, 0)


def _experts(first_block, n_blocks_e, xs, w_gate, w_up, w_down, bm):
    n_exp, d, de = w_gate.shape
    rows = bm * SUBLANES
    grid_spec = pltpu.PrefetchScalarGridSpec(
        num_scalar_prefetch=2,
        grid=(n_exp,),
        in_specs=[pl.BlockSpec(memory_space=pl.ANY),
                  pl.BlockSpec((None, d, de), lambda e, fb, nb: (e, 0, 0)),
                  pl.BlockSpec((None, d, de), lambda e, fb, nb: (e, 0, 0)),
                  pl.BlockSpec((None, de, d), lambda e, fb, nb: (e, 0, 0))],
        out_specs=pl.BlockSpec(memory_space=pl.ANY),
        scratch_shapes=[pltpu.VMEM((X_SLOTS * rows, LANES), I32), pltpu.VMEM((Y_SLOTS * rows, LANES), I32),
                        pltpu.SemaphoreType.DMA((X_SLOTS,)), pltpu.SemaphoreType.DMA((Y_SLOTS,)),
                        pltpu.VMEM((d, de), BF16), pltpu.VMEM((d, de), BF16), pltpu.VMEM((de, d), BF16),
                        pltpu.VMEM((SUBLANES * 2 * bm, LANES), F32), pltpu.VMEM((SUBLANES * 2 * bm, LANES), F32)],
    )
    return pl.pallas_call(
        functools.partial(_expert_kernel, bm=bm),
        grid_spec=grid_spec,
        out_shape=jax.ShapeDtypeStruct(xs.shape, I32),
        compiler_params=_params(1, 56),
        name="routed_experts",
    )(first_block, n_blocks_e, xs, w_gate, w_up, w_down)


def _final_kernel(dest_ref, w_ref, yb_hbm, x1_ref, h2_ref, gate_ref, gpost_ref, wsg_ref, wsu_ref,
                  wsd_ref, o_ref, buf_ref, sem, stage_ref, *, steps_per_row):
    tm = x1_ref.shape[0] // 2
    i = pl.program_id(0)
    last = pl.num_programs(0) - 1
    r = i // steps_per_row
    tile_rows = TOP_K * tm * SUBLANES
    gate = gate_ref[pl.ds(r, 1), :]

    def request(tile, slot, t, k):
        src = pl.multiple_of(dest_ref[tile, k, t], SUBLANES)
        dst = slot * tile_rows + (k * tm + t) * SUBLANES
        pltpu.make_async_copy(yb_hbm.at[pl.ds(src, SUBLANES), :], buf_ref.at[pl.ds(dst, SUBLANES), :],
                              sem.at[slot]).start(priority=k % 2)

    def wait_tile(slot):
        pltpu.make_async_copy(yb_hbm.at[pl.ds(0, tile_rows), :],
                              buf_ref.at[pl.ds(slot * tile_rows, tile_rows), :], sem.at[slot]).wait()

    @pl.when(i == 0)
    def _():
        def first(t, carry):
            for k in range(TOP_K):
                request(0, 0, t, k)
            return carry
        lax.fori_loop(0, tm, first, 0)

    pending = [(t, k) for t in range(tm) for k in range(TOP_K)]
    per_piece = 2 * len(pending) // (SUBLANES * TOP_K)

    def combine(half, slot, next_tile):
        rows = pl.ds(half * tm, tm)
        h = h2_ref[rows, :]
        g = jnp.dot(h, wsg_ref[...], preferred_element_type=F32)
        u = jnp.dot(h, wsu_ref[...], preferred_element_type=F32)
        shared = jnp.dot((_silu(g) * u).astype(BF16), wsd_ref[...], preferred_element_type=F32)
        w = w_ref[rows, :]
        base = slot * tile_rows
        los, his = [], []
        for s in range(SUBLANES):
            lo_acc = jnp.zeros((tm, LANES), F32)
            hi_acc = jnp.zeros((tm, LANES), F32)
            for k in range(TOP_K):
                lo, hi = _unpack_word(buf_ref[pl.ds(base + k * tm * SUBLANES + s, tm, stride=SUBLANES), :],
                                      stage_ref, s * TOP_K + k)
                wk = w[:, k:k + 1]
                lo_acc = lo_acc + wk * lo
                hi_acc = hi_acc + wk * hi
                n = s * TOP_K + k
                for t, kk in pending[n * per_piece:(n + 1) * per_piece]:
                    request(next_tile, 1 - slot, t, kk)
            los.append(lo_acc)
            his.append(hi_acc)
        ffn = jnp.concatenate(los + his, axis=1) + shared
        o_ref[rows, :] = x1_ref[rows, :] + gate * _rms(ffn, gpost_ref[...])

    wait_tile(0)
    combine(0, 0, 2 * i + 1)
    wait_tile(1)
    combine(1, 1, jnp.minimum(2 * i + 2, 2 * last + 1))

    @pl.when(i == last)
    def _():
        wait_tile(0)


def _final(dest8, w_tok, yb, x1, h2b, mod, g_post, ws_gate, ws_up, ws_down):
    t, d = x1.shape
    nb = mod.shape[0]
    ds_ = ws_gate.shape[1]
    tm = dest8.shape[2]
    n_steps = t // (2 * tm)
    row = lambda i, dst: (i, 0)
    fixed = lambda i, dst: (0, 0)
    grid_spec = pltpu.PrefetchScalarGridSpec(
        num_scalar_prefetch=1,
        grid=(n_steps,),
        in_specs=[pl.BlockSpec((2 * tm, TOP_K), row),
                  pl.BlockSpec(memory_space=pl.ANY),
                  pl.BlockSpec((2 * tm, d), row), pl.BlockSpec((2 * tm, d), row),
                  pl.BlockSpec((nb, d), lambda i, dst: (0, 5)),
                  pl.BlockSpec((1, d), fixed),
                  pl.BlockSpec((d, ds_), fixed), pl.BlockSpec((d, ds_), fixed), pl.BlockSpec((ds_, d), fixed)],
        out_specs=pl.BlockSpec((2 * tm, d), row),
        scratch_shapes=[pltpu.VMEM((2 * TOP_K * tm * SUBLANES, LANES), I32), pltpu.SemaphoreType.DMA((2,)),
                        pltpu.VMEM((SUBLANES * TOP_K * 2 * tm, LANES), F32)],
    )
    return pl.pallas_call(
        functools.partial(_final_kernel, steps_per_row=t // nb // (2 * tm)),
        grid_spec=grid_spec,
        out_shape=jax.ShapeDtypeStruct((t, d), F32),
        compiler_params=_params(1, 48),
        name="combine_final",
    )(dest8, w_tok, yb, x1, h2b, mod, g_post, ws_gate, ws_up, ws_down)


def _rope_tables(n_pos):
    half = LANES // 2
    pos = jnp.arange(n_pos)
    row = (pos // GRID_W).astype(F32)
    col = (pos % GRID_W).astype(F32)
    inv = ROPE_THETA ** (-jnp.arange(0, half, 2, dtype=F32) / half)
    ang = jnp.concatenate([row[:, None] * inv, col[:, None] * inv], axis=-1)
    cos = jnp.repeat(jnp.cos(ang), 2, axis=-1)
    sin = jnp.repeat(jnp.sin(ang), 2, axis=-1) * jnp.tile(jnp.array([-1.0, 1.0], F32), half)
    return cos, sin


def _sublayer1(x, mod, p, seq_len, rope, cache, n_heads, n_kv, tq):
    t = x.shape[0]
    h = _prenorm(x, mod, p['g_pre1'], 0, 1)
    z = _matmul_wcast(h, p['w_in'], _z_col_map, 9 * 1024, BF16)
    kv = _matmul_wcast(h, p['w_in'], lambda j: j + 5, 1024, F32)
    qn, kn, vb, k32, v32 = _qk_prep(z, kv, p['g_q'], p['g_k'], rope, seq_len, n_heads, n_kv)
    attn = _attention(qn, kn, vb, cache, t // seq_len, seq_len, n_heads, n_kv, tq)
    cvg = _gated_conv(z, p['conv_w'], p['conv_b'], seq_len, 6)
    merged = _merge(cvg, attn, z, p['w_conv_out'], p['w_attn_out'], 4, 8)
    x1, h2b, h2p, logits_t = _out_proj(merged, x, mod, p['g_post1'], p['g_pre2'], p['w_out_b'], p['w_router_t'])
    return x1, h2b, h2p, logits_t, k32, v32


def kernel(x_prompt, x_sample, cache_k, cache_v, c, c_ctx, w_mod, b_mod, g_pre1, w_in, conv_w, conv_b, g_q, g_k, w_conv_out, w_attn_out, w_out, g_post1, g_pre2, w_router, b_router, w_e_gate, w_e_up, w_e_down, w_s_gate, w_s_up, w_s_down, g_post2):
    batch, seq, d = x_prompt.shape
    dec_batch, dec_seq, _ = x_sample.shape
    depth = w_mod.shape[0]
    assert depth == 1
    past, n_kv, head_dim = cache_k.shape[2:]
    assert head_dim == LANES
    n_heads = w_attn_out.shape[1] // head_dim
    n_exp = w_router.shape[2]
    t_ctx, t_lat = batch * seq, dec_batch * dec_seq
    t_all = t_ctx + t_lat
    l = 0

    p = {
        'g_pre1': g_pre1[l][None], 'w_in': w_in[l], 'conv_w': conv_w[l], 'conv_b': conv_b[l][None],
        'g_q': g_q[l][None], 'g_k': g_k[l][None], 'w_conv_out': w_conv_out[l], 'w_attn_out': w_attn_out[l],
        'w_out_b': w_out[l].astype(BF16), 'g_post1': g_post1[l][None], 'g_pre2': g_pre2[l][None],
        'w_router_t': w_router[l].T,
    }
    cond = jnp.concatenate([c_ctx[None], c, jnp.zeros((SUBLANES - 1 - dec_batch, d), F32)], axis=0)
    mod = _modulation(cond, w_mod[l], b_mod[l][None])
    mod_ctx, mod_lat = mod[0:1], mod[1:1 + dec_batch]

    xc = x_prompt.reshape(t_ctx, d)
    xl = x_sample.reshape(t_lat, d)
    cache = (cache_k[:, l].reshape(dec_batch, past, n_kv * head_dim),
             cache_v[:, l].reshape(dec_batch, past, n_kv * head_dim))
    rope = _rope_tables(dec_seq)

    x1c, h2bc, h2pc, logc, k32, v32 = _sublayer1(xc, mod_ctx, p, seq, None, None, n_heads, n_kv, 256)
    x1l, h2bl, h2pl, logl, _, _ = _sublayer1(xl, mod_lat, p, dec_seq, rope, cache, n_heads, n_kv, 512)
    idx_all, w_all, rank_all, cnt = _router(jnp.concatenate([logc, logl], axis=1), b_router[l][:, None])

    bm = 256
    n_blocks = (t_all * TOP_K + n_exp * (bm - 1)) // bm + 1
    n_slots = n_blocks * bm
    blocks_e = jnp.floor((cnt[:, 0] + (bm - 1)) / bm)
    first_e = jnp.cumsum(blocks_e) - blocks_e
    pstart = (first_e * bm).astype(I32)

    tmf = 128
    w_tok = w_all.T
    dest = _dest_slots(pstart, idx_all, rank_all)
    dest3 = dest.reshape(TOP_K, t_all // tmf, tmf).transpose(1, 0, 2)
    used_end = jnp.sum(blocks_e, keepdims=True) * bm
    pad_lo = jnp.concatenate([first_e * bm + cnt[:, 0], used_end]).astype(I32)
    pad_hi = jnp.concatenate([(first_e + blocks_e) * bm, jnp.full((1,), float(n_slots), F32)]).astype(I32)
    slot_tab = _slot_table(pad_lo, pad_hi, dest3, n_slots)

    h2p = jnp.concatenate([h2pc, h2pl], axis=0)
    xs = _gather_rows(slot_tab, h2p, GATHER_BLOCK)
    yb = _experts(first_e.astype(I32), blocks_e.astype(I32), xs, w_e_gate[l], w_e_up[l], w_e_down[l], bm)

    ws = (w_s_gate[l].astype(BF16), w_s_up[l].astype(BF16), w_s_down[l].astype(BF16))

    def finish(lo, n, x1, h2b, mod_g):
        dest_g = dest3[lo // tmf:(lo + n) // tmf] * SUBLANES
        return _final(dest_g, w_tok[lo:lo + n], yb, x1, h2b, mod_g, g_post2[l][None], *ws)

    y_ctx = finish(0, t_ctx, x1c, h2bc, mod_ctx)
    y_lat = finish(t_ctx, t_lat, x1l, h2bl, mod_lat)

    new_k = k32.reshape(batch, 1, seq, n_kv, head_dim)
    new_v = v32.reshape(batch, 1, seq, n_kv, head_dim)
    return (y_ctx.reshape(batch, seq, d), y_lat.reshape(dec_batch, dec_seq, d), new_k, new_v)
```

```python
import functools

import jax
import jax.numpy as jnp
from jax import lax
from jax.experimental import pallas as pl
from jax.experimental.pallas import tpu as pltpu

GRID_W = 64
ROPE_THETA = 10000.0
N_GROUPS = 8
TOPK_GROUPS = 4
TOP_K = 8
ROUTED_SCALE = 2.5
EPS = 1e-6

LANES = 128
SUBLANES = 8
V7X_VMEM_BYTES = 64 * 1024 * 1024
MIB = 1024 * 1024

F32 = jnp.float32
BF16 = jnp.bfloat16
I32 = jnp.int32


def _params(n_grid, vmem_mib):
    assert vmem_mib * MIB < V7X_VMEM_BYTES
    return pltpu.CompilerParams(
        dimension_semantics=("arbitrary",) * n_grid, vmem_limit_bytes=vmem_mib * MIB)


def _silu(x):
    return x * jax.nn.sigmoid(x)


def _rms(x, g):
    return x * lax.rsqrt(jnp.mean(x * x, axis=-1, keepdims=True) + EPS) * g


def _pack_rows(val, out_ref, stage_ref, rows, row0=0, stage0=0):
    half = val.shape[1] // 2
    for s in range(half // LANES):
        base = stage0 + s * 2 * rows
        stage_ref[pl.ds(base, rows, stride=2), :] = val[:, s * LANES:(s + 1) * LANES]
        stage_ref[pl.ds(base + 1, rows, stride=2), :] = val[:, half + s * LANES:half + (s + 1) * LANES]
        pair = stage_ref[pl.ds(base, 2 * rows), :].astype(BF16)
        out_ref[pl.ds(row0 + s, rows, stride=SUBLANES), :] = pltpu.bitcast(pair, I32)


def _unpack_word(word, stage_ref, slot):
    rows = word.shape[0]
    base = slot * 2 * rows
    stage_ref[pl.ds(base, 2 * rows), :] = pltpu.bitcast(word, BF16).astype(F32)
    return stage_ref[pl.ds(base, rows, stride=2), :], stage_ref[pl.ds(base + 1, rows, stride=2), :]


def _mod_kernel(c_ref, w_ref, b_ref, o_ref):
    s = _silu(c_ref[...]).astype(BF16)
    o_ref[...] = jnp.dot(s, w_ref[...].astype(BF16), preferred_element_type=F32) + b_ref[...]


def _modulation(cond, w, b):
    rows, d = cond.shape
    n = w.shape[1]
    tn = 1024
    return pl.pallas_call(
        _mod_kernel,
        grid=(n // tn,),
        in_specs=[pl.BlockSpec((rows, d), lambda j: (0, 0)),
                  pl.BlockSpec((d, tn), lambda j: (0, j)),
                  pl.BlockSpec((1, tn), lambda j: (0, j))],
        out_specs=pl.BlockSpec((rows, tn), lambda j: (0, j)),
        out_shape=jax.ShapeDtypeStruct((rows, n), F32),
        compiler_params=_params(1, 40),
        name="modulation",
    )(cond, w, b)


def _prenorm_kernel(x_ref, shift_ref, scale_ref, g_ref, o_ref, *, tiles_per_row):
    r = pl.program_id(0) // tiles_per_row
    y = _rms(x_ref[...], g_ref[...])
    o_ref[...] = (y * (1.0 + scale_ref[pl.ds(r, 1), :]) + shift_ref[pl.ds(r, 1), :]).astype(o_ref.dtype)


def _prenorm(x, mod, g, shift_col, scale_col):
    t, d = x.shape
    nb = mod.shape[0]
    tm = 512
    return pl.pallas_call(
        functools.partial(_prenorm_kernel, tiles_per_row=t // nb // tm),
        grid=(t // tm,),
        in_specs=[pl.BlockSpec((tm, d), lambda i: (i, 0)),
                  pl.BlockSpec((nb, d), lambda i: (0, shift_col)),
                  pl.BlockSpec((nb, d), lambda i: (0, scale_col)),
                  pl.BlockSpec((1, d), lambda i: (0, 0))],
        out_specs=pl.BlockSpec((tm, d), lambda i: (i, 0)),
        out_shape=jax.ShapeDtypeStruct((t, d), BF16),
        compiler_params=_params(1, 32),
        name="prenorm",
    )(x, mod, mod, g)


def _mm_wcast_kernel(a_ref, w_ref, o_ref, wb_ref):
    @pl.when(pl.program_id(1) == 0)
    def _():
        wb_ref[...] = w_ref[...].astype(BF16)

    o_ref[...] = jnp.dot(a_ref[...], wb_ref[...], preferred_element_type=F32).astype(o_ref.dtype)


def _matmul_wcast(a, w, col_map, n_out, out_dtype):
    m, k = a.shape
    tn = 1024
    tm = 2048 if jnp.dtype(out_dtype).itemsize == 2 else 1024
    return pl.pallas_call(
        _mm_wcast_kernel,
        grid=(n_out // tn, m // tm),
        in_specs=[pl.BlockSpec((tm, k), lambda j, i: (i, 0)),
                  pl.BlockSpec((k, tn), lambda j, i: (0, col_map(j)))],
        out_specs=pl.BlockSpec((tm, tn), lambda j, i: (i, j)),
        out_shape=jax.ShapeDtypeStruct((m, n_out), out_dtype),
        scratch_shapes=[pltpu.VMEM((k, tn), BF16)],
        compiler_params=_params(2, 56),
        name="in_proj",
    )(a, w)


def _z_col_map(j):
    return jnp.where(j < 2, j + 3, jnp.where(j < 6, j + 4, j - 6))


def _qk_kernel(*refs, use_rope, n_heads, n_kv, q_scale):
    if use_rope:
        q_ref, kv_ref, gq_ref, gk_ref, cos_ref, sin_ref, qn_ref, kn_ref, vb_ref, k32_ref, v32_ref = refs
        cos = cos_ref[...]
        sin = sin_ref[...]
        even = lax.broadcasted_iota(I32, cos.shape, 1) % 2 == 0
    else:
        q_ref, kv_ref, gq_ref, gk_ref, qn_ref, kn_ref, vb_ref, k32_ref, v32_ref = refs

    def norm_rope(xh, g):
        y = _rms(xh, g)
        if use_rope:
            sw = jnp.where(even, pltpu.roll(y, LANES - 1, 1), pltpu.roll(y, 1, 1))
            y = y * cos + sw * sin
        return y

    gq = gq_ref[...]
    gk = gk_ref[...]
    for h in range(n_heads):
        sl = slice(h * LANES, (h + 1) * LANES)
        qn_ref[:, sl] = (norm_rope(q_ref[:, sl].astype(F32), gq) * q_scale).astype(BF16)
    kw = n_kv * LANES
    for h in range(n_kv):
        sl = slice(h * LANES, (h + 1) * LANES)
        kh = norm_rope(kv_ref[:, sl], gk)
        k32_ref[:, sl] = kh
        kn_ref[:, sl] = kh.astype(BF16)
    v = kv_ref[:, kw:2 * kw]
    v32_ref[...] = v
    vb_ref[...] = v.astype(BF16)


def _qk_prep(z, kv, gq, gk, rope, seq_len, n_heads, n_kv):
    t = z.shape[0]
    dq = n_heads * LANES
    dk = n_kv * LANES
    tm = 256
    in_specs = [pl.BlockSpec((tm, dq), lambda i: (i, 0)),
                pl.BlockSpec((tm, 2 * dk), lambda i: (i, 0)),
                pl.BlockSpec((1, LANES), lambda i: (0, 0)),
                pl.BlockSpec((1, LANES), lambda i: (0, 0))]
    args = [z, kv, gq, gk]
    if rope is not None:
        per_seq = seq_len // tm
        in_specs += [pl.BlockSpec((tm, LANES), lambda i: (i % per_seq, 0))] * 2
        args += list(rope)
    out_specs = [pl.BlockSpec((tm, dq), lambda i: (i, 0))] + [pl.BlockSpec((tm, dk), lambda i: (i, 0))] * 4
    out_shape = [jax.ShapeDtypeStruct((t, dq), BF16), jax.ShapeDtypeStruct((t, dk), BF16),
                 jax.ShapeDtypeStruct((t, dk), BF16), jax.ShapeDtypeStruct((t, dk), F32),
                 jax.ShapeDtypeStruct((t, dk), F32)]
    return pl.pallas_call(
        functools.partial(_qk_kernel, use_rope=rope is not None, n_heads=n_heads, n_kv=n_kv,
                          q_scale=LANES ** -0.5),
        grid=(t // tm,),
        in_specs=in_specs, out_specs=out_specs, out_shape=out_shape,
        compiler_params=_params(1, 32),
        name="qk_prep",
    )(*args)


def _attn_kernel(*refs, has_cache, group, chunks):
    if has_cache:
        q_ref, k_ref, v_ref, ck_ref, cv_ref, o_ref = refs
    else:
        q_ref, k_ref, v_ref, o_ref = refs
    tq = q_ref.shape[0]
    nt = (((1,), (1,)), ((), ()))
    if has_cache:
        ck = ck_ref[...].astype(BF16)
        cv = cv_ref[...].astype(BF16)
    per = group // chunks
    for c in range(chunks):
        heads = range(c * per, (c + 1) * per)
        q = jnp.concatenate([q_ref[:, g * LANES:(g + 1) * LANES] for g in heads], axis=0)
        s_own = lax.dot_general(q, k_ref[...], nt, preferred_element_type=F32)
        m = jnp.max(s_own, axis=-1, keepdims=True)
        if has_cache:
            s_ctx = lax.dot_general(q, ck, nt, preferred_element_type=F32)
            m = jnp.maximum(m, jnp.max(s_ctx, axis=-1, keepdims=True))
        p = jnp.exp(s_own - m)
        denom = jnp.sum(p, axis=-1, keepdims=True)
        acc = jnp.dot(p.astype(BF16), v_ref[...], preferred_element_type=F32)
        if has_cache:
            pc = jnp.exp(s_ctx - m)
            denom = denom + jnp.sum(pc, axis=-1, keepdims=True)
            acc = acc + jnp.dot(pc.astype(BF16), cv, preferred_element_type=F32)
        o = acc / denom
        for n, g in enumerate(heads):
            o_ref[:, g * LANES:(g + 1) * LANES] = o[n * tq:(n + 1) * tq].astype(o_ref.dtype)


def _attention(qn, kn, vb, cache, batch, seq_len, n_heads, n_kv, tq):
    t = qn.shape[0]
    group = n_heads // n_kv
    nq = seq_len // tq
    in_specs = [pl.BlockSpec((tq, group * LANES), lambda b, h, i: (b * nq + i, h)),
                pl.BlockSpec((seq_len, LANES), lambda b, h, i: (b, h)),
                pl.BlockSpec((seq_len, LANES), lambda b, h, i: (b, h))]
    args = [qn, kn, vb]
    if cache is not None:
        past = cache[0].shape[1]
        in_specs += [pl.BlockSpec((None, past, LANES), lambda b, h, i: (b, 0, h))] * 2
        args += list(cache)
    return pl.pallas_call(
        functools.partial(_attn_kernel, has_cache=cache is not None, group=group,
                          chunks=group if cache is not None else group // 2),
        grid=(batch, n_kv, nq),
        in_specs=in_specs,
        out_specs=pl.BlockSpec((tq, group * LANES), lambda b, h, i: (b * nq + i, h)),
        out_shape=jax.ShapeDtypeStruct((t, n_heads * LANES), BF16),
        compiler_params=_params(3, 40),
        name="attention",
    )(*args)


CONV_HALO = 16


def _conv_kernel(u_ref, b_ref, c_ref, up_ref, cp_ref, un_ref, cn_ref, w_ref, bias_ref, o_ref, *,
                 tiles_per_seq):
    tm = u_ref.shape[0]
    pos = pl.program_id(0) % tiles_per_seq
    cu = c_ref[...].astype(F32) * u_ref[...].astype(F32)
    halo_prev = (cp_ref[...].astype(F32) * up_ref[...].astype(F32))[CONV_HALO - 1:CONV_HALO, :]
    halo_next = (cn_ref[...].astype(F32) * un_ref[...].astype(F32))[0:1, :]
    halo_prev = jnp.where(pos == 0, 0.0, halo_prev)
    halo_next = jnp.where(pos == tiles_per_seq - 1, 0.0, halo_next)
    row = lax.broadcasted_iota(I32, cu.shape, 0)
    prev = jnp.where(row == 0, halo_prev, pltpu.roll(cu, 1, 0))
    nxt = jnp.where(row == tm - 1, halo_next, pltpu.roll(cu, tm - 1, 0))
    w = w_ref[...]
    conv = prev * w[0:1, :] + cu * w[1:2, :] + nxt * w[2:3, :] + bias_ref[...]
    o_ref[...] = (b_ref[...].astype(F32) * conv).astype(o_ref.dtype)


def _gated_conv(z, conv_w, conv_b, seq_len, col0):
    t = z.shape[0]
    dc = conv_w.shape[1]
    tm = 256
    hb = tm // CONV_HALO
    last = t // CONV_HALO - 1
    prev_map = lambda c: (lambda i: (jnp.maximum(i * hb - 1, 0), c))
    next_map = lambda c: (lambda i: (jnp.minimum((i + 1) * hb, last), c))
    in_specs = [pl.BlockSpec((tm, dc), lambda i: (i, col0)),
                pl.BlockSpec((tm, dc), lambda i: (i, col0 + 1)),
                pl.BlockSpec((tm, dc), lambda i: (i, col0 + 2)),
                pl.BlockSpec((CONV_HALO, dc), prev_map(col0)),
                pl.BlockSpec((CONV_HALO, dc), prev_map(col0 + 2)),
                pl.BlockSpec((CONV_HALO, dc), next_map(col0)),
                pl.BlockSpec((CONV_HALO, dc), next_map(col0 + 2)),
                pl.BlockSpec(conv_w.shape, lambda i: (0, 0)),
                pl.BlockSpec((1, dc), lambda i: (0, 0))]
    return pl.pallas_call(
        functools.partial(_conv_kernel, tiles_per_seq=seq_len // tm),
        grid=(t // tm,),
        in_specs=in_specs,
        out_specs=pl.BlockSpec((tm, dc), lambda i: (i, 0)),
        out_shape=jax.ShapeDtypeStruct((t, dc), BF16),
        compiler_params=_params(1, 32),
        name="gated_conv",
    )(z, z, z, z, z, z, z, conv_w, conv_b)


def _merge_kernel(cv_ref, at_ref, gc_ref, ga_ref, wc_ref, wa_ref, o_ref, wcb_ref, wab_ref):
    @pl.when(pl.program_id(1) == 0)
    def _():
        wcb_ref[...] = wc_ref[...].astype(BF16)
        wab_ref[...] = wa_ref[...].astype(BF16)

    conv_out = jnp.dot(cv_ref[...], wcb_ref[...], preferred_element_type=F32)
    attn_out = jnp.dot(at_ref[...], wab_ref[...], preferred_element_type=F32)
    merged = (jax.nn.sigmoid(gc_ref[...].astype(F32)) * conv_out
              + jax.nn.sigmoid(ga_ref[...].astype(F32)) * attn_out)
    o_ref[...] = merged.astype(o_ref.dtype)


def _merge(cvg, attn, z, w_conv_out, w_attn_out, gc_col0, ga_col0):
    t, dc = cvg.shape
    dq = attn.shape[1]
    d = w_conv_out.shape[1]
    tm, tn = 512, 512
    return pl.pallas_call(
        _merge_kernel,
        grid=(d // tn, t // tm),
        in_specs=[pl.BlockSpec((tm, dc), lambda j, i: (i, 0)),
                  pl.BlockSpec((tm, dq), lambda j, i: (i, 0)),
                  pl.BlockSpec((tm, tn), lambda j, i: (i, gc_col0 + j)),
                  pl.BlockSpec((tm, tn), lambda j, i: (i, ga_col0 + j)),
                  pl.BlockSpec((dc, tn), lambda j, i: (0, j)),
                  pl.BlockSpec((dq, tn), lambda j, i: (0, j))],
        out_specs=pl.BlockSpec((tm, tn), lambda j, i: (i, j)),
        out_shape=jax.ShapeDtypeStruct((t, d), BF16),
        scratch_shapes=[pltpu.VMEM((dc, tn), BF16), pltpu.VMEM((dq, tn), BF16)],
        compiler_params=_params(2, 40),
        name="merge",
    )(cvg, attn, z, z, w_conv_out, w_attn_out)


def _route(logits_t, bias_col, carry):
    n_exp, tm = logits_t.shape
    per = n_exp // N_GROUPS
    assert per == SUBLANES
    neg = -jnp.inf
    scores = jax.nn.sigmoid(logits_t)
    biased = scores + bias_col
    sub = lax.broadcasted_iota(I32, (per, tm), 0).astype(F32)
    xs = [biased[g * per:(g + 1) * per, :] for g in range(N_GROUPS)]
    sc = [scores[g * per:(g + 1) * per, :] for g in range(N_GROUPS)]
    ids = [sub + float(g * per) for g in range(N_GROUPS)]

    def colmax(a):
        return jnp.max(a, axis=0, keepdims=True)

    def colmin(a):
        return jnp.min(a, axis=0, keepdims=True)

    rows = []
    for g in range(N_GROUPS):
        m1 = colmax(xs[g])
        j1 = colmin(jnp.where(xs[g] == m1, sub, float(per)))
        m2 = colmax(jnp.where(sub == j1, neg, xs[g]))
        rows.append(m1 + m2)
    gs = jnp.concatenate(rows, axis=0)
    gsel = jnp.zeros_like(gs)
    for _ in range(TOPK_GROUPS):
        m = colmax(gs)
        j = colmin(jnp.where(gs == m, sub, float(N_GROUPS)))
        hit = sub == j
        gsel = jnp.where(hit, 1.0, gsel)
        gs = jnp.where(hit, neg, gs)
    masked = [jnp.where(gsel[g:g + 1, :] > 0.0, xs[g], neg) for g in range(N_GROUPS)]
    idx_rows, w_rows = [], []
    member = [jnp.zeros((per, tm), F32) for _ in range(N_GROUPS)]
    for _ in range(TOP_K):
        mm = masked[0]
        for g in range(1, N_GROUPS):
            mm = jnp.maximum(mm, masked[g])
        m = colmax(mm)
        idx = colmin(jnp.where(masked[0] == m, ids[0], float(n_exp)))
        for g in range(1, N_GROUPS):
            idx = jnp.minimum(idx, colmin(jnp.where(masked[g] == m, ids[g], float(n_exp))))
        wk = jnp.zeros_like(idx)
        for g in range(N_GROUPS):
            hit = ids[g] == idx
            wk = wk + jnp.sum(jnp.where(hit, sc[g], 0.0), axis=0, keepdims=True)
            masked[g] = jnp.where(hit, neg, masked[g])
            member[g] = jnp.where(hit, 1.0, member[g])
        idx_rows.append(idx)
        w_rows.append(wk)
    w = jnp.concatenate(w_rows, axis=0)
    w = w / jnp.sum(w, axis=0, keepdims=True) * ROUTED_SCALE

    earlier = (lax.broadcasted_iota(I32, (tm, tm), 0) < lax.broadcasted_iota(I32, (tm, tm), 1)).astype(BF16)
    before = jnp.dot(jnp.concatenate(member, axis=0).astype(BF16), earlier, preferred_element_type=F32)
    rank_rows = []
    for k in range(TOP_K):
        rk = jnp.zeros_like(idx_rows[k])
        for g in range(N_GROUPS):
            pos = before[g * per:(g + 1) * per, :] + carry[g]
            rk = rk + jnp.sum(jnp.where(ids[g] == idx_rows[k], pos, 0.0), axis=0, keepdims=True)
        rank_rows.append(rk)
    new_carry = [carry[g] + jnp.sum(member[g], axis=1, keepdims=True) for g in range(N_GROUPS)]
    idx = jnp.concatenate(idx_rows, axis=0).astype(I32)
    rank = jnp.concatenate(rank_rows, axis=0).astype(I32)
    return idx, w, rank, new_carry


def _out_kernel(mg_ref, x_ref, gate_ref, shift_ref, scale_ref, gpost_ref, gpre_ref, wo_ref, wr_ref,
                x1_ref, h2b_ref, h2p_ref, logit_ref, stage_ref, *, tiles_per_row):
    tm = x_ref.shape[0]
    r = pl.program_id(0) // tiles_per_row
    gate = gate_ref[pl.ds(r, 1), :]
    scale = 1.0 + scale_ref[pl.ds(r, 1), :]
    shift = shift_ref[pl.ds(r, 1), :]
    rows = tm // OUT_CHUNKS
    for c in range(OUT_CHUNKS):
        sl = pl.ds(c * rows, rows)
        mix = jnp.dot(mg_ref[sl, :], wo_ref[...], preferred_element_type=F32)
        x1 = x_ref[sl, :] + gate * _rms(mix, gpost_ref[...])
        x1_ref[sl, :] = x1
        h2 = _rms(x1, gpre_ref[...]) * scale + shift
        h2b_ref[sl, :] = h2.astype(BF16)
        _pack_rows(h2, h2p_ref, stage_ref, rows, row0=c * rows * SUBLANES, stage0=c * rows * 2 * SUBLANES)
        logit_ref[:, sl] = lax.dot_general(wr_ref[...], h2, (((1,), (1,)), ((), ())),
                                           preferred_element_type=F32, precision=lax.Precision.HIGHEST)


OUT_CHUNKS = 4


def _out_proj(merged, x, mod, g_post, g_pre, w_out_b, w_router_t):
    t, d = x.shape
    nb = mod.shape[0]
    n_exp = w_router_t.shape[0]
    tm = 512
    row = lambda i: (i, 0)
    fixed = lambda i: (0, 0)
    once = pl.Buffered(1)
    in_specs = [pl.BlockSpec((tm, d), row), pl.BlockSpec((tm, d), row),
                pl.BlockSpec((nb, d), lambda i: (0, 2)),
                pl.BlockSpec((nb, d), lambda i: (0, 3)),
                pl.BlockSpec((nb, d), lambda i: (0, 4)),
                pl.BlockSpec((1, d), fixed), pl.BlockSpec((1, d), fixed),
                pl.BlockSpec((d, d), fixed, pipeline_mode=once),
                pl.BlockSpec((n_exp, d), fixed, pipeline_mode=once)]
    out_specs = [pl.BlockSpec((tm, d), row), pl.BlockSpec((tm, d), row),
                 pl.BlockSpec((tm * SUBLANES, LANES), row),
                 pl.BlockSpec((n_exp, tm), lambda i: (0, i))]
    out_shape = [jax.ShapeDtypeStruct((t, d), F32), jax.ShapeDtypeStruct((t, d), BF16),
                 jax.ShapeDtypeStruct((t * SUBLANES, LANES), I32),
                 jax.ShapeDtypeStruct((n_exp, t), F32)]
    return pl.pallas_call(
        functools.partial(_out_kernel, tiles_per_row=t // nb // tm),
        grid=(t // tm,),
        in_specs=in_specs, out_specs=out_specs, out_shape=out_shape,
        scratch_shapes=[pltpu.VMEM((SUBLANES * 2 * tm, LANES), F32)],
        compiler_params=_params(1, 56),
        name="out_proj",
    )(merged, x, mod, mod, mod, g_post, g_pre, w_out_b, w_router_t)


def _router_kernel(logit_ref, br_ref, idx_ref, wsel_ref, rank_ref, cnt_ref):
    per = SUBLANES

    @pl.when(pl.program_id(0) == 0)
    def _():
        cnt_ref[...] = jnp.zeros_like(cnt_ref)

    carry = [cnt_ref[g * per:(g + 1) * per, 0:1] for g in range(N_GROUPS)]
    idx, w, rank, carry = _route(logit_ref[...], br_ref[...], carry)
    idx_ref[...] = idx
    wsel_ref[...] = w
    rank_ref[...] = rank
    for g in range(N_GROUPS):
        cnt_ref[g * per:(g + 1) * per, :] = jnp.broadcast_to(carry[g], (per, LANES))


def _router(logits_t, b_router_col):
    n_exp, t = logits_t.shape
    tr = 1024
    tile = lambda i: (0, i)
    fixed = lambda i: (0, 0)
    return pl.pallas_call(
        _router_kernel,
        grid=(t // tr,),
        in_specs=[pl.BlockSpec((n_exp, tr), tile), pl.BlockSpec((n_exp, 1), fixed)],
        out_specs=[pl.BlockSpec((TOP_K, tr), tile), pl.BlockSpec((TOP_K, tr), tile),
                   pl.BlockSpec((TOP_K, tr), tile), pl.BlockSpec((n_exp, LANES), fixed)],
        out_shape=[jax.ShapeDtypeStruct((TOP_K, t), I32), jax.ShapeDtypeStruct((TOP_K, t), F32),
                   jax.ShapeDtypeStruct((TOP_K, t), I32), jax.ShapeDtypeStruct((n_exp, LANES), F32)],
        compiler_params=_params(1, 32),
        name="router",
    )(logits_t, b_router_col)


def _dest_kernel(pstart_ref, idx_ref, rank_ref, o_ref):
    idx = idx_ref[...]
    acc = rank_ref[...]
    for e in range(pstart_ref.shape[0]):
        acc = acc + jnp.where(idx == e, pstart_ref[e], 0)
    o_ref[...] = acc


def _dest_slots(pstart, idx, rank):
    return pl.pallas_call(
        _dest_kernel,
        in_specs=[pl.BlockSpec(memory_space=pltpu.SMEM),
                  pl.BlockSpec(memory_space=pltpu.VMEM), pl.BlockSpec(memory_space=pltpu.VMEM)],
        out_specs=pl.BlockSpec(memory_space=pltpu.VMEM),
        out_shape=jax.ShapeDtypeStruct(idx.shape, I32),
        name="dest_slots",
    )(pstart, idx, rank)


def _slot_table_kernel(pad_lo_ref, pad_hi_ref, dest_ref, tab_ref):
    i = pl.program_id(0)
    tm = dest_ref.shape[2]

    @pl.when(i == 0)
    def _():
        def clear_range(e, carry):
            def clear(s, c):
                tab_ref[s] = 0
                return c
            return lax.fori_loop(pad_lo_ref[e], pad_hi_ref[e], clear, carry)
        lax.fori_loop(0, pad_lo_ref.shape[0], clear_range, 0)

    def fill(t, carry):
        for k in range(TOP_K):
            tab_ref[dest_ref[0, k, t]] = (i * tm + t) * TOP_K + k
        return carry

    lax.fori_loop(0, tm, fill, 0)


def _slot_table(pad_lo, pad_hi, dest3, n_slots):
    nt, _, tm = dest3.shape
    grid_spec = pltpu.PrefetchScalarGridSpec(
        num_scalar_prefetch=2,
        grid=(nt,),
        in_specs=[pl.BlockSpec((1, TOP_K, tm), lambda i, lo, hi: (i, 0, 0), memory_space=pltpu.SMEM)],
        out_specs=pl.BlockSpec(memory_space=pltpu.SMEM),
    )
    return pl.pallas_call(
        _slot_table_kernel,
        grid_spec=grid_spec,
        out_shape=jax.ShapeDtypeStruct((n_slots,), I32),
        compiler_params=_params(1, 16),
        name="slot_table",
    )(pad_lo, pad_hi, dest3)


GATHER_UNROLL = 32
GATHER_BLOCK = 1024
assert TOP_K == SUBLANES


def _gather_kernel(tab_ref, src_ref, o_ref):
    n = tab_ref.shape[2]

    def move(c, carry):
        for u in range(GATHER_UNROLL):
            r = c * GATHER_UNROLL + u
            src = pl.multiple_of(tab_ref[0, 0, r] & -SUBLANES, SUBLANES)
            dst = pl.multiple_of(r * SUBLANES, SUBLANES)
            o_ref[pl.ds(dst, SUBLANES), :] = src_ref[pl.ds(src, SUBLANES), :]
        return carry

    lax.fori_loop(0, n // GATHER_UNROLL, move, 0)


def _gather_rows(slot_tab, h2p, gb):
    n_slots = slot_tab.shape[0]
    nblk = n_slots // gb
    resident = h2p.size * h2p.dtype.itemsize
    return pl.pallas_call(
        _gather_kernel,
        grid=(nblk,),
        in_specs=[pl.BlockSpec((1, 1, gb), lambda b: (b, 0, 0), memory_space=pltpu.SMEM),
                  pl.BlockSpec(memory_space=pltpu.VMEM)],
        out_specs=pl.BlockSpec((gb * SUBLANES, LANES), lambda b: (b, 0)),
        out_shape=jax.ShapeDtypeStruct((n_slots * SUBLANES, LANES), I32),
        compiler_params=_params(1, resident // MIB + 8),
        name="dispatch_gather",
    )(slot_tab.reshape(nblk, 1, gb), h2p)


ROW_DMA_PRIORITY = 1
EXPERT_CHUNKS = 1
X_SLOTS = 8
Y_SLOTS = 3


def _expert_kernel(first_ref, nblk_ref, xs_hbm, wg_ref, wu_ref, wd_ref, yb_hbm,
                   xbuf_ref, obuf_ref, in_sem, out_sem, wgb_ref, wub_ref, wdb_ref, stage_in_ref, stage_out_ref,
                   *, bm):
    e = pl.program_id(0)
    last_e = pl.num_programs(0) - 1
    nb = nblk_ref[e]
    b0 = first_ref[e]
    total = first_ref[last_e] + nblk_ref[last_e]
    rows = bm * SUBLANES

    def x_copy(blk):
        slot = lax.rem(blk, X_SLOTS)
        return pltpu.make_async_copy(
            xs_hbm.at[pl.ds(pl.multiple_of(blk * rows, rows), rows), :],
            xbuf_ref.at[pl.ds(pl.multiple_of(slot * rows, rows), rows), :], in_sem.at[slot])

    def y_copy(blk):
        slot = lax.rem(blk, Y_SLOTS)
        return pltpu.make_async_copy(
            obuf_ref.at[pl.ds(pl.multiple_of(slot * rows, rows), rows), :],
            yb_hbm.at[pl.ds(pl.multiple_of(blk * rows, rows), rows), :], out_sem.at[slot])

    @pl.when(e == 0)
    def _():
        for j in range(X_SLOTS - 1):
            @pl.when(total > j)
            def _(j=j):
                x_copy(j).start(priority=ROW_DMA_PRIORITY)

    @pl.when(nb > 0)
    def _():
        wgb_ref[...] = wg_ref[...].astype(BF16)
        wub_ref[...] = wu_ref[...].astype(BF16)
        wdb_ref[...] = wd_ref[...].astype(BF16)

    def block(j, carry):
        blk = b0 + j
        xslot = lax.rem(blk, X_SLOTS)
        yblk = blk
        yslot = lax.rem(blk, Y_SLOTS)
        x_copy(blk).wait()

        ahead = blk + (X_SLOTS - 1)

        @pl.when(ahead < total)
        def _():
            x_copy(ahead).start(priority=ROW_DMA_PRIORITY)

        @pl.when(yblk >= Y_SLOTS)
        def _():
            y_copy(yblk - Y_SLOTS).wait()

        xbase = xslot * rows
        ybase = yslot * rows
        sub = bm // EXPERT_CHUNKS
        for c in range(EXPERT_CHUNKS):
            cbase = xbase + c * sub * SUBLANES
            obase = ybase + c * sub * SUBLANES
            los, his = [], []
            for s in range(SUBLANES):
                lo, hi = _unpack_word(xbuf_ref[pl.ds(cbase + s, sub, stride=SUBLANES), :], stage_in_ref,
                                      c * SUBLANES + s)
                los.append(lo.astype(BF16))
                his.append(hi.astype(BF16))
            x = jnp.concatenate(los + his, axis=1)
            g = jnp.dot(x, wgb_ref[...], preferred_element_type=F32)
            u = jnp.dot(x, wub_ref[...], preferred_element_type=F32)
            a = (_silu(g) * u).astype(BF16)
            y = jnp.dot(a, wdb_ref[...], preferred_element_type=F32)
            _pack_rows(y, obuf_ref, stage_out_ref, sub, row0=obase, stage0=c * sub * 2 * SUBLANES)
        y_copy(yblk).start()
        return carry

    lax.fori_loop(0, nb, block, 0)

    @pl.when(e == last_e)
    def _():
        for back in range(Y_SLOTS, 0, -1):
            @pl.when(total >= back)
            def _(back=back):
                y_copy(total - back).wait()

        obuf_ref[pl.ds(0, rows), :] = pltpu.bitcast(jnp.zeros((2 * rows, LANES), BF16), I32)

        def clear(j, carry):
            cp = pltpu.make_async_copy(obuf_ref.at[pl.ds(0, rows), :],
                                       yb_hbm.at[pl.ds(pl.multiple_of(j * rows, rows), rows), :], out_sem.at[0])
            cp.start()
            cp.wait()
            return carry

        lax.fori_loop(b0 + nb, yb_hbm.shape[0] // rows, clear, 0)


def _experts(first_block, n_blocks_e, xs, w_gate, w_up, w_down, bm):
    n_exp, d, de = w_gate.shape
    rows = bm * SUBLANES
    grid_spec = pltpu.PrefetchScalarGridSpec(
        num_scalar_prefetch=2,
        grid=(n_exp,),
        in_specs=[pl.BlockSpec(memory_space=pl.ANY),
                  pl.BlockSpec((None, d, de), lambda e, fb, nb: (e, 0, 0)),
                  pl.BlockSpec((None, d, de), lambda e, fb, nb: (e, 0, 0)),
                  pl.BlockSpec((None, de, d), lambda e, fb, nb: (e, 0, 0))],
        out_specs=pl.BlockSpec(memory_space=pl.ANY),
        scratch_shapes=[pltpu.VMEM((X_SLOTS * rows, LANES), I32), pltpu.VMEM((Y_SLOTS * rows, LANES), I32),
                        pltpu.SemaphoreType.DMA((X_SLOTS,)), pltpu.SemaphoreType.DMA((Y_SLOTS,)),
                        pltpu.VMEM((d, de), BF16), pltpu.VMEM((d, de), BF16), pltpu.VMEM((de, d), BF16),
                        pltpu.VMEM((SUBLANES * 2 * bm, LANES), F32), pltpu.VMEM((SUBLANES * 2 * bm, LANES), F32)],
    )
    return pl.pallas_call(
        functools.partial(_expert_kernel, bm=bm),
        grid_spec=grid_spec,
        out_shape=jax.ShapeDtypeStruct(xs.shape, I32),
        compiler_params=_params(1, 56),
        name="routed_experts",
    )(first_block, n_blocks_e, xs, w_gate, w_up, w_down)


def _final_kernel(dest_ref, w_ref, yb_hbm, x1_ref, h2_ref, gate_ref, gpost_ref, wsg_ref, wsu_ref,
                  wsd_ref, o_ref, buf_ref, sem, stage_ref, *, steps_per_row):
    tm = x1_ref.shape[0] // 2
    i = pl.program_id(0)
    last = pl.num_programs(0) - 1
    r = i // steps_per_row
    tile_rows = TOP_K * tm * SUBLANES
    gate = gate_ref[pl.ds(r, 1), :]

    def request(tile, slot, t, k):
        src = pl.multiple_of(dest_ref[tile, k, t], SUBLANES)
        dst = slot * tile_rows + (k * tm + t) * SUBLANES
        pltpu.make_async_copy(yb_hbm.at[pl.ds(src, SUBLANES), :], buf_ref.at[pl.ds(dst, SUBLANES), :],
                              sem.at[slot]).start(priority=k % 2)

    def wait_tile(slot):
        pltpu.make_async_copy(yb_hbm.at[pl.ds(0, tile_rows), :],
                              buf_ref.at[pl.ds(slot * tile_rows, tile_rows), :], sem.at[slot]).wait()

    @pl.when(i == 0)
    def _():
        def first(t, carry):
            for k in range(TOP_K):
                request(0, 0, t, k)
            return carry
        lax.fori_loop(0, tm, first, 0)

    pending = [(t, k) for t in range(tm) for k in range(TOP_K)]
    per_piece = 2 * len(pending) // (SUBLANES * TOP_K)

    def combine(half, slot, next_tile):
        rows = pl.ds(half * tm, tm)
        h = h2_ref[rows, :]
        g = jnp.dot(h, wsg_ref[...], preferred_element_type=F32)
        u = jnp.dot(h, wsu_ref[...], preferred_element_type=F32)
        shared = jnp.dot((_silu(g) * u).astype(BF16), wsd_ref[...], preferred_element_type=F32)
        w = w_ref[rows, :]
        base = slot * tile_rows
        los, his = [], []
        for s in range(SUBLANES):
            lo_acc = jnp.zeros((tm, LANES), F32)
            hi_acc = jnp.zeros((tm, LANES), F32)
            for k in range(TOP_K):
                lo, hi = _unpack_word(buf_ref[pl.ds(base + k * tm * SUBLANES + s, tm, stride=SUBLANES), :],
                                      stage_ref, s * TOP_K + k)
                wk = w[:, k:k + 1]
                lo_acc = lo_acc + wk * lo
                hi_acc = hi_acc + wk * hi
                n = s * TOP_K + k
                for t, kk in pending[n * per_piece:(n + 1) * per_piece]:
                    request(next_tile, 1 - slot, t, kk)
            los.append(lo_acc)
            his.append(hi_acc)
        ffn = jnp.concatenate(los + his, axis=1) + shared
        o_ref[rows, :] = x1_ref[rows, :] + gate * _rms(ffn, gpost_ref[...])

    wait_tile(0)
    combine(0, 0, 2 * i + 1)
    wait_tile(1)
    combine(1, 1, jnp.minimum(2 * i + 2, 2 * last + 1))

    @pl.when(i == last)
    def _():
        wait_tile(0)


def _final(dest8, w_tok, yb, x1, h2b, mod, g_post, ws_gate, ws_up, ws_down):
    t, d = x1.shape
    nb = mod.shape[0]
    ds_ = ws_gate.shape[1]
    tm = dest8.shape[2]
    n_steps = t // (2 * tm)
    row = lambda i, dst: (i, 0)
    fixed = lambda i, dst: (0, 0)
    grid_spec = pltpu.PrefetchScalarGridSpec(
        num_scalar_prefetch=1,
        grid=(n_steps,),
        in_specs=[pl.BlockSpec((2 * tm, TOP_K), row),
                  pl.BlockSpec(memory_space=pl.ANY),
                  pl.BlockSpec((2 * tm, d), row), pl.BlockSpec((2 * tm, d), row),
                  pl.BlockSpec((nb, d), lambda i, dst: (0, 5)),
                  pl.BlockSpec((1, d), fixed),
                  pl.BlockSpec((d, ds_), fixed), pl.BlockSpec((d, ds_), fixed), pl.BlockSpec((ds_, d), fixed)],
        out_specs=pl.BlockSpec((2 * tm, d), row),
        scratch_shapes=[pltpu.VMEM((2 * TOP_K * tm * SUBLANES, LANES), I32), pltpu.SemaphoreType.DMA((2,)),
                        pltpu.VMEM((SUBLANES * TOP_K * 2 * tm, LANES), F32)],
    )
    return pl.pallas_call(
        functools.partial(_final_kernel, steps_per_row=t // nb // (2 * tm)),
        grid_spec=grid_spec,
        out_shape=jax.ShapeDtypeStruct((t, d), F32),
        compiler_params=_params(1, 48),
        name="combine_final",
    )(dest8, w_tok, yb, x1, h2b, mod, g_post, ws_gate, ws_up, ws_down)


def _rope_tables(n_pos):
    half = LANES // 2
    pos = jnp.arange(n_pos)
    row = (pos // GRID_W).astype(F32)
    col = (pos % GRID_W).astype(F32)
    inv = ROPE_THETA ** (-jnp.arange(0, half, 2, dtype=F32) / half)
    ang = jnp.concatenate([row[:, None] * inv, col[:, None] * inv], axis=-1)
    cos = jnp.repeat(jnp.cos(ang), 2, axis=-1)
    sin = jnp.repeat(jnp.sin(ang), 2, axis=-1) * jnp.tile(jnp.array([-1.0, 1.0], F32), half)
    return cos, sin


def _sublayer1(x, mod, p, seq_len, rope, cache, n_heads, n_kv, tq):
    t = x.shape[0]
    h = _prenorm(x, mod, p['g_pre1'], 0, 1)
    z = _matmul_wcast(h, p['w_in'], _z_col_map, 9 * 1024, BF16)
    kv = _matmul_wcast(h, p['w_in'], lambda j: j + 5, 1024, F32)
    qn, kn, vb, k32, v32 = _qk_prep(z, kv, p['g_q'], p['g_k'], rope, seq_len, n_heads, n_kv)
    attn = _attention(qn, kn, vb, cache, t // seq_len, seq_len, n_heads, n_kv, tq)
    cvg = _gated_conv(z, p['conv_w'], p['conv_b'], seq_len, 6)
    merged = _merge(cvg, attn, z, p['w_conv_out'], p['w_attn_out'], 4, 8)
    x1, h2b, h2p, logits_t = _out_proj(merged, x, mod, p['g_post1'], p['g_pre2'], p['w_out_b'], p['w_router_t'])
    return x1, h2b, h2p, logits_t, k32, v32


def kernel(x_prompt, x_sample, cache_k, cache_v, c, c_ctx, w_mod, b_mod, g_pre1, w_in, conv_w, conv_b, g_q, g_k, w_conv_out, w_attn_out, w_out, g_post1, g_pre2, w_router, b_router, w_e_gate, w_e_up, w_e_down, w_s_gate, w_s_up, w_s_down, g_post2):
    batch, seq, d = x_prompt.shape
    dec_batch, dec_seq, _ = x_sample.shape
    depth = w_mod.shape[0]
    assert depth == 1
    past, n_kv, head_dim = cache_k.shape[2:]
    assert head_dim == LANES
    n_heads = w_attn_out.shape[1] // head_dim
    n_exp = w_router.shape[2]
    t_ctx, t_lat = batch * seq, dec_batch * dec_seq
    t_all = t_ctx + t_lat
    l = 0

    p = {
        'g_pre1': g_pre1[l][None], 'w_in': w_in[l], 'conv_w': conv_w[l], 'conv_b': conv_b[l][None],
        'g_q': g_q[l][None], 'g_k': g_k[l][None], 'w_conv_out': w_conv_out[l], 'w_attn_out': w_attn_out[l],
        'w_out_b': w_out[l].astype(BF16), 'g_post1': g_post1[l][None], 'g_pre2': g_pre2[l][None],
        'w_router_t': w_router[l].T,
    }
    cond = jnp.concatenate([c_ctx[None], c, jnp.zeros((SUBLANES - 1 - dec_batch, d), F32)], axis=0)
    mod = _modulation(cond, w_mod[l], b_mod[l][None])
    mod_ctx, mod_lat = mod[0:1], mod[1:1 + dec_batch]

    xc = x_prompt.reshape(t_ctx, d)
    xl = x_sample.reshape(t_lat, d)
    cache = (cache_k[:, l].reshape(dec_batch, past, n_kv * head_dim),
             cache_v[:, l].reshape(dec_batch, past, n_kv * head_dim))
    rope = _rope_tables(dec_seq)

    x1c, h2bc, h2pc, logc, k32, v32 = _sublayer1(xc, mod_ctx, p, seq, None, None, n_heads, n_kv, 256)
    x1l, h2bl, h2pl, logl, _, _ = _sublayer1(xl, mod_lat, p, dec_seq, rope, cache, n_heads, n_kv, 512)
    idx_all, w_all, rank_all, cnt = _router(jnp.concatenate([logc, logl], axis=1), b_router[l][:, None])

    bm = 256
    n_blocks = (t_all * TOP_K + n_exp * (bm - 1)) // bm + 1
    n_slots = n_blocks * bm
    blocks_e = jnp.floor((cnt[:, 0] + (bm - 1)) / bm)
    first_e = jnp.cumsum(blocks_e) - blocks_e
    pstart = (first_e * bm).astype(I32)

    tmf = 128
    w_tok = w_all.T
    dest = _dest_slots(pstart, idx_all, rank_all)
    dest3 = dest.reshape(TOP_K, t_all // tmf, tmf).transpose(1, 0, 2)
    used_end = jnp.sum(blocks_e, keepdims=True) * bm
    pad_lo = jnp.concatenate([first_e * bm + cnt[:, 0], used_end]).astype(I32)
    pad_hi = jnp.concatenate([(first_e + blocks_e) * bm, jnp.full((1,), float(n_slots), F32)]).astype(I32)
    slot_tab = _slot_table(pad_lo, pad_hi, dest3, n_slots)

    h2p = jnp.concatenate([h2pc, h2pl], axis=0)
    xs = _gather_rows(slot_tab, h2p, GATHER_BLOCK)
    yb = _experts(first_e.astype(I32), blocks_e.astype(I32), xs, w_e_gate[l], w_e_up[l], w_e_down[l], bm)

    ws = (w_s_gate[l].astype(BF16), w_s_up[l].astype(BF16), w_s_down[l].astype(BF16))

    def finish(lo, n, x1, h2b, mod_g):
        dest_g = dest3[lo // tmf:(lo + n) // tmf] * SUBLANES
        return _final(dest_g, w_tok[lo:lo + n], yb, x1, h2b, mod_g, g_post2[l][None], *ws)

    y_ctx = finish(0, t_ctx, x1c, h2bc, mod_ctx)
    y_lat = finish(t_ctx, t_lat, x1l, h2bl, mod_lat)

    new_k = k32.reshape(batch, 1, seq, n_kv, head_dim)
    new_v = v32.reshape(batch, 1, seq, n_kv, head_dim)
    return (y_ctx.reshape(batch, seq, d), y_lat.reshape(dec_batch, dec_seq, d), new_k, new_v)
```
